```python
import math
import jax, jax.numpy as jnp
from jax import lax
import numpy as np


D_MODEL = 1024
BATCH = 8
SEQ = 2048
DEPTH = 1
DEC_BATCH = 32
DEC_SEQ = 8
PAST_LEN = 16384
PAGE_SIZE = 128

D_LRU = D_MODEL
N_LRU_BLOCKS = 8
LRU_BLOCK = D_LRU // N_LRU_BLOCKS
CONV_WIDTH = 4
LRU_C = 8.0
HEAD_DIM = 128
HEADS_PER_GROUP = 4
WINDOWS = (128, 512, 2048)
DILATIONS = (1, 4, 16)
N_GROUPS = 3
N_ATTN_HEADS = N_GROUPS * HEADS_PER_GROUP
N_BACK = 128
N_SLOTS = N_BACK + 1
BLK = 128
D_QKV = N_ATTN_HEADS * HEAD_DIM
D_ATTN_OUT = HEADS_PER_GROUP * HEAD_DIM
ATTN_SCALE = HEAD_DIM ** -0.5
N_BUCKETS = 32
MAX_DISTANCE = 2048
NORM_EPS = 1e-6
D_IN = 2 * D_LRU + 3 * D_QKV + D_ATTN_OUT + 2 * D_MODEL

kernel_name = 'hybrid_rglru_dilated_swa_decode_step'


def rms_norm(x, g):
    xf = x.astype(jnp.float32)
    y = xf * lax.rsqrt(jnp.mean(xf * xf, axis=-1, keepdims=True) + NORM_EPS)
    return (y * g.astype(jnp.float32)).astype(x.dtype)


def t5_bucket(dist):
    max_exact = N_BUCKETS // 2
    df = jnp.maximum(dist, 1).astype(jnp.float32)
    large = max_exact + (jnp.log(df / max_exact) / math.log(MAX_DISTANCE / max_exact)
                         * (N_BUCKETS - max_exact)).astype(jnp.int32)
    return jnp.where(dist < max_exact, dist, jnp.minimum(large, N_BUCKETS - 1))


def slot_bias(rel_bias, g):
    dist = jnp.arange(N_SLOTS, dtype=jnp.int32) * DILATIONS[g]
    return rel_bias[t5_bucket(dist), g * HEADS_PER_GROUP:(g + 1) * HEADS_PER_GROUP].astype(jnp.float32)


def softmax_stats(s, valid):
    s = jnp.where(valid, s, -jnp.inf)
    m = jnp.max(s, axis=-1, keepdims=True)
    p = jnp.exp(s - m)
    den = jnp.sum(p, axis=-1, keepdims=True)
    return p / den, (m + jnp.log(den))[..., 0]


def dilated_group_prompt(q, k, v, bias, dil):
    B, S, H, HD = q.shape
    span = dil * BLK
    s_pad = -(-S // span) * span
    m_len = s_pad // dil
    n_blk = m_len // BLK

    def classes(t):
        t = jnp.pad(t, ((0, 0), (0, s_pad - S), (0, 0), (0, 0))).reshape(B, m_len, dil, H, HD)
        return jnp.moveaxis(t, 2, 1).reshape(B, dil, n_blk, BLK, H, HD)

    def with_prev(t):
        prev = jnp.pad(t, ((0, 0), (0, 0), (1, 0), (0, 0), (0, 0), (0, 0)))[:, :, :-1]
        return jnp.concatenate([prev, t], axis=3)

    qc = classes(q)
    kb = with_prev(classes(k))
    vb = with_prev(classes(v))
    qi = jnp.arange(BLK)[:, None]
    ki = jnp.arange(2 * BLK)[None, :]
    dist = qi + BLK - ki
    in_band = (dist >= 0) & (dist <= N_BACK)
    has_prev = jnp.arange(n_blk)[:, None, None] > 0
    valid = in_band[None] & (has_prev | (ki >= BLK)[None])
    b = jnp.transpose(bias[jnp.clip(dist, 0, N_BACK)], (2, 0, 1))
    s = jnp.einsum('brnqhe,brnkhe->brnhqk', qc, kb, preferred_element_type=jnp.float32)
    s = s * ATTN_SCALE + b
    p, lse = softmax_stats(s, valid[None, None, :, None])
    o = jnp.einsum('brnhqk,brnkhe->brnqhe', p.astype(vb.dtype), vb)
    o = jnp.moveaxis(o.reshape(B, dil, m_len, H, HD), 1, 2).reshape(B, s_pad, H, HD)[:, :S]
    lse = jnp.moveaxis(jnp.moveaxis(lse, 3, 4).reshape(B, dil, m_len, H), 1, 2).reshape(B, s_pad, H)[:, :S]
    return o, lse


def dilated_group_sample(q, k, v, kv_buf, bias, dil):
    L = kv_buf.shape[1]
    T = q.shape[1]
    kc = jnp.concatenate([kv_buf[:, :, 0], k.astype(kv_buf.dtype)], axis=1)
    vc = jnp.concatenate([kv_buf[:, :, 1], v.astype(kv_buf.dtype)], axis=1)
    idx = L + jnp.arange(T)[:, None] - dil * jnp.arange(N_SLOTS)[None, :]
    valid = idx >= 0
    idx = jnp.maximum(idx, 0)
    kg = kc[:, idx]
    vg = vc[:, idx]
    s = jnp.einsum('bthe,btjhe->bthj', q, kg, preferred_element_type=jnp.float32)
    s = s * ATTN_SCALE + bias.T
    p, lse = softmax_stats(s, valid[None, :, None, :])
    o = jnp.einsum('bthj,btjhe->bthe', p.astype(vg.dtype), vg)
    new_buf = jnp.stack([kc, vc], axis=2)[:, T:]
    return o, lse, new_buf


def causal_conv(x, buf, w, b):
    T = x.shape[1]
    xc = jnp.concatenate([buf.astype(x.dtype), x], axis=1)
    y = b + sum(xc[:, j:j + T] * w[j] for j in range(CONV_WIDTH))
    return y, xc[:, T:]


def rg_lru(x, h0, w_a, b_a, w_x, b_x, lam):
    B, T, _ = x.shape
    f32 = jnp.float32
    xf = x.astype(f32)
    xb = xf.reshape(B, T, N_LRU_BLOCKS, LRU_BLOCK)
    gate_a = jnp.einsum('btni,nij->btnj', xb, w_a.astype(f32)).reshape(B, T, D_LRU) + b_a.astype(f32)
    gate_x = jnp.einsum('btni,nij->btnj', xb, w_x.astype(f32)).reshape(B, T, D_LRU) + b_x.astype(f32)
    r = jax.nn.sigmoid(gate_a)
    i = jax.nn.sigmoid(gate_x)
    log_a = -LRU_C * r * jax.nn.softplus(-lam.astype(f32))
    a = jnp.exp(log_a)
    u = jnp.sqrt(-jnp.expm1(2.0 * log_a)) * (i * xf)

    def step(h, au):
        a_t, u_t = au
        h = a_t * h + u_t
        return h, h

    h_last, hs = lax.scan(step, h0.astype(f32), (jnp.moveaxis(a, 1, 0), jnp.moveaxis(u, 1, 0)))
    return jnp.moveaxis(hs, 0, 1), h_last


def mixer_layer(x, conv_buf, h0, kv_bufs, g_norm, w_in, b_merge, conv_w, conv_b,
                lru_w_a, lru_b_a, lru_w_x, lru_b_x, lru_lambda, g_q, g_k, rel_bias,
                w_lru_proj, w_attn_proj, w_out):
    B, T, _ = x.shape
    f32 = jnp.float32
    u = rms_norm(x, g_norm)
    z = jnp.einsum('btd,de->bte', u, w_in)
    o1 = D_LRU
    o2 = 2 * D_LRU
    o3 = o2 + D_QKV
    o4 = o3 + D_QKV
    o5 = o4 + D_QKV
    o6 = o5 + D_ATTN_OUT
    xa, ga, q, k, v, gb, gm = jnp.split(z, [o1, o2, o3, o4, o5, o6], axis=-1)

    xa, new_conv = causal_conv(xa, conv_buf, conv_w, conv_b)
    hs, h_last = rg_lru(xa, h0, lru_w_a, lru_b_a, lru_w_x, lru_b_x, lru_lambda)
    ya = jnp.einsum('bte,ed->btd', (hs * jax.nn.silu(ga.astype(f32))).astype(x.dtype), w_lru_proj)

    q = rms_norm(q.reshape(B, T, N_ATTN_HEADS, HEAD_DIM), g_q)
    k = rms_norm(k.reshape(B, T, N_ATTN_HEADS, HEAD_DIM), g_k)
    v = v.reshape(B, T, N_ATTN_HEADS, HEAD_DIM)
    outs, lses, new_bufs = [], [], []
    for g in range(N_GROUPS):
        sl = slice(g * HEADS_PER_GROUP, (g + 1) * HEADS_PER_GROUP)
        bias = slot_bias(rel_bias, g)
        qg, kg, vg = q[:, :, sl], k[:, :, sl], v[:, :, sl]
        if kv_bufs is None:
            o, lse = dilated_group_prompt(qg, kg, vg, bias, DILATIONS[g])
            keep = min(WINDOWS[g], T)
            nb = jnp.stack([kg, vg], axis=2)[:, T - keep:]
        else:
            o, lse, nb = dilated_group_sample(qg, kg, vg, kv_bufs[g], bias, DILATIONS[g])
        outs.append(o.astype(f32))
        lses.append(lse)
        new_bufs.append(nb)
    alpha = jax.nn.softmax(jnp.stack(lses, axis=0), axis=0)
    ob = jnp.sum(alpha[..., None] * jnp.stack(outs, axis=0), axis=0).reshape(B, T, D_ATTN_OUT)
    yb = jnp.einsum('bte,ed->btd', (ob * jax.nn.silu(gb.astype(f32))).astype(x.dtype), w_attn_proj)

    gates = jax.nn.sigmoid((gm + b_merge).astype(f32))
    merged = gates[..., :D_MODEL] * ya.astype(f32) + gates[..., D_MODEL:] * yb.astype(f32)
    y = x + jnp.einsum('btd,de->bte', merged.astype(x.dtype), w_out)
    return y, new_conv, h_last.astype(x.dtype), new_bufs


def setup_inputs(seed: int = 0) -> dict:
    key = jax.random.key(seed)
    ks = jax.random.split(key, 24)

    def nrm(k, shape, scale):
        return scale * jax.random.normal(k, shape, jnp.float32)

    kv_shape = lambda w: (DEPTH, DEC_BATCH, min(w, PAST_LEN), 2, HEADS_PER_GROUP, HEAD_DIM)
    u_lam = jax.random.uniform(ks[16], (DEPTH, D_LRU), jnp.float32, minval=0.9, maxval=0.999)
    s_lam = u_lam ** (1.0 / LRU_C)
    lru_lambda = jnp.log(s_lam) - jnp.log1p(-s_lam)
    return {
        'x_prompt': nrm(ks[0], (BATCH, SEQ, D_MODEL), 1.0),
        'x_sample': nrm(ks[1], (DEC_BATCH, DEC_SEQ, D_MODEL), 1.0),
        'cache_kv_w128': nrm(ks[2], kv_shape(WINDOWS[0]), 1.0),
        'cache_kv_w512': nrm(ks[3], kv_shape(WINDOWS[1]), 1.0),
        'cache_kv_w2048': nrm(ks[4], kv_shape(WINDOWS[2]), 1.0),
        'state_conv': nrm(ks[5], (DEPTH, DEC_BATCH, CONV_WIDTH - 1, D_LRU), 1.0),
        'state_h': nrm(ks[6], (DEPTH, DEC_BATCH, D_LRU), 0.5),
        'g_norm': 1.0 + nrm(ks[7], (DEPTH, D_MODEL), 0.05),
        'w_in': nrm(ks[8], (DEPTH, D_MODEL, D_IN), D_MODEL ** -0.5),
        'b_merge': nrm(ks[9], (DEPTH, 2 * D_MODEL), 0.1),
        'conv_w': nrm(ks[10], (DEPTH, CONV_WIDTH, D_LRU), CONV_WIDTH ** -0.5),
        'conv_b': nrm(ks[11], (DEPTH, D_LRU), 0.02),
        'lru_w_a': nrm(ks[12], (DEPTH, N_LRU_BLOCKS, LRU_BLOCK, LRU_BLOCK), LRU_BLOCK ** -0.5),
        'lru_b_a': nrm(ks[13], (DEPTH, D_LRU), 0.1),
        'lru_w_x': nrm(ks[14], (DEPTH, N_LRU_BLOCKS, LRU_BLOCK, LRU_BLOCK), LRU_BLOCK ** -0.5),
        'lru_b_x': nrm(ks[15], (DEPTH, D_LRU), 0.1),
        'lru_lambda': lru_lambda,
        'g_q': 1.0 + nrm(ks[17], (DEPTH, HEAD_DIM), 0.05),
        'g_k': 1.0 + nrm(ks[18], (DEPTH, HEAD_DIM), 0.05),
        'rel_bias': nrm(ks[19], (N_BUCKETS, N_ATTN_HEADS), 0.5),
        'w_lru_proj': nrm(ks[20], (DEPTH, D_LRU, D_MODEL), D_LRU ** -0.5),
        'w_attn_proj': nrm(ks[21], (DEPTH, D_ATTN_OUT, D_MODEL), D_ATTN_OUT ** -0.5),
        'w_out': nrm(ks[22], (DEPTH, D_MODEL, D_MODEL), D_MODEL ** -0.5),
    }


def reference(x_prompt, x_sample, cache_kv_w128, cache_kv_w512, cache_kv_w2048, state_conv, state_h,
              g_norm, w_in, b_merge, conv_w, conv_b, lru_w_a, lru_b_a, lru_w_x, lru_b_x, lru_lambda,
              g_q, g_k, rel_bias, w_lru_proj, w_attn_proj, w_out):
    yp, ys = x_prompt, x_sample
    bp = x_prompt.shape[0]
    kv_p = [[], [], []]
    kv_s = [[], [], []]
    conv_p, h_p, conv_s, h_s = [], [], [], []
    for l in range(DEPTH):
        params = (g_norm[l], w_in[l], b_merge[l], conv_w[l], conv_b[l], lru_w_a[l], lru_b_a[l],
                  lru_w_x[l], lru_b_x[l], lru_lambda[l], g_q[l], g_k[l], rel_bias,
                  w_lru_proj[l], w_attn_proj[l], w_out[l])
        yp, cp, hp, bufs_p = mixer_layer(
            yp, jnp.zeros((bp, CONV_WIDTH - 1, D_LRU), yp.dtype), jnp.zeros((bp, D_LRU), jnp.float32),
            None, *params)
        ys, cs, hsm, bufs_s = mixer_layer(
            ys, state_conv[l], state_h[l], (cache_kv_w128[l], cache_kv_w512[l], cache_kv_w2048[l]), *params)
        for g in range(N_GROUPS):
            kv_p[g].append(bufs_p[g])
            kv_s[g].append(bufs_s[g])
        conv_p.append(cp)
        h_p.append(hp)
        conv_s.append(cs)
        h_s.append(hsm)
    kv128_p, kv512_p, kv2048_p = (jnp.stack(t, axis=0) for t in kv_p)
    kv128_s, kv512_s, kv2048_s = (jnp.stack(t, axis=0) for t in kv_s)
    conv_prompt = jnp.stack(conv_p, axis=0)
    h_prompt = jnp.stack(h_p, axis=0)
    conv_sample = jnp.stack(conv_s, axis=0)
    h_sample = jnp.stack(h_s, axis=0)
    return (yp, ys, kv128_p, kv512_p, kv2048_p, conv_prompt, h_prompt,
            kv128_s, kv512_s, kv2048_s, conv_sample, h_sample)
```

```python
import functools
import math

import jax
import jax.numpy as jnp
from jax import lax
from jax.experimental import pallas as pl
from jax.experimental.pallas import tpu as pltpu

F32 = jnp.float32
BF16 = jnp.bfloat16

D_MODEL = 1024
D_LRU = 1024
N_LRU_BLOCKS = 8
LRU_BLOCK = D_LRU // N_LRU_BLOCKS
CONV_WIDTH = 4
LRU_C = 8.0
HEAD_DIM = 128
HEADS_PER_GROUP = 4
WINDOWS = (128, 512, 2048)
DILATIONS = (1, 4, 16)
N_GROUPS = 3
N_BACK = 128
D_QKV = N_GROUPS * HEADS_PER_GROUP * HEAD_DIM
D_GRP = HEADS_PER_GROUP * HEAD_DIM
ATTN_SCALE = HEAD_DIM ** -0.5
N_BUCKETS = 32
MAX_DISTANCE = 2048
NORM_EPS = 1e-6
NEG = -1e30
KEY_WIN = 256
VMEM_LIMIT = 60000 * 1024


def _rms(x, g):
    ms = jnp.mean(x * x, axis=-1, keepdims=True)
    return x * lax.rsqrt(ms + NORM_EPS) * g


def _sigmoid(x):
    return jax.nn.sigmoid(x)


def _dot(a, b):
    return jnp.dot(a, b, preferred_element_type=F32)


def _dot_nt(a, b):
    return lax.dot_general(a, b, (((1,), (1,)), ((), ())), preferred_element_type=F32)


def _bias_table(relb_ref, col, dist, valid):
    max_exact = N_BUCKETS // 2
    df = jnp.maximum(dist, 1).astype(F32)
    large = max_exact + (jnp.log(df / max_exact) / math.log(MAX_DISTANCE / max_exact)
                         * (N_BUCKETS - max_exact)).astype(jnp.int32)
    bucket = jnp.where(dist < max_exact, dist, jnp.minimum(large, N_BUCKETS - 1))
    out = jnp.zeros(dist.shape, F32)
    for b in range(N_BUCKETS):
        out = jnp.where(bucket == b, relb_ref[b, col], out)
    return jnp.where(valid, out, NEG)


def _softmax_pv(s, v):
    m = jnp.max(s, axis=-1, keepdims=True)
    p = jnp.exp(s - m)
    den = jnp.sum(p, axis=-1, keepdims=True)
    o = _dot(p.astype(BF16), v) / den
    return o, m + jnp.log(den)


def _merge_groups(o_refs, lse_refs):
    l0, l1, l2 = (r[...] for r in lse_refs)
    m = jnp.maximum(jnp.maximum(l0, l1), l2)
    e0, e1, e2 = jnp.exp(l0 - m), jnp.exp(l1 - m), jnp.exp(l2 - m)
    num = e0 * o_refs[0][...] + e1 * o_refs[1][...] + e2 * o_refs[2][...]
    return num / (e0 + e1 + e2)


def _tail(x, u, ya, ob, wg_ref, bm_ref, wap_ref, wout_ref):
    zg = _dot(u, wg_ref[...])
    gb = zg[:, :D_GRP]
    yb = _dot((ob * (gb * _sigmoid(gb))).astype(BF16), wap_ref[...])
    gates = _sigmoid(zg[:, D_GRP:] + bm_ref[...])
    merged = gates[:, :D_MODEL] * ya + gates[:, D_MODEL:] * yb
    return x + _dot(merged.astype(BF16), wout_ref[...])


def _qkv(u, wqkv_ref, gq_ref, gk_ref, store):
    for part, g_ref in enumerate((gq_ref, gk_ref, None)):
        z = _dot(u, wqkv_ref[:, part * D_QKV:(part + 1) * D_QKV])
        for h in range(D_QKV // HEAD_DIM):
            zh = z[:, h * HEAD_DIM:(h + 1) * HEAD_DIM]
            store(part, h, zh if g_ref is None else _rms(zh, g_ref[...]))


def _lru_body(x_ref, gn_ref, w_ref, cw_ref, cb_ref, wa_ref, ba_ref, wx_ref, bx_ref, lam_ref, wp_ref,
              conv0_ref, h0_ref, ya_ref, convo_ref, ho_ref, xs_ref, ga_ref, gs_ref, h_ref, *, B, tt):
    R = B * tt
    hist = (CONV_WIDTH - 1) * B

    @pl.when(pl.program_id(0) == 0)
    def _():
        xs_ref[0:hist, :] = conv0_ref[...]
        h_ref[...] = h0_ref[...]

    xt = jnp.concatenate([x_ref[:, t, :] for t in range(tt)], axis=0)
    u = _rms(xt, gn_ref[...]).astype(BF16)
    xs_ref[hist:hist + R, :] = _dot(u, w_ref[:, :D_LRU])
    ga_ref[...] = _dot(u, w_ref[:, D_LRU:])
    sp = jax.nn.softplus(-lam_ref[...])
    for n in range(N_LRU_BLOCKS):
        cs = slice(n * LRU_BLOCK, (n + 1) * LRU_BLOCK)
        y = cb_ref[:, cs] + sum(xs_ref[j * B:j * B + R, cs] * cw_ref[j:j + 1, cs] for j in range(CONV_WIDTH))
        yb = y.astype(BF16)
        r = _sigmoid(_dot(yb, wa_ref[n]) + ba_ref[:, cs])
        i = _sigmoid(_dot(yb, wx_ref[n]) + bx_ref[:, cs])
        log_a = -LRU_C * r * sp[:, cs]
        a = jnp.exp(log_a)
        th = jnp.tanh(log_a)
        uu = jnp.sqrt(-2.0 * th / (1.0 - th)) * (i * y)
        h = h_ref[:, cs]
        hs = []
        for t in range(tt):
            h = a[t * B:(t + 1) * B] * h + uu[t * B:(t + 1) * B]
            hs.append(h)
        h_ref[:, cs] = h
        g = ga_ref[:, cs]
        gs_ref[:, cs] = (jnp.concatenate(hs, axis=0) * (g * _sigmoid(g))).astype(BF16)
    ya = _dot(gs_ref[...], wp_ref[...])
    for t in range(tt):
        ya_ref[:, t, :] = ya[t * B:(t + 1) * B].astype(ya_ref.dtype)
    tail = xs_ref[R:R + hist, :]
    convo_ref[...] = tail
    xs_ref[0:hist, :] = tail
    ho_ref[...] = h_ref[...]


def _lru_call(x, conv0, h0, gn, w_lru, cw, cb, wa, ba, wx, bx, lam, wp, *, tt):
    B, S, _ = x.shape
    assert S % tt == 0 and tt >= CONV_WIDTH - 1 and tt % 8 == 0 and B % 8 == 0
    R = B * tt
    hist = (CONV_WIDTH - 1) * B
    const = lambda shape: pl.BlockSpec(shape, lambda i: (0,) * len(shape), pipeline_mode=pl.Buffered(1))
    return pl.pallas_call(
        functools.partial(_lru_body, B=B, tt=tt),
        grid=(S // tt,),
        in_specs=[
            pl.BlockSpec((B, tt, D_MODEL), lambda i: (0, i, 0)),
            const((1, D_MODEL)), const((D_MODEL, 2 * D_LRU)), const((CONV_WIDTH, D_LRU)), const((1, D_LRU)),
            const((N_LRU_BLOCKS, LRU_BLOCK, LRU_BLOCK)), const((1, D_LRU)),
            const((N_LRU_BLOCKS, LRU_BLOCK, LRU_BLOCK)), const((1, D_LRU)), const((1, D_LRU)),
            const((D_LRU, D_MODEL)), const((hist, D_LRU)), const((B, D_LRU)),
        ],
        out_specs=[
            pl.BlockSpec((B, tt, D_MODEL), lambda i: (0, i, 0)),
            pl.BlockSpec((hist, D_LRU), lambda i: (0, 0)),
            pl.BlockSpec((B, D_LRU), lambda i: (0, 0)),
        ],
        out_shape=[
            jax.ShapeDtypeStruct((B, S, D_MODEL), F32),
            jax.ShapeDtypeStruct((hist, D_LRU), F32),
            jax.ShapeDtypeStruct((B, D_LRU), F32),
        ],
        scratch_shapes=[
            pltpu.VMEM((hist + R, D_LRU), F32),
            pltpu.VMEM((R, D_LRU), F32),
            pltpu.VMEM((R, D_LRU), BF16),
            pltpu.VMEM((B, D_LRU), F32),
        ],
        compiler_params=pltpu.CompilerParams(dimension_semantics=("arbitrary",), vmem_limit_bytes=VMEM_LIMIT),
        name="lru_branch",
    )(x, gn, w_lru, cw, cb, wa, ba, wx, bx, lam, wp, conv0, h0)


def _attn_body(x_ref, ya_ref, gn_ref, wqkv_ref, wg_ref, gq_ref, gk_ref, bm_ref, relb_ref, wap_ref, wout_ref,
               y_ref, kv0_ref, kv1_ref, kv2_ref,
               zq_ref, zk_ref, zv_ref, qc0, qc1, qc2, hk0, hv0, hk1, hv1, hk2, hv2,
               o0, o1, o2, l0, l1, l2, bt0, bt1, bt2, *, S, tq):
    t = pl.program_id(1)
    nt = S // tq
    qcs, hks, hvs = (qc0, qc1, qc2), (hk0, hk1, hk2), (hv0, hv1, hv2)
    o_refs, l_refs, bts, kvs = (o0, o1, o2), (l0, l1, l2), (bt0, bt1, bt2), (kv0_ref, kv1_ref, kv2_ref)
    Qc = tuple(tq // d for d in DILATIONS)
    QB = tuple(min(q, N_BACK) for q in Qc)
    pad1 = KEY_WIN - QB[1]
    n2 = S // DILATIONS[2]

    @pl.when((pl.program_id(0) == 0) & (t == 0))
    def _():
        for ref in (hk0, hv0, hk1, hv1, hk2, hv2):
            ref[...] = jnp.zeros(ref.shape, ref.dtype)
        for g in (0, 1):
            a = lax.broadcasted_iota(jnp.int32, (QB[g], KEY_WIN), 0)
            c = lax.broadcasted_iota(jnp.int32, (QB[g], KEY_WIN), 1)
            j = a + (KEY_WIN - QB[g]) - c
            for h in range(HEADS_PER_GROUP):
                bts[g][h] = _bias_table(relb_ref, g * HEADS_PER_GROUP + h, j * DILATIONS[g], (j >= 0) & (j <= N_BACK))
        i = lax.broadcasted_iota(jnp.int32, (n2, n2), 0)
        c = lax.broadcasted_iota(jnp.int32, (n2, n2), 1)
        for h in range(HEADS_PER_GROUP):
            bt2[h] = _bias_table(relb_ref, 2 * HEADS_PER_GROUP + h, (i - c) * DILATIONS[2],
                                 (i - c >= 0) & (i - c <= N_BACK))

    x = x_ref[...]
    u = _rms(x, gn_ref[...]).astype(BF16)
    z_refs = (zq_ref, zk_ref, zv_ref)

    def store(part, head, value):
        z_refs[part][head] = value

    _qkv(u, wqkv_ref, gq_ref, gk_ref, store)

    for g in range(N_GROUPS):
        keep = min(WINDOWS[g], S)
        rb = min(tq, keep)

        @pl.when(t >= nt - keep // rb)
        def _(g=g, rb=rb):
            for part, z_ref in enumerate((zk_ref, zv_ref)):
                for h in range(HEADS_PER_GROUP):
                    c0 = part * D_GRP + h * HEAD_DIM
                    kvs[g][:, c0:c0 + HEAD_DIM] = z_ref[g * HEADS_PER_GROUP + h, tq - rb:tq, :]

    for g in range(N_GROUPS):
        d = DILATIONS[g]
        for r in range(d):
            rows = pl.ds(r, Qc[g], stride=d) if d > 1 else slice(None)
            if g == 0:
                dst = pl.ds(tq, tq)
            elif g == 1:
                dst = pl.ds(pl.multiple_of(pad1 + t * Qc[1], Qc[1]), Qc[1])
            else:
                dst = pl.ds(pl.multiple_of(t * Qc[2], Qc[2]), Qc[2])
            for h in range(HEADS_PER_GROUP):
                hs_ = slice(h * HEAD_DIM, (h + 1) * HEAD_DIM)
                gh = g * HEADS_PER_GROUP + h
                qcs[g][r, :, hs_] = zq_ref[gh, rows, :].astype(BF16)
                hks[g][r, dst, hs_] = zk_ref[gh, rows, :].astype(BF16)
                hvs[g][r, dst, hs_] = zv_ref[gh, rows, :].astype(BF16)

    col = lax.broadcasted_iota(jnp.int32, (1, KEY_WIN), 1)
    for g in range(N_GROUPS):
        d = DILATIONS[g]
        for r in range(d):
            for qb in range(Qc[g] // QB[g]):
                i0 = t * Qc[g] + qb * QB[g]
                if g == 0:
                    win = pl.ds(tq + qb * QB[0] + QB[0] - KEY_WIN, KEY_WIN)
                elif g == 1:
                    win = pl.ds(pl.multiple_of(i0, QB[1]), KEY_WIN)
                else:
                    win = slice(None)
                for h in range(HEADS_PER_GROUP):
                    hs_ = slice(h * HEAD_DIM, (h + 1) * HEAD_DIM)
                    q = qcs[g][r, qb * QB[g]:(qb + 1) * QB[g], hs_]
                    s = _dot_nt(q, hks[g][r, win, hs_]) * ATTN_SCALE
                    if g == 2:
                        s = s + bt2[h, pl.ds(pl.multiple_of(i0, QB[2]), QB[2]), :]
                    else:
                        s = jnp.where(col >= KEY_WIN - QB[g] - i0, s + bts[g][h], NEG)
                    o, lse = _softmax_pv(s, hvs[g][r, win, hs_])
                    rows = pl.ds(r + d * qb * QB[g], QB[g], stride=d) if d > 1 else pl.ds(qb * QB[g], QB[g])
                    o_refs[g][h, rows, :] = o
                    l_refs[g][h, rows, :] = jnp.broadcast_to(lse, (QB[g], HEAD_DIM))

    hk0[0, 0:tq, :] = hk0[0, tq:2 * tq, :]
    hv0[0, 0:tq, :] = hv0[0, tq:2 * tq, :]

    ob = _merge_groups(o_refs, l_refs)
    ob = jnp.concatenate([ob[h] for h in range(HEADS_PER_GROUP)], axis=1)
    y_ref[...] = _tail(x, u, ya_ref[...].astype(F32), ob, wg_ref, bm_ref, wap_ref, wout_ref)


def _attn_call(x, ya, gn, wqkv, wg, gq, gk, bmerge, relb, wap, wout, *, tq):
    B, S, _ = x.shape
    nt = S // tq
    assert S % tq == 0 and tq % (DILATIONS[2] * 16) == 0 and tq >= N_BACK
    assert S // DILATIONS[2] == N_BACK and tq // DILATIONS[1] <= N_BACK
    Qc = tuple(tq // d for d in DILATIONS)
    QB = tuple(min(q, N_BACK) for q in Qc)
    n2 = S // DILATIONS[2]
    const = lambda shape: pl.BlockSpec(shape, lambda b, t: (0,) * len(shape), pipeline_mode=pl.Buffered(1))
    row = pl.BlockSpec((None, tq, D_MODEL), lambda b, t: (b, t, 0))

    def kv_spec(g):
        keep = min(WINDOWS[g], S)
        rb = min(tq, keep)
        first = nt - keep // rb
        return pl.BlockSpec((None, rb, 2 * D_GRP), lambda b, t: (b, jnp.maximum(t - first, 0), 0))

    cls = lambda g, rows, dt: pltpu.VMEM((DILATIONS[g], rows, D_GRP), dt)
    scratch = [pltpu.VMEM((D_QKV // HEAD_DIM, tq, HEAD_DIM), F32)] * 3
    scratch += [cls(g, Qc[g], BF16) for g in range(N_GROUPS)]
    scratch += [cls(0, 2 * tq, BF16)] * 2 + [cls(1, KEY_WIN - QB[1] + S // DILATIONS[1], BF16)] * 2 + [cls(2, n2, BF16)] * 2
    scratch += [pltpu.VMEM((HEADS_PER_GROUP, tq, HEAD_DIM), F32)] * 6
    scratch += [pltpu.VMEM((HEADS_PER_GROUP, QB[0], KEY_WIN), F32), pltpu.VMEM((HEADS_PER_GROUP, QB[1], KEY_WIN), F32),
                pltpu.VMEM((HEADS_PER_GROUP, n2, n2), F32)]
    return pl.pallas_call(
        functools.partial(_attn_body, S=S, tq=tq),
        grid=(B, nt),
        in_specs=[row, row, const((1, D_MODEL)), const((D_MODEL, 3 * D_QKV)), const((D_MODEL, D_GRP + 2 * D_MODEL)),
                  const((1, HEAD_DIM)), const((1, HEAD_DIM)), const((1, 2 * D_MODEL)),
                  pl.BlockSpec(memory_space=pltpu.SMEM), const((D_GRP, D_MODEL)), const((D_MODEL, D_MODEL))],
        out_specs=[row, kv_spec(0), kv_spec(1), kv_spec(2)],
        out_shape=[jax.ShapeDtypeStruct((B, S, D_MODEL), F32)]
        + [jax.ShapeDtypeStruct((B, min(WINDOWS[g], S), 2 * D_GRP), F32) for g in range(N_GROUPS)],
        scratch_shapes=scratch,
        compiler_params=pltpu.CompilerParams(dimension_semantics=("arbitrary", "arbitrary"), vmem_limit_bytes=VMEM_LIMIT),
        name="attn_branch",
    )(x, ya, gn, wqkv, wg, gq, gk, bmerge, relb, wap, wout)


def _qkv_body(x_ref, gn_ref, wqkv_ref, gq_ref, gk_ref, zq_ref, zk_ref, zv_ref):
    u = _rms(x_ref[...], gn_ref[...]).astype(BF16)
    z_refs = (zq_ref, zk_ref, zv_ref)

    def store(part, head, value):
        z_refs[part][:, head * HEAD_DIM:(head + 1) * HEAD_DIM] = value

    _qkv(u, wqkv_ref, gq_ref, gk_ref, store)


def _qkv_call(x, gn, wqkv, gq, gk):
    M = x.shape[0]
    out = jax.ShapeDtypeStruct((M, D_QKV), F32)
    return pl.pallas_call(_qkv_body, out_shape=[out, out, out],
                          compiler_params=pltpu.CompilerParams(vmem_limit_bytes=VMEM_LIMIT),
                          name="decode_qkv")(x, gn, wqkv, gq, gk)


def _cache_attn_body(q_ref, k_ref, v_ref, cache_ref, relb_ref, new_ref, o_ref, lse_ref, kc_ref, vc_ref, bt_ref, *, g, T):
    L = cache_ref.shape[0]
    d = DILATIONS[g]
    W = L + HEAD_DIM

    @pl.when(pl.program_id(0) == 0)
    def _():
        kc_ref[L:W, :] = jnp.zeros((W - L, D_GRP), BF16)
        vc_ref[L:W, :] = jnp.zeros((W - L, D_GRP), BF16)
        tq = lax.broadcasted_iota(jnp.int32, (T, W), 0)
        p = lax.broadcasted_iota(jnp.int32, (T, W), 1)
        dist = L + tq - p
        valid = (dist >= 0) & ((dist & (d - 1)) == 0) & (dist <= N_BACK * d) & (p < L + T)
        for h in range(HEADS_PER_GROUP):
            bt_ref[h] = _bias_table(relb_ref, g * HEADS_PER_GROUP + h, dist, valid)

    k_new = k_ref[...]
    v_new = v_ref[...]
    new_ref[0:L - T, :] = cache_ref[T:L, :]
    new_ref[L - T:L, 0:D_GRP] = k_new
    new_ref[L - T:L, D_GRP:2 * D_GRP] = v_new
    kc_ref[0:L, :] = cache_ref[:, 0:D_GRP].astype(BF16)
    vc_ref[0:L, :] = cache_ref[:, D_GRP:2 * D_GRP].astype(BF16)
    kc_ref[L:L + 2 * T, :] = jnp.concatenate([k_new, jnp.zeros_like(k_new)], axis=0).astype(BF16)
    vc_ref[L:L + 2 * T, :] = jnp.concatenate([v_new, jnp.zeros_like(v_new)], axis=0).astype(BF16)
    for h in range(HEADS_PER_GROUP):
        hs_ = slice(h * HEAD_DIM, (h + 1) * HEAD_DIM)
        s = _dot_nt(q_ref[:, hs_].astype(BF16), kc_ref[:, hs_]) * ATTN_SCALE + bt_ref[h]
        o, lse = _softmax_pv(s, vc_ref[:, hs_])
        o_ref[:, hs_] = o
        lse_ref[:, hs_] = jnp.broadcast_to(lse, (T, HEAD_DIM))


def _cache_attn_call(q, k, v, cache, relb, *, g, T):
    DB, L, _ = cache.shape
    assert L == N_BACK * DILATIONS[g] and T % 8 == 0
    W = L + HEAD_DIM
    gcol = pl.BlockSpec((T, D_GRP), lambda b: (b, g))
    blk = pl.BlockSpec((None, L, 2 * D_GRP), lambda b: (b, 0, 0))
    orow = pl.BlockSpec((T, D_GRP), lambda b: (b, 0))
    return pl.pallas_call(
        functools.partial(_cache_attn_body, g=g, T=T),
        grid=(DB,),
        in_specs=[gcol, gcol, gcol, blk, pl.BlockSpec(memory_space=pltpu.SMEM)],
        out_specs=[blk, orow, orow],
        out_shape=[jax.ShapeDtypeStruct(cache.shape, F32), jax.ShapeDtypeStruct((DB * T, D_GRP), F32),
                   jax.ShapeDtypeStruct((DB * T, D_GRP), F32)],
        scratch_shapes=[pltpu.VMEM((W, D_GRP), BF16), pltpu.VMEM((W, D_GRP), BF16),
                        pltpu.VMEM((HEADS_PER_GROUP, T, W), F32)],
        compiler_params=pltpu.CompilerParams(dimension_semantics=("arbitrary",), vmem_limit_bytes=VMEM_LIMIT),
        name=f"cache_attn_w{L}",
    )(q, k, v, cache, relb)


def _tail_body(x_ref, ya_ref, gn_ref, wg_ref, bm_ref, wap_ref, wout_ref, o0, o1, o2, l0, l1, l2, y_ref):
    x = x_ref[...]
    u = _rms(x, gn_ref[...]).astype(BF16)
    ob = _merge_groups((o0, o1, o2), (l0, l1, l2))
    y_ref[...] = _tail(x, u, ya_ref[...], ob, wg_ref, bm_ref, wap_ref, wout_ref)


def _tail_call(x, ya, gn, wg, bmerge, wap, wout, os_, ls_):
    return pl.pallas_call(_tail_body, out_shape=jax.ShapeDtypeStruct(x.shape, F32),
                          compiler_params=pltpu.CompilerParams(vmem_limit_bytes=VMEM_LIMIT),
                          name="decode_tail")(x, ya, gn, wg, bmerge, wap, wout, *os_, *ls_)


def kernel(x_prompt, x_sample, cache_kv_w128, cache_kv_w512, cache_kv_w2048, state_conv, state_h, g_norm, w_in,
           b_merge, conv_w, conv_b, lru_w_a, lru_b_a, lru_w_x, lru_b_x, lru_lambda, g_q, g_k, rel_bias,
           w_lru_proj, w_attn_proj, w_out):
    assert w_in.shape[0] == 1, "single-layer step"
    B, S, _ = x_prompt.shape
    DB, T, _ = x_sample.shape
    o2 = 2 * D_LRU
    o5 = o2 + 3 * D_QKV
    w_lru = w_in[0, :, :o2].astype(BF16)
    wqkv = w_in[0, :, o2:o5].astype(BF16)
    wg = w_in[0, :, o5:].astype(BF16)
    wa, wx = lru_w_a[0].astype(BF16), lru_w_x[0].astype(BF16)
    wp, wap, wout = w_lru_proj[0].astype(BF16), w_attn_proj[0].astype(BF16), w_out[0].astype(BF16)
    lru_params = (g_norm, w_lru, conv_w[0], conv_b, wa, lru_b_a, wx, lru_b_x, lru_lambda, wp)
    hist = CONV_WIDTH - 1

    ya_p, conv_p, h_p = _lru_call(x_prompt, jnp.zeros((hist * B, D_LRU), F32), jnp.zeros((B, D_LRU), F32),
                                  *lru_params, tt=32)
    y_p, kv0_p, kv1_p, kv2_p = _attn_call(x_prompt, ya_p, g_norm, wqkv, wg, g_q, g_k, b_merge, rel_bias, wap, wout,
                                          tq=256)

    conv0_s = jnp.swapaxes(state_conv[0], 0, 1).reshape(hist * DB, D_LRU)
    ya_s, conv_s, h_s = _lru_call(x_sample, conv0_s, state_h[0], *lru_params, tt=T)
    xs2 = x_sample.reshape(DB * T, D_MODEL)
    zq, zk, zv = _qkv_call(xs2, g_norm, wqkv, g_q, g_k)
    news, os_, ls_ = [], [], []
    for g, cache in enumerate((cache_kv_w128, cache_kv_w512, cache_kv_w2048)):
        L = cache.shape[2]
        new, o, lse = _cache_attn_call(zq, zk, zv, cache.reshape(DB, L, 2 * D_GRP), rel_bias, g=g, T=T)
        news.append(new.reshape(cache.shape))
        os_.append(o)
        ls_.append(lse)
    y_s = _tail_call(xs2, ya_s.reshape(DB * T, D_MODEL), g_norm, wg, b_merge, wap, wout, os_, ls_)

    kv_shape = lambda a: a.reshape(1, a.shape[0], a.shape[1], 2, HEADS_PER_GROUP, HEAD_DIM)
    conv_out = lambda c, nb: jnp.swapaxes(c.reshape(hist, nb, D_LRU), 0, 1)[None]
    return (y_p, y_s.reshape(DB, T, D_MODEL), kv_shape(kv0_p), kv_shape(kv1_p), kv_shape(kv2_p),
            conv_out(conv_p, B), h_p[None], news[0], news[1], news[2], conv_out(conv_s, DB), h_s[None])
```

```python
import functools
import math

import jax
import jax.numpy as jnp
from jax import lax
from jax.experimental import pallas as pl
from jax.experimental.pallas import tpu as pltpu

F32 = jnp.float32
BF16 = jnp.bfloat16

D_MODEL = 1024
D_LRU = 1024
N_LRU_BLOCKS = 8
LRU_BLOCK = D_LRU // N_LRU_BLOCKS
CONV_WIDTH = 4
LRU_C = 8.0
HEAD_DIM = 128
HEADS_PER_GROUP = 4
WINDOWS = (128, 512, 2048)
DILATIONS = (1, 4, 16)
N_GROUPS = 3
N_BACK = 128
D_QKV = N_GROUPS * HEADS_PER_GROUP * HEAD_DIM
D_GRP = HEADS_PER_GROUP * HEAD_DIM
ATTN_SCALE = HEAD_DIM ** -0.5
N_BUCKETS = 32
MAX_DISTANCE = 2048
NORM_EPS = 1e-6
NEG = -1e30
KEY_WIN = 256
KV_ROWS = 2 * HEADS_PER_GROUP
VMEM_LIMIT = 60000 * 1024


def _rms(x, g):
    ms = jnp.mean(x * x, axis=-1, keepdims=True)
    return x * lax.rsqrt(ms + NORM_EPS) * g


def _sigmoid(x):
    return jax.nn.sigmoid(x)


def _dot(a, b):
    return jnp.dot(a, b, preferred_element_type=F32)


def _dot_nt(a, b):
    return lax.dot_general(a, b, (((1,), (1,)), ((), ())), preferred_element_type=F32)


def _bias_table(relb_ref, col, dist, valid):
    max_exact = N_BUCKETS // 2
    df = jnp.maximum(dist, 1).astype(F32)
    large = max_exact + (jnp.log(df / max_exact) / math.log(MAX_DISTANCE / max_exact)
                         * (N_BUCKETS - max_exact)).astype(jnp.int32)
    bucket = jnp.where(dist < max_exact, dist, jnp.minimum(large, N_BUCKETS - 1))
    out = jnp.zeros(dist.shape, F32)
    for b in range(N_BUCKETS):
        out = jnp.where(bucket == b, relb_ref[b, col], out)
    return jnp.where(valid, out, NEG)


def _softmax_pv(s, v):
    m = jnp.max(s, axis=-1, keepdims=True)
    p = jnp.exp(s - m)
    den = jnp.sum(p, axis=-1, keepdims=True)
    o = _dot(p.astype(BF16), v) / den
    return o, m + jnp.log(den)


def _merge_groups(o_refs, lse_refs):
    l0, l1, l2 = (r[...] for r in lse_refs)
    m = jnp.maximum(jnp.maximum(l0, l1), l2)
    e0, e1, e2 = jnp.exp(l0 - m), jnp.exp(l1 - m), jnp.exp(l2 - m)
    num = e0 * o_refs[0][...] + e1 * o_refs[1][...] + e2 * o_refs[2][...]
    return num / (e0 + e1 + e2)


def _tail(x, u, ya, ob, wg_ref, bm_ref, wap_ref, wout_ref):
    zg = _dot(u, wg_ref[...])
    gb = zg[:, :D_GRP]
    yb = _dot((ob * (gb * _sigmoid(gb))).astype(BF16), wap_ref[...])
    gates = _sigmoid(zg[:, D_GRP:] + bm_ref[...])
    merged = gates[:, :D_MODEL] * ya + gates[:, D_MODEL:] * yb
    return x + _dot(merged.astype(BF16), wout_ref[...])


def _qkv(u, wqkv_ref, gq_ref, gk_ref, store):
    for part, g_ref in enumerate((gq_ref, gk_ref, None)):
        z = _dot(u, wqkv_ref[:, part * D_QKV:(part + 1) * D_QKV])
        for h in range(D_QKV // HEAD_DIM):
            zh = z[:, h * HEAD_DIM:(h + 1) * HEAD_DIM]
            store(part, h, zh if g_ref is None else _rms(zh, g_ref[...]))


def _lru_body(x_ref, gn_ref, w_ref, cw_ref, cb_ref, wa_ref, ba_ref, wx_ref, bx_ref, lam_ref, wp_ref,
              conv0_ref, h0_ref, ya_ref, convo_ref, ho_ref, xs_ref, ga_ref, gs_ref, h_ref, *, B, tt):
    R = B * tt
    hist = (CONV_WIDTH - 1) * B

    @pl.when(pl.program_id(0) == 0)
    def _():
        xs_ref[0:hist, :] = conv0_ref[...]
        h_ref[...] = h0_ref[...]

    xt = jnp.concatenate([x_ref[:, t, :] for t in range(tt)], axis=0)
    u = _rms(xt, gn_ref[...]).astype(BF16)
    xs_ref[hist:hist + R, :] = _dot(u, w_ref[:, :D_LRU])
    ga_ref[...] = _dot(u, w_ref[:, D_LRU:])
    sp = jax.nn.softplus(-lam_ref[...])
    for n in range(N_LRU_BLOCKS):
        cs = slice(n * LRU_BLOCK, (n + 1) * LRU_BLOCK)
        y = cb_ref[:, cs] + sum(xs_ref[j * B:j * B + R, cs] * cw_ref[j:j + 1, cs] for j in range(CONV_WIDTH))
        yb = y.astype(BF16)
        r = _sigmoid(_dot(yb, wa_ref[n]) + ba_ref[:, cs])
        i = _sigmoid(_dot(yb, wx_ref[n]) + bx_ref[:, cs])
        log_a = -LRU_C * r * sp[:, cs]
        a = jnp.exp(log_a)
        th = jnp.tanh(log_a)
        uu = jnp.sqrt(-2.0 * th / (1.0 - th)) * (i * y)
        h = h_ref[:, cs]
        hs = []
        for t in range(tt):
            h = a[t * B:(t + 1) * B] * h + uu[t * B:(t + 1) * B]
            hs.append(h)
        h_ref[:, cs] = h
        g = ga_ref[:, cs]
        gs_ref[:, cs] = (jnp.concatenate(hs, axis=0) * (g * _sigmoid(g))).astype(BF16)
    ya = _dot(gs_ref[...], wp_ref[...])
    for t in range(tt):
        ya_ref[:, t, :] = ya[t * B:(t + 1) * B].astype(ya_ref.dtype)
    tail = xs_ref[R:R + hist, :]
    convo_ref[...] = tail
    xs_ref[0:hist, :] = tail
    ho_ref[...] = h_ref[...]


def _lru_call(x, conv0, h0, gn, w_lru, cw, cb, wa, ba, wx, bx, lam, wp, *, tt):
    B, S, _ = x.shape
    assert S % tt == 0 and tt >= CONV_WIDTH - 1 and tt % 8 == 0 and B % 8 == 0
    R = B * tt
    hist = (CONV_WIDTH - 1) * B
    const = lambda shape: pl.BlockSpec(shape, lambda i: (0,) * len(shape), pipeline_mode=pl.Buffered(1))
    return pl.pallas_call(
        functools.partial(_lru_body, B=B, tt=tt),
        grid=(S // tt,),
        in_specs=[
            pl.BlockSpec((B, tt, D_MODEL), lambda i: (0, i, 0)),
            const((1, D_MODEL)), const((D_MODEL, 2 * D_LRU)), const((CONV_WIDTH, D_LRU)), const((1, D_LRU)),
            const((N_LRU_BLOCKS, LRU_BLOCK, LRU_BLOCK)), const((1, D_LRU)),
            const((N_LRU_BLOCKS, LRU_BLOCK, LRU_BLOCK)), const((1, D_LRU)), const((1, D_LRU)),
            const((D_LRU, D_MODEL)), const((hist, D_LRU)), const((B, D_LRU)),
        ],
        out_specs=[
            pl.BlockSpec((B, tt, D_MODEL), lambda i: (0, i, 0)),
            pl.BlockSpec((hist, D_LRU), lambda i: (0, 0)),
            pl.BlockSpec((B, D_LRU), lambda i: (0, 0)),
        ],
        out_shape=[
            jax.ShapeDtypeStruct((B, S, D_MODEL), F32),
            jax.ShapeDtypeStruct((hist, D_LRU), F32),
            jax.ShapeDtypeStruct((B, D_LRU), F32),
        ],
        scratch_shapes=[
            pltpu.VMEM((hist + R, D_LRU), F32),
            pltpu.VMEM((R, D_LRU), F32),
            pltpu.VMEM((R, D_LRU), BF16),
            pltpu.VMEM((B, D_LRU), F32),
        ],
        compiler_params=pltpu.CompilerParams(dimension_semantics=("arbitrary",), vmem_limit_bytes=VMEM_LIMIT),
        name="lru_branch",
    )(x, gn, w_lru, cw, cb, wa, ba, wx, bx, lam, wp, conv0, h0)


def _attn_body(x_ref, ya_ref, gn_ref, wqkv_ref, wg_ref, gq_ref, gk_ref, bm_ref, relb_ref, wap_ref, wout_ref,
               y_ref, kv0_ref, kv1_ref, kv2_ref,
               zq_ref, zk_ref, zv_ref, qc0, qc1, qc2, hk0, hv0, hk1, hv1, hk2, hv2,
               o0, o1, o2, l0, l1, l2, bt0, bt1, bt2, *, S, tq):
    t = pl.program_id(1)
    nt = S // tq
    qcs, hks, hvs = (qc0, qc1, qc2), (hk0, hk1, hk2), (hv0, hv1, hv2)
    o_refs, l_refs, bts, kvs = (o0, o1, o2), (l0, l1, l2), (bt0, bt1, bt2), (kv0_ref, kv1_ref, kv2_ref)
    Qc = tuple(tq // d for d in DILATIONS)
    QB = tuple(min(q, N_BACK) for q in Qc)
    pad1 = KEY_WIN - QB[1]
    n2 = S // DILATIONS[2]

    @pl.when((pl.program_id(0) == 0) & (t == 0))
    def _():
        for ref in (hk0, hv0, hk1, hv1, hk2, hv2):
            ref[...] = jnp.zeros(ref.shape, ref.dtype)
        for g in (0, 1):
            a = lax.broadcasted_iota(jnp.int32, (QB[g], KEY_WIN), 0)
            c = lax.broadcasted_iota(jnp.int32, (QB[g], KEY_WIN), 1)
            j = a + (KEY_WIN - QB[g]) - c
            for h in range(HEADS_PER_GROUP):
                bts[g][h] = _bias_table(relb_ref, g * HEADS_PER_GROUP + h, j * DILATIONS[g], (j >= 0) & (j <= N_BACK))
        i = lax.broadcasted_iota(jnp.int32, (n2, n2), 0)
        c = lax.broadcasted_iota(jnp.int32, (n2, n2), 1)
        for h in range(HEADS_PER_GROUP):
            bt2[h] = _bias_table(relb_ref, 2 * HEADS_PER_GROUP + h, (i - c) * DILATIONS[2],
                                 (i - c >= 0) & (i - c <= N_BACK))

    x = x_ref[...]
    u = _rms(x, gn_ref[...]).astype(BF16)
    z_refs = (zq_ref, zk_ref, zv_ref)

    def store(part, head, value):
        z_refs[part][head] = value

    _qkv(u, wqkv_ref, gq_ref, gk_ref, store)

    for g in range(N_GROUPS):
        keep = min(WINDOWS[g], S)
        rb = min(tq, keep)

        @pl.when(t >= nt - keep // rb)
        def _(g=g, rb=rb):
            for part, z_ref in enumerate((zk_ref, zv_ref)):
                for h in range(HEADS_PER_GROUP):
                    dst = pl.ds(part * HEADS_PER_GROUP + h, rb, stride=KV_ROWS)
                    kvs[g][dst, :] = z_ref[g * HEADS_PER_GROUP + h, tq - rb:tq, :]

    for g in range(N_GROUPS):
        d = DILATIONS[g]
        for r in range(d):
            rows = pl.ds(r, Qc[g], stride=d) if d > 1 else slice(None)
            if g == 0:
                dst = pl.ds(tq, tq)
            elif g == 1:
                dst = pl.ds(pl.multiple_of(pad1 + t * Qc[1], Qc[1]), Qc[1])
            else:
                dst = pl.ds(pl.multiple_of(t * Qc[2], Qc[2]), Qc[2])
            for h in range(HEADS_PER_GROUP):
                hs_ = slice(h * HEAD_DIM, (h + 1) * HEAD_DIM)
                gh = g * HEADS_PER_GROUP + h
                qcs[g][r, :, hs_] = zq_ref[gh, rows, :].astype(BF16)
                hks[g][r, dst, hs_] = zk_ref[gh, rows, :].astype(BF16)
                hvs[g][r, dst, hs_] = zv_ref[gh, rows, :].astype(BF16)

    col = lax.broadcasted_iota(jnp.int32, (1, KEY_WIN), 1)
    for g in range(N_GROUPS):
        d = DILATIONS[g]
        for r in range(d):
            for qb in range(Qc[g] // QB[g]):
                i0 = t * Qc[g] + qb * QB[g]
                if g == 0:
                    win = pl.ds(tq + qb * QB[0] + QB[0] - KEY_WIN, KEY_WIN)
                elif g == 1:
                    win = pl.ds(pl.multiple_of(i0, QB[1]), KEY_WIN)
                else:
                    win = slice(None)
                for h in range(HEADS_PER_GROUP):
                    hs_ = slice(h * HEAD_DIM, (h + 1) * HEAD_DIM)
                    q = qcs[g][r, qb * QB[g]:(qb + 1) * QB[g], hs_]
                    s = _dot_nt(q, hks[g][r, win, hs_]) * ATTN_SCALE
                    if g == 2:
                        s = s + bt2[h, pl.ds(pl.multiple_of(i0, QB[2]), QB[2]), :]
                    else:
                        s = jnp.where(col >= KEY_WIN - QB[g] - i0, s + bts[g][h], NEG)
                    o, lse = _softmax_pv(s, hvs[g][r, win, hs_])
                    rows = pl.ds(r + d * qb * QB[g], QB[g], stride=d) if d > 1 else pl.ds(qb * QB[g], QB[g])
                    o_refs[g][h, rows, :] = o
                    l_refs[g][h, rows, :] = jnp.broadcast_to(lse, (QB[g], HEAD_DIM))

    hk0[0, 0:tq, :] = hk0[0, tq:2 * tq, :]
    hv0[0, 0:tq, :] = hv0[0, tq:2 * tq, :]

    ob = _merge_groups(o_refs, l_refs)
    ob = jnp.concatenate([ob[h] for h in range(HEADS_PER_GROUP)], axis=1)
    y_ref[...] = _tail(x, u, ya_ref[...].astype(F32), ob, wg_ref, bm_ref, wap_ref, wout_ref)


def _attn_call(x, ya, gn, wqkv, wg, gq, gk, bmerge, relb, wap, wout, *, tq):
    B, S, _ = x.shape
    nt = S // tq
    assert S % tq == 0 and tq % (DILATIONS[2] * 16) == 0 and tq >= N_BACK
    assert S // DILATIONS[2] == N_BACK and tq // DILATIONS[1] <= N_BACK
    Qc = tuple(tq // d for d in DILATIONS)
    QB = tuple(min(q, N_BACK) for q in Qc)
    n2 = S // DILATIONS[2]
    const = lambda shape: pl.BlockSpec(shape, lambda b, t: (0,) * len(shape), pipeline_mode=pl.Buffered(1))
    row = pl.BlockSpec((None, tq, D_MODEL), lambda b, t: (b, t, 0))

    def kv_spec(g):
        keep = min(WINDOWS[g], S)
        rb = min(tq, keep)
        first = nt - keep // rb
        return pl.BlockSpec((None, rb * KV_ROWS, HEAD_DIM), lambda b, t: (b, jnp.maximum(t - first, 0), 0))

    cls = lambda g, rows, dt: pltpu.VMEM((DILATIONS[g], rows, D_GRP), dt)
    scratch = [pltpu.VMEM((D_QKV // HEAD_DIM, tq, HEAD_DIM), F32)] * 3
    scratch += [cls(g, Qc[g], BF16) for g in range(N_GROUPS)]
    scratch += [cls(0, 2 * tq, BF16)] * 2 + [cls(1, KEY_WIN - QB[1] + S // DILATIONS[1], BF16)] * 2 + [cls(2, n2, BF16)] * 2
    scratch += [pltpu.VMEM((HEADS_PER_GROUP, tq, HEAD_DIM), F32)] * 6
    scratch += [pltpu.VMEM((HEADS_PER_GROUP, QB[0], KEY_WIN), F32), pltpu.VMEM((HEADS_PER_GROUP, QB[1], KEY_WIN), F32),
                pltpu.VMEM((HEADS_PER_GROUP, n2, n2), F32)]
    return pl.pallas_call(
        functools.partial(_attn_body, S=S, tq=tq),
        grid=(B, nt),
        in_specs=[row, row, const((1, D_MODEL)), const((D_MODEL, 3 * D_QKV)), const((D_MODEL, D_GRP + 2 * D_MODEL)),
                  const((1, HEAD_DIM)), const((1, HEAD_DIM)), const((1, 2 * D_MODEL)),
                  pl.BlockSpec(memory_space=pltpu.SMEM), const((D_GRP, D_MODEL)), const((D_MODEL, D_MODEL))],
        out_specs=[row, kv_spec(0), kv_spec(1), kv_spec(2)],
        out_shape=[jax.ShapeDtypeStruct((B, S, D_MODEL), F32)]
        + [jax.ShapeDtypeStruct((B, min(WINDOWS[g], S) * KV_ROWS, HEAD_DIM), F32) for g in range(N_GROUPS)],
        scratch_shapes=scratch,
        compiler_params=pltpu.CompilerParams(dimension_semantics=("arbitrary", "arbitrary"), vmem_limit_bytes=VMEM_LIMIT),
        name="attn_branch",
    )(x, ya, gn, wqkv, wg, gq, gk, bmerge, relb, wap, wout)


def _qkv_body(x_ref, gn_ref, wqkv_ref, gq_ref, gk_ref, zq_ref, zk_ref, zv_ref):
    u = _rms(x_ref[...], gn_ref[...]).astype(BF16)
    z_refs = (zq_ref, zk_ref, zv_ref)

    def store(part, head, value):
        z_refs[part][:, head * HEAD_DIM:(head + 1) * HEAD_DIM] = value

    _qkv(u, wqkv_ref, gq_ref, gk_ref, store)


def _qkv_call(x, gn, wqkv, gq, gk):
    M = x.shape[0]
    out = jax.ShapeDtypeStruct((M, D_QKV), F32)
    return pl.pallas_call(_qkv_body, out_shape=[out, out, out],
                          compiler_params=pltpu.CompilerParams(vmem_limit_bytes=VMEM_LIMIT),
                          name="decode_qkv")(x, gn, wqkv, gq, gk)


def _cache_attn_body(q_ref, k_ref, v_ref, cache_ref, relb_ref, new_ref, o_ref, lse_ref, kc_ref, vc_ref, bt_ref, *, g, T):
    L = cache_ref.shape[0] // KV_ROWS
    d = DILATIONS[g]
    W = L + HEAD_DIM

    @pl.when(pl.program_id(0) == 0)
    def _():
        kc_ref[L:W, :] = jnp.zeros((W - L, D_GRP), BF16)
        vc_ref[L:W, :] = jnp.zeros((W - L, D_GRP), BF16)
        tq = lax.broadcasted_iota(jnp.int32, (T, W), 0)
        p = lax.broadcasted_iota(jnp.int32, (T, W), 1)
        dist = L + tq - p
        valid = (dist >= 0) & ((dist & (d - 1)) == 0) & (dist <= N_BACK * d) & (p < L + T)
        for h in range(HEADS_PER_GROUP):
            bt_ref[h] = _bias_table(relb_ref, g * HEADS_PER_GROUP + h, dist, valid)

    k_new = k_ref[...]
    v_new = v_ref[...]
    new_ref[0:(L - T) * KV_ROWS, :] = cache_ref[T * KV_ROWS:L * KV_ROWS, :]
    for h in range(HEADS_PER_GROUP):
        hs_ = slice(h * HEAD_DIM, (h + 1) * HEAD_DIM)
        new_ref[pl.ds((L - T) * KV_ROWS + h, T, stride=KV_ROWS), :] = k_new[:, hs_]
        new_ref[pl.ds((L - T) * KV_ROWS + HEADS_PER_GROUP + h, T, stride=KV_ROWS), :] = v_new[:, hs_]
        kc_ref[0:L, hs_] = cache_ref[pl.ds(h, L, stride=KV_ROWS), :].astype(BF16)
        vc_ref[0:L, hs_] = cache_ref[pl.ds(HEADS_PER_GROUP + h, L, stride=KV_ROWS), :].astype(BF16)
    kc_ref[L:L + 2 * T, :] = jnp.concatenate([k_new, jnp.zeros_like(k_new)], axis=0).astype(BF16)
    vc_ref[L:L + 2 * T, :] = jnp.concatenate([v_new, jnp.zeros_like(v_new)], axis=0).astype(BF16)
    for h in range(HEADS_PER_GROUP):
        hs_ = slice(h * HEAD_DIM, (h + 1) * HEAD_DIM)
        s = _dot_nt(q_ref[:, hs_].astype(BF16), kc_ref[:, hs_]) * ATTN_SCALE + bt_ref[h]
        o, lse = _softmax_pv(s, vc_ref[:, hs_])
        o_ref[:, hs_] = o
        lse_ref[:, hs_] = jnp.broadcast_to(lse, (T, HEAD_DIM))


def _cache_attn_call(q, k, v, cache, relb, *, g, T):
    DB, rows, _ = cache.shape
    L = rows // KV_ROWS
    assert L == N_BACK * DILATIONS[g] and T % 8 == 0
    W = L + HEAD_DIM
    gcol = pl.BlockSpec((T, D_GRP), lambda b: (b, g))
    blk = pl.BlockSpec((None, rows, HEAD_DIM), lambda b: (b, 0, 0))
    orow = pl.BlockSpec((T, D_GRP), lambda b: (b, 0))
    return pl.pallas_call(
        functools.partial(_cache_attn_body, g=g, T=T),
        grid=(DB,),
        in_specs=[gcol, gcol, gcol, blk, pl.BlockSpec(memory_space=pltpu.SMEM)],
        out_specs=[blk, orow, orow],
        out_shape=[jax.ShapeDtypeStruct(cache.shape, F32), jax.ShapeDtypeStruct((DB * T, D_GRP), F32),
                   jax.ShapeDtypeStruct((DB * T, D_GRP), F32)],
        scratch_shapes=[pltpu.VMEM((W, D_GRP), BF16), pltpu.VMEM((W, D_GRP), BF16),
                        pltpu.VMEM((HEADS_PER_GROUP, T, W), F32)],
        compiler_params=pltpu.CompilerParams(dimension_semantics=("arbitrary",), vmem_limit_bytes=VMEM_LIMIT),
        name=f"cache_attn_w{L}",
    )(q, k, v, cache, relb)


def _tail_body(x_ref, ya_ref, gn_ref, wg_ref, bm_ref, wap_ref, wout_ref, o0, o1, o2, l0, l1, l2, y_ref):
    x = x_ref[...]
    u = _rms(x, gn_ref[...]).astype(BF16)
    ob = _merge_groups((o0, o1, o2), (l0, l1, l2))
    y_ref[...] = _tail(x, u, ya_ref[...], ob, wg_ref, bm_ref, wap_ref, wout_ref)


def _tail_call(x, ya, gn, wg, bmerge, wap, wout, os_, ls_):
    return pl.pallas_call(_tail_body, out_shape=jax.ShapeDtypeStruct(x.shape, F32),
                          compiler_params=pltpu.CompilerParams(vmem_limit_bytes=VMEM_LIMIT),
                          name="decode_tail")(x, ya, gn, wg, bmerge, wap, wout, *os_, *ls_)


def kernel(x_prompt, x_sample, cache_kv_w128, cache_kv_w512, cache_kv_w2048, state_conv, state_h, g_norm, w_in,
           b_merge, conv_w, conv_b, lru_w_a, lru_b_a, lru_w_x, lru_b_x, lru_lambda, g_q, g_k, rel_bias,
           w_lru_proj, w_attn_proj, w_out):
    assert w_in.shape[0] == 1, "single-layer step"
    B, S, _ = x_prompt.shape
    DB, T, _ = x_sample.shape
    o2 = 2 * D_LRU
    o5 = o2 + 3 * D_QKV
    w_lru = w_in[0, :, :o2].astype(BF16)
    wqkv = w_in[0, :, o2:o5].astype(BF16)
    wg = w_in[0, :, o5:].astype(BF16)
    wa, wx = lru_w_a[0].astype(BF16), lru_w_x[0].astype(BF16)
    wp, wap, wout = w_lru_proj[0].astype(BF16), w_attn_proj[0].astype(BF16), w_out[0].astype(BF16)
    lru_params = (g_norm, w_lru, conv_w[0], conv_b, wa, lru_b_a, wx, lru_b_x, lru_lambda, wp)
    hist = CONV_WIDTH - 1

    ya_p, conv_p, h_p = _lru_call(x_prompt, jnp.zeros((hist * B, D_LRU), F32), jnp.zeros((B, D_LRU), F32),
                                  *lru_params, tt=32)
    y_p, kv0_p, kv1_p, kv2_p = _attn_call(x_prompt, ya_p, g_norm, wqkv, wg, g_q, g_k, b_merge, rel_bias, wap, wout,
                                          tq=256)

    conv0_s = jnp.swapaxes(state_conv[0], 0, 1).reshape(hist * DB, D_LRU)
    ya_s, conv_s, h_s = _lru_call(x_sample, conv0_s, state_h[0], *lru_params, tt=T)
    xs2 = x_sample.reshape(DB * T, D_MODEL)
    zq, zk, zv = _qkv_call(xs2, g_norm, wqkv, g_q, g_k)
    news, os_, ls_ = [], [], []
    for g, cache in enumerate((cache_kv_w128, cache_kv_w512, cache_kv_w2048)):
        L = cache.shape[2]
        new, o, lse = _cache_attn_call(zq, zk, zv, cache.reshape(DB, L * KV_ROWS, HEAD_DIM), rel_bias, g=g, T=T)
        news.append(new.reshape(cache.shape))
        os_.append(o)
        ls_.append(lse)
    y_s = _tail_call(xs2, ya_s.reshape(DB * T, D_MODEL), g_norm, wg, b_merge, wap, wout, os_, ls_)

    kv_shape = lambda a: a.reshape(1, a.shape[0], a.shape[1] // KV_ROWS, 2, HEADS_PER_GROUP, HEAD_DIM)
    conv_out = lambda c, nb: jnp.swapaxes(c.reshape(hist, nb, D_LRU), 0, 1)[None]
    return (y_p, y_s.reshape(DB, T, D_MODEL), kv_shape(kv0_p), kv_shape(kv1_p), kv_shape(kv2_p),
            conv_out(conv_p, B), h_p[None], news[0], news[1], news[2], conv_out(conv_s, DB), h_s[None])
```

```python
import functools
import math

import jax
import jax.numpy as jnp
from jax import lax
from jax.experimental import pallas as pl
from jax.experimental.pallas import tpu as pltpu

F32 = jnp.float32
BF16 = jnp.bfloat16

D_MODEL = 1024
D_LRU = 1024
N_LRU_BLOCKS = 8
LRU_BLOCK = D_LRU // N_LRU_BLOCKS
CONV_WIDTH = 4
LRU_C = 8.0
HEAD_DIM = 128
HEADS_PER_GROUP = 4
WINDOWS = (128, 512, 2048)
DILATIONS = (1, 4, 16)
N_GROUPS = 3
N_BACK = 128
D_QKV = N_GROUPS * HEADS_PER_GROUP * HEAD_DIM
D_GRP = HEADS_PER_GROUP * HEAD_DIM
ATTN_SCALE = HEAD_DIM ** -0.5
N_BUCKETS = 32
MAX_DISTANCE = 2048
NORM_EPS = 1e-6
NEG = -1e30
KEY_WIN = 256
KV_ROWS = 2 * HEADS_PER_GROUP
VMEM_LIMIT = 60000 * 1024


def _rms(x, g):
    ms = jnp.mean(x * x, axis=-1, keepdims=True)
    return x * lax.rsqrt(ms + NORM_EPS) * g


def _sigmoid(x):
    return jax.nn.sigmoid(x)


def _dot(a, b):
    return jnp.dot(a, b, preferred_element_type=F32)


def _dot_nt(a, b):
    return lax.dot_general(a, b, (((1,), (1,)), ((), ())), preferred_element_type=F32)


def _bias_table(relb_ref, col, dist, valid):
    max_exact = N_BUCKETS // 2
    df = jnp.maximum(dist, 1).astype(F32)
    large = max_exact + (jnp.log(df / max_exact) / math.log(MAX_DISTANCE / max_exact)
                         * (N_BUCKETS - max_exact)).astype(jnp.int32)
    bucket = jnp.where(dist < max_exact, dist, jnp.minimum(large, N_BUCKETS - 1))
    out = jnp.zeros(dist.shape, F32)
    for b in range(N_BUCKETS):
        out = jnp.where(bucket == b, relb_ref[b, col], out)
    return jnp.where(valid, out, NEG)


def _softmax_pv(s, v):
    m = jnp.max(s, axis=-1, keepdims=True)
    p = jnp.exp(s - m)
    den = jnp.sum(p, axis=-1, keepdims=True)
    o = _dot(p.astype(BF16), v) / den
    return o, m + jnp.log(den)


def _merge_groups(o_refs, lse_refs):
    l0, l1, l2 = (r[...] for r in lse_refs)
    m = jnp.maximum(jnp.maximum(l0, l1), l2)
    e0, e1, e2 = jnp.exp(l0 - m), jnp.exp(l1 - m), jnp.exp(l2 - m)
    num = e0 * o_refs[0][...] + e1 * o_refs[1][...] + e2 * o_refs[2][...]
    return num / (e0 + e1 + e2)


def _tail(x, u, ya, ob, wg_ref, bm_ref, wap_ref, wout_ref):
    zg = _dot(u, wg_ref[...])
    gb = zg[:, :D_GRP]
    yb = _dot((ob * (gb * _sigmoid(gb))).astype(BF16), wap_ref[...])
    gates = _sigmoid(zg[:, D_GRP:] + bm_ref[...])
    merged = gates[:, :D_MODEL] * ya + gates[:, D_MODEL:] * yb
    return x + _dot(merged.astype(BF16), wout_ref[...])


def _qkv(u, wqkv_ref, gq_ref, gk_ref, store):
    for part, g_ref in enumerate((gq_ref, gk_ref, None)):
        z = _dot(u, wqkv_ref[:, part * D_QKV:(part + 1) * D_QKV])
        zs = [z[:, h * HEAD_DIM:(h + 1) * HEAD_DIM] for h in range(D_QKV // HEAD_DIM)]
        if g_ref is not None:
            ms = [jnp.mean(zh * zh, axis=-1, keepdims=True) for zh in zs]
            zs = [zh * lax.rsqrt(m + NORM_EPS) * g_ref[...] for zh, m in zip(zs, ms)]
        for h, zh in enumerate(zs):
            store(part, h, zh)


def _lru_body(x_ref, gn_ref, w_ref, cw_ref, cb_ref, wa_ref, ba_ref, wx_ref, bx_ref, lam_ref, wp_ref,
              conv0_ref, h0_ref, ya_ref, convo_ref, ho_ref, xs_ref, ga_ref, gs_ref, h_ref, *, B, tt):
    R = B * tt
    hist = (CONV_WIDTH - 1) * B

    @pl.when(pl.program_id(0) == 0)
    def _():
        xs_ref[0:hist, :] = conv0_ref[...]
        h_ref[...] = h0_ref[...]

    xt = jnp.concatenate([x_ref[:, t, :] for t in range(tt)], axis=0)
    u = _rms(xt, gn_ref[...]).astype(BF16)
    xs_ref[hist:hist + R, :] = _dot(u, w_ref[:, :D_LRU])
    ga_ref[...] = _dot(u, w_ref[:, D_LRU:])
    sp = jax.nn.softplus(-lam_ref[...])
    for n in range(N_LRU_BLOCKS):
        cs = slice(n * LRU_BLOCK, (n + 1) * LRU_BLOCK)
        y = cb_ref[:, cs] + sum(xs_ref[j * B:j * B + R, cs] * cw_ref[j:j + 1, cs] for j in range(CONV_WIDTH))
        yb = y.astype(BF16)
        r = _sigmoid(_dot(yb, wa_ref[n]) + ba_ref[:, cs])
        i = _sigmoid(_dot(yb, wx_ref[n]) + bx_ref[:, cs])
        log_a = -LRU_C * r * sp[:, cs]
        a = jnp.exp(log_a)
        th = jnp.tanh(log_a)
        uu = jnp.sqrt(-2.0 * th / (1.0 - th)) * (i * y)
        h = h_ref[:, cs]
        hs = []
        for t in range(tt):
            h = a[t * B:(t + 1) * B] * h + uu[t * B:(t + 1) * B]
            hs.append(h)
        h_ref[:, cs] = h
        g = ga_ref[:, cs]
        gs_ref[:, cs] = (jnp.concatenate(hs, axis=0) * (g * _sigmoid(g))).astype(BF16)
    ya = _dot(gs_ref[...], wp_ref[...])
    for t in range(tt):
        ya_ref[:, t, :] = ya[t * B:(t + 1) * B].astype(ya_ref.dtype)
    tail = xs_ref[R:R + hist, :]
    convo_ref[...] = tail
    xs_ref[0:hist, :] = tail
    ho_ref[...] = h_ref[...]


def _lru_call(x, conv0, h0, gn, w_lru, cw, cb, wa, ba, wx, bx, lam, wp, *, tt):
    B, S, _ = x.shape
    assert S % tt == 0 and tt >= CONV_WIDTH - 1 and tt % 8 == 0 and B % 8 == 0
    R = B * tt
    hist = (CONV_WIDTH - 1) * B
    const = lambda shape: pl.BlockSpec(shape, lambda i: (0,) * len(shape), pipeline_mode=pl.Buffered(1))
    return pl.pallas_call(
        functools.partial(_lru_body, B=B, tt=tt),
        grid=(S // tt,),
        in_specs=[
            pl.BlockSpec((B, tt, D_MODEL), lambda i: (0, i, 0)),
            const((1, D_MODEL)), const((D_MODEL, 2 * D_LRU)), const((CONV_WIDTH, D_LRU)), const((1, D_LRU)),
            const((N_LRU_BLOCKS, LRU_BLOCK, LRU_BLOCK)), const((1, D_LRU)),
            const((N_LRU_BLOCKS, LRU_BLOCK, LRU_BLOCK)), const((1, D_LRU)), const((1, D_LRU)),
            const((D_LRU, D_MODEL)), const((hist, D_LRU)), const((B, D_LRU)),
        ],
        out_specs=[
            pl.BlockSpec((B, tt, D_MODEL), lambda i: (0, i, 0)),
            pl.BlockSpec((hist, D_LRU), lambda i: (0, 0)),
            pl.BlockSpec((B, D_LRU), lambda i: (0, 0)),
        ],
        out_shape=[
            jax.ShapeDtypeStruct((B, S, D_MODEL), F32),
            jax.ShapeDtypeStruct((hist, D_LRU), F32),
            jax.ShapeDtypeStruct((B, D_LRU), F32),
        ],
        scratch_shapes=[
            pltpu.VMEM((hist + R, D_LRU), F32),
            pltpu.VMEM((R, D_LRU), F32),
            pltpu.VMEM((R, D_LRU), BF16),
            pltpu.VMEM((B, D_LRU), F32),
        ],
        compiler_params=pltpu.CompilerParams(dimension_semantics=("arbitrary",), vmem_limit_bytes=VMEM_LIMIT),
        name="lru_branch",
    )(x, gn, w_lru, cw, cb, wa, ba, wx, bx, lam, wp, conv0, h0)


def _attn_body(x_ref, ya_ref, gn_ref, wqkv_ref, wg_ref, gq_ref, gk_ref, bm_ref, relb_ref, wap_ref, wout_ref,
               y_ref, kv0_ref, kv1_ref, kv2_ref,
               zq_ref, zk_ref, zv_ref, qc0, qc1, qc2, hk0, hv0, hk1, hv1, hk2, hv2,
               o0, o1, o2, l0, l1, l2, bt0, bt1, bt2, s_ref, p_ref, *, S, tq):
    t = pl.program_id(1)
    nt = S // tq
    qcs, hks, hvs = (qc0, qc1, qc2), (hk0, hk1, hk2), (hv0, hv1, hv2)
    o_refs, l_refs, bts, kvs = (o0, o1, o2), (l0, l1, l2), (bt0, bt1, bt2), (kv0_ref, kv1_ref, kv2_ref)
    Qc = tuple(tq // d for d in DILATIONS)
    QB = tuple(min(q, N_BACK) for q in Qc)
    pad1 = KEY_WIN - QB[1]
    n2 = S // DILATIONS[2]

    @pl.when((pl.program_id(0) == 0) & (t == 0))
    def _():
        for ref in (hk0, hv0, hk1, hv1, hk2, hv2):
            ref[...] = jnp.zeros(ref.shape, ref.dtype)
        for g in (0, 1):
            a = lax.broadcasted_iota(jnp.int32, (QB[g], KEY_WIN), 0)
            c = lax.broadcasted_iota(jnp.int32, (QB[g], KEY_WIN), 1)
            j = a + (KEY_WIN - QB[g]) - c
            for h in range(HEADS_PER_GROUP):
                bts[g][h] = _bias_table(relb_ref, g * HEADS_PER_GROUP + h, j * DILATIONS[g], (j >= 0) & (j <= N_BACK))
        i = lax.broadcasted_iota(jnp.int32, (n2, n2), 0)
        c = lax.broadcasted_iota(jnp.int32, (n2, n2), 1)
        for h in range(HEADS_PER_GROUP):
            bt2[h] = _bias_table(relb_ref, 2 * HEADS_PER_GROUP + h, (i - c) * DILATIONS[2],
                                 (i - c >= 0) & (i - c <= N_BACK))

    x = x_ref[...]
    u = _rms(x, gn_ref[...]).astype(BF16)
    z_refs = (zq_ref, zk_ref, zv_ref)

    def store(part, head, value):
        z_refs[part][head] = value

    _qkv(u, wqkv_ref, gq_ref, gk_ref, store)

    for g in range(N_GROUPS):
        keep = min(WINDOWS[g], S)
        rb = min(tq, keep)

        @pl.when(t >= nt - keep // rb)
        def _(g=g, rb=rb):
            for part, z_ref in enumerate((zk_ref, zv_ref)):
                for h in range(HEADS_PER_GROUP):
                    dst = pl.ds(part * HEADS_PER_GROUP + h, rb, stride=KV_ROWS)
                    kvs[g][dst, :] = z_ref[g * HEADS_PER_GROUP + h, tq - rb:tq, :]

    for g in range(N_GROUPS):
        d = DILATIONS[g]
        for r in range(d):
            rows = pl.ds(r, Qc[g], stride=d) if d > 1 else slice(None)
            if g == 0:
                dst = pl.ds(tq, tq)
            elif g == 1:
                dst = pl.ds(pl.multiple_of(pad1 + t * Qc[1], Qc[1]), Qc[1])
            else:
                dst = pl.ds(pl.multiple_of(t * Qc[2], Qc[2]), Qc[2])
            for h in range(HEADS_PER_GROUP):
                hs_ = slice(h * HEAD_DIM, (h + 1) * HEAD_DIM)
                gh = g * HEADS_PER_GROUP + h
                qcs[g][r, :, hs_] = zq_ref[gh, rows, :].astype(BF16)
                hks[g][r, dst, hs_] = zk_ref[gh, rows, :].astype(BF16)
                hvs[g][r, dst, hs_] = zv_ref[gh, rows, :].astype(BF16)

    col = lax.broadcasted_iota(jnp.int32, (1, KEY_WIN), 1)
    rowi = lax.broadcasted_iota(jnp.int32, (tq, 1), 0)
    for g in range(N_GROUPS):
        d = DILATIONS[g]
        nq = Qc[g] // QB[g]
        kw = n2 if g == 2 else KEY_WIN
        units = [(r, qb) for r in range(d) for qb in range(nq)]

        def key_win(qb, g=g):
            if g == 0:
                return pl.ds(tq + qb * QB[0] + QB[0] - KEY_WIN, KEY_WIN)
            if g == 1:
                return pl.ds(pl.multiple_of(t * Qc[1], QB[1]), KEY_WIN)
            return slice(None)

        for r, qb in units:
            cm = slice(r * Qc[g] + qb * QB[g], r * Qc[g] + (qb + 1) * QB[g])
            for h in range(HEADS_PER_GROUP):
                hs_ = slice(h * HEAD_DIM, (h + 1) * HEAD_DIM)
                s_ref[h, cm, 0:kw] = _dot_nt(qcs[g][r, qb * QB[g]:(qb + 1) * QB[g], hs_], hks[g][r, key_win(qb), hs_])

        if g < 2:
            i0 = t * Qc[g] + (rowi & (Qc[g] - QB[g]))
            started = col >= KEY_WIN - QB[g] - i0
        for h in range(HEADS_PER_GROUP):
            bias = bt2[h, pl.ds(pl.multiple_of(t * QB[2], QB[2]), QB[2]), :] if g == 2 else bts[g][h]
            s = s_ref[h, :, 0:kw] * ATTN_SCALE
            s = (s.reshape(len(units), QB[g], kw) + bias[None]).reshape(tq, kw)
            if g < 2:
                s = jnp.where(started, s, NEG)
            m = jnp.max(s, axis=-1, keepdims=True)
            p = jnp.exp(s - m)
            den = jnp.sum(p, axis=-1, keepdims=True)
            p_ref[h, :, 0:kw] = (p * (1.0 / den)).astype(BF16)
            lse = jnp.broadcast_to(m + jnp.log(den), (tq, HEAD_DIM))
            for r in range(d):
                rows = pl.ds(r, Qc[g], stride=d) if d > 1 else slice(None)
                l_refs[g][h, rows, :] = lse[r * Qc[g]:(r + 1) * Qc[g]]

        for r, qb in units:
            cm = slice(r * Qc[g] + qb * QB[g], r * Qc[g] + (qb + 1) * QB[g])
            rows = pl.ds(r + d * qb * QB[g], QB[g], stride=d) if d > 1 else pl.ds(qb * QB[g], QB[g])
            for h in range(HEADS_PER_GROUP):
                hs_ = slice(h * HEAD_DIM, (h + 1) * HEAD_DIM)
                o_refs[g][h, rows, :] = _dot(p_ref[h, cm, 0:kw], hvs[g][r, key_win(qb), hs_])

    hk0[0, 0:tq, :] = hk0[0, tq:2 * tq, :]
    hv0[0, 0:tq, :] = hv0[0, tq:2 * tq, :]

    ob = _merge_groups(o_refs, l_refs)
    ob = jnp.concatenate([ob[h] for h in range(HEADS_PER_GROUP)], axis=1)
    y_ref[...] = _tail(x, u, ya_ref[...].astype(F32), ob, wg_ref, bm_ref, wap_ref, wout_ref)


def _attn_call(x, ya, gn, wqkv, wg, gq, gk, bmerge, relb, wap, wout, *, tq):
    B, S, _ = x.shape
    nt = S // tq
    assert S % tq == 0 and tq % (DILATIONS[2] * 16) == 0 and tq >= N_BACK
    assert S // DILATIONS[2] == N_BACK and tq // DILATIONS[1] <= N_BACK
    Qc = tuple(tq // d for d in DILATIONS)
    QB = tuple(min(q, N_BACK) for q in Qc)
    n2 = S // DILATIONS[2]
    const = lambda shape: pl.BlockSpec(shape, lambda b, t: (0,) * len(shape), pipeline_mode=pl.Buffered(1))
    row = pl.BlockSpec((None, tq, D_MODEL), lambda b, t: (b, t, 0))

    def kv_spec(g):
        keep = min(WINDOWS[g], S)
        rb = min(tq, keep)
        first = nt - keep // rb
        return pl.BlockSpec((None, rb * KV_ROWS, HEAD_DIM), lambda b, t: (b, jnp.maximum(t - first, 0), 0))

    cls = lambda g, rows, dt: pltpu.VMEM((DILATIONS[g], rows, D_GRP), dt)
    scratch = [pltpu.VMEM((D_QKV // HEAD_DIM, tq, HEAD_DIM), F32)] * 3
    scratch += [cls(g, Qc[g], BF16) for g in range(N_GROUPS)]
    scratch += [cls(0, 2 * tq, BF16)] * 2 + [cls(1, KEY_WIN - QB[1] + S // DILATIONS[1], BF16)] * 2 + [cls(2, n2, BF16)] * 2
    scratch += [pltpu.VMEM((HEADS_PER_GROUP, tq, HEAD_DIM), F32)] * 6
    scratch += [pltpu.VMEM((HEADS_PER_GROUP, QB[0], KEY_WIN), F32), pltpu.VMEM((HEADS_PER_GROUP, QB[1], KEY_WIN), F32),
                pltpu.VMEM((HEADS_PER_GROUP, n2, n2), F32)]
    scratch += [pltpu.VMEM((HEADS_PER_GROUP, tq, KEY_WIN), F32), pltpu.VMEM((HEADS_PER_GROUP, tq, KEY_WIN), BF16)]
    return pl.pallas_call(
        functools.partial(_attn_body, S=S, tq=tq),
        grid=(B, nt),
        in_specs=[row, row, const((1, D_MODEL)), const((D_MODEL, 3 * D_QKV)), const((D_MODEL, D_GRP + 2 * D_MODEL)),
                  const((1, HEAD_DIM)), const((1, HEAD_DIM)), const((1, 2 * D_MODEL)),
                  pl.BlockSpec(memory_space=pltpu.SMEM), const((D_GRP, D_MODEL)), const((D_MODEL, D_MODEL))],
        out_specs=[row, kv_spec(0), kv_spec(1), kv_spec(2)],
        out_shape=[jax.ShapeDtypeStruct((B, S, D_MODEL), F32)]
        + [jax.ShapeDtypeStruct((B, min(WINDOWS[g], S) * KV_ROWS, HEAD_DIM), F32) for g in range(N_GROUPS)],
        scratch_shapes=scratch,
        compiler_params=pltpu.CompilerParams(dimension_semantics=("arbitrary", "arbitrary"), vmem_limit_bytes=VMEM_LIMIT),
        name="attn_branch",
    )(x, ya, gn, wqkv, wg, gq, gk, bmerge, relb, wap, wout)


def _qkv_body(x_ref, gn_ref, wqkv_ref, gq_ref, gk_ref, zq_ref, zk_ref, zv_ref):
    u = _rms(x_ref[...], gn_ref[...]).astype(BF16)
    z_refs = (zq_ref, zk_ref, zv_ref)

    def store(part, head, value):
        z_refs[part][:, head * HEAD_DIM:(head + 1) * HEAD_DIM] = value

    _qkv(u, wqkv_ref, gq_ref, gk_ref, store)


def _qkv_call(x, gn, wqkv, gq, gk):
    M = x.shape[0]
    out = jax.ShapeDtypeStruct((M, D_QKV), F32)
    return pl.pallas_call(_qkv_body, out_shape=[out, out, out],
                          compiler_params=pltpu.CompilerParams(vmem_limit_bytes=VMEM_LIMIT),
                          name="decode_qkv")(x, gn, wqkv, gq, gk)


def _cache_attn_body(q_ref, k_ref, v_ref, cache_ref, relb_ref, new_ref, o_ref, lse_ref, kc_ref, vc_ref, bt_ref, *, g, T):
    L = cache_ref.shape[0] // KV_ROWS
    d = DILATIONS[g]
    W = L + HEAD_DIM

    @pl.when(pl.program_id(0) == 0)
    def _():
        kc_ref[L:W, :] = jnp.zeros((W - L, D_GRP), BF16)
        vc_ref[L:W, :] = jnp.zeros((W - L, D_GRP), BF16)
        tq = lax.broadcasted_iota(jnp.int32, (T, W), 0)
        p = lax.broadcasted_iota(jnp.int32, (T, W), 1)
        dist = L + tq - p
        valid = (dist >= 0) & ((dist & (d - 1)) == 0) & (dist <= N_BACK * d) & (p < L + T)
        for h in range(HEADS_PER_GROUP):
            bt_ref[h] = _bias_table(relb_ref, g * HEADS_PER_GROUP + h, dist, valid)

    k_new = k_ref[...]
    v_new = v_ref[...]
    new_ref[0:(L - T) * KV_ROWS, :] = cache_ref[T * KV_ROWS:L * KV_ROWS, :]
    for h in range(HEADS_PER_GROUP):
        hs_ = slice(h * HEAD_DIM, (h + 1) * HEAD_DIM)
        new_ref[pl.ds((L - T) * KV_ROWS + h, T, stride=KV_ROWS), :] = k_new[:, hs_]
        new_ref[pl.ds((L - T) * KV_ROWS + HEADS_PER_GROUP + h, T, stride=KV_ROWS), :] = v_new[:, hs_]
        kc_ref[0:L, hs_] = cache_ref[pl.ds(h, L, stride=KV_ROWS), :].astype(BF16)
        vc_ref[0:L, hs_] = cache_ref[pl.ds(HEADS_PER_GROUP + h, L, stride=KV_ROWS), :].astype(BF16)
    kc_ref[L:L + 2 * T, :] = jnp.concatenate([k_new, jnp.zeros_like(k_new)], axis=0).astype(BF16)
    vc_ref[L:L + 2 * T, :] = jnp.concatenate([v_new, jnp.zeros_like(v_new)], axis=0).astype(BF16)
    for h in range(HEADS_PER_GROUP):
        hs_ = slice(h * HEAD_DIM, (h + 1) * HEAD_DIM)
        s = _dot_nt(q_ref[:, hs_].astype(BF16), kc_ref[:, hs_]) * ATTN_SCALE + bt_ref[h]
        o, lse = _softmax_pv(s, vc_ref[:, hs_])
        o_ref[:, hs_] = o
        lse_ref[:, hs_] = jnp.broadcast_to(lse, (T, HEAD_DIM))


def _cache_attn_call(q, k, v, cache, relb, *, g, T):
    DB, rows, _ = cache.shape
    L = rows // KV_ROWS
    assert L == N_BACK * DILATIONS[g] and T % 8 == 0
    W = L + HEAD_DIM
    gcol = pl.BlockSpec((T, D_GRP), lambda b: (b, g))
    blk = pl.BlockSpec((None, rows, HEAD_DIM), lambda b: (b, 0, 0))
    orow = pl.BlockSpec((T, D_GRP), lambda b: (b, 0))
    return pl.pallas_call(
        functools.partial(_cache_attn_body, g=g, T=T),
        grid=(DB,),
        in_specs=[gcol, gcol, gcol, blk, pl.BlockSpec(memory_space=pltpu.SMEM)],
        out_specs=[blk, orow, orow],
        out_shape=[jax.ShapeDtypeStruct(cache.shape, F32), jax.ShapeDtypeStruct((DB * T, D_GRP), F32),
                   jax.ShapeDtypeStruct((DB * T, D_GRP), F32)],
        scratch_shapes=[pltpu.VMEM((W, D_GRP), BF16), pltpu.VMEM((W, D_GRP), BF16),
                        pltpu.VMEM((HEADS_PER_GROUP, T, W), F32)],
        compiler_params=pltpu.CompilerParams(dimension_semantics=("arbitrary",), vmem_limit_bytes=VMEM_LIMIT),
        name=f"cache_attn_w{L}",
    )(q, k, v, cache, relb)


def _tail_body(x_ref, ya_ref, gn_ref, wg_ref, bm_ref, wap_ref, wout_ref, o0, o1, o2, l0, l1, l2, y_ref):
    x = x_ref[...]
    u = _rms(x, gn_ref[...]).astype(BF16)
    ob = _merge_groups((o0, o1, o2), (l0, l1, l2))
    y_ref[...] = _tail(x, u, ya_ref[...], ob, wg_ref, bm_ref, wap_ref, wout_ref)


def _tail_call(x, ya, gn, wg, bmerge, wap, wout, os_, ls_):
    return pl.pallas_call(_tail_body, out_shape=jax.ShapeDtypeStruct(x.shape, F32),
                          compiler_params=pltpu.CompilerParams(vmem_limit_bytes=VMEM_LIMIT),
                          name="decode_tail")(x, ya, gn, wg, bmerge, wap, wout, *os_, *ls_)


def kernel(x_prompt, x_sample, cache_kv_w128, cache_kv_w512, cache_kv_w2048, state_conv, state_h, g_norm, w_in,
           b_merge, conv_w, conv_b, lru_w_a, lru_b_a, lru_w_x, lru_b_x, lru_lambda, g_q, g_k, rel_bias,
           w_lru_proj, w_attn_proj, w_out):
    assert w_in.shape[0] == 1, "single-layer step"
    B, S, _ = x_prompt.shape
    DB, T, _ = x_sample.shape
    o2 = 2 * D_LRU
    o5 = o2 + 3 * D_QKV
    w_lru = w_in[0, :, :o2].astype(BF16)
    wqkv = w_in[0, :, o2:o5].astype(BF16)
    wg = w_in[0, :, o5:].astype(BF16)
    wa, wx = lru_w_a[0].astype(BF16), lru_w_x[0].astype(BF16)
    wp, wap, wout = w_lru_proj[0].astype(BF16), w_attn_proj[0].astype(BF16), w_out[0].astype(BF16)
    lru_params = (g_norm, w_lru, conv_w[0], conv_b, wa, lru_b_a, wx, lru_b_x, lru_lambda, wp)
    hist = CONV_WIDTH - 1

    ya_p, conv_p, h_p = _lru_call(x_prompt, jnp.zeros((hist * B, D_LRU), F32), jnp.zeros((B, D_LRU), F32),
                                  *lru_params, tt=32)
    y_p, kv0_p, kv1_p, kv2_p = _attn_call(x_prompt, ya_p, g_norm, wqkv, wg, g_q, g_k, b_merge, rel_bias, wap, wout,
                                          tq=256)

    conv0_s = jnp.swapaxes(state_conv[0], 0, 1).reshape(hist * DB, D_LRU)
    ya_s, conv_s, h_s = _lru_call(x_sample, conv0_s, state_h[0], *lru_params, tt=T)
    xs2 = x_sample.reshape(DB * T, D_MODEL)
    zq, zk, zv = _qkv_call(xs2, g_norm, wqkv, g_q, g_k)
    news, os_, ls_ = [], [], []
    for g, cache in enumerate((cache_kv_w128, cache_kv_w512, cache_kv_w2048)):
        L = cache.shape[2]
        new, o, lse = _cache_attn_call(zq, zk, zv, cache.reshape(DB, L * KV_ROWS, HEAD_DIM), rel_bias, g=g, T=T)
        news.append(new.reshape(cache.shape))
        os_.append(o)
        ls_.append(lse)
    y_s = _tail_call(xs2, ya_s.reshape(DB * T, D_MODEL), g_norm, wg, b_merge, wap, wout, os_, ls_)

    kv_shape = lambda a: a.reshape(1, a.shape[0], a.shape[1] // KV_ROWS, 2, HEADS_PER_GROUP, HEAD_DIM)
    conv_out = lambda c, nb: jnp.swapaxes(c.reshape(hist, nb, D_LRU), 0, 1)[None]
    return (y_p, y_s.reshape(DB, T, D_MODEL), kv_shape(kv0_p), kv_shape(kv1_p), kv_shape(kv2_p),
            conv_out(conv_p, B), h_p[None], news[0], news[1], news[2], conv_out(conv_s, DB), h_s[None])
```

```python
import functools
import math

import jax
import jax.numpy as jnp
from jax import lax
from jax.experimental import pallas as pl
from jax.experimental.pallas import tpu as pltpu

F32 = jnp.float32
BF16 = jnp.bfloat16

D_MODEL = 1024
D_LRU = 1024
N_LRU_BLOCKS = 8
LRU_BLOCK = D_LRU // N_LRU_BLOCKS
CONV_WIDTH = 4
LRU_C = 8.0
HEAD_DIM = 128
HEADS_PER_GROUP = 4
WINDOWS = (128, 512, 2048)
DILATIONS = (1, 4, 16)
N_GROUPS = 3
N_BACK = 128
D_QKV = N_GROUPS * HEADS_PER_GROUP * HEAD_DIM
D_GRP = HEADS_PER_GROUP * HEAD_DIM
ATTN_SCALE = HEAD_DIM ** -0.5
N_BUCKETS = 32
MAX_DISTANCE = 2048
NORM_EPS = 1e-6
NEG = -1e30
KEY_WIN = 256
KV_ROWS = 2 * HEADS_PER_GROUP
VMEM_LIMIT = 60000 * 1024


def _rms(x, g):
    ms = jnp.mean(x * x, axis=-1, keepdims=True)
    return x * lax.rsqrt(ms + NORM_EPS) * g


def _sigmoid(x):
    return 0.5 * jnp.tanh(0.5 * x) + 0.5


def _dot(a, b):
    return jnp.dot(a, b, preferred_element_type=F32)


def _dot_nt(a, b):
    return lax.dot_general(a, b, (((1,), (1,)), ((), ())), preferred_element_type=F32)


def _bias_table(relb_ref, col, dist, valid):
    max_exact = N_BUCKETS // 2
    n_log = N_BUCKETS - max_exact
    df = jnp.maximum(dist, 1).astype(F32)
    val = jnp.log(df / max_exact) / math.log(MAX_DISTANCE / max_exact) * n_log
    out = jnp.zeros(dist.shape, F32)
    for b in range(max_exact):
        out = jnp.where(dist == b, relb_ref[b, col], out)
    for k in range(n_log):
        out = jnp.where((dist >= max_exact) & (val >= k), relb_ref[max_exact + k, col], out)
    return jnp.where(valid, out, NEG)


def _softmax_pv(s, v):
    m = jnp.max(s, axis=-1, keepdims=True)
    p = jnp.exp(s - m)
    den = jnp.sum(p, axis=-1, keepdims=True)
    o = _dot(p.astype(BF16), v) / den
    return o, m + jnp.log(den)


def _merge_groups(o_refs, lse_refs):
    l0, l1, l2 = (r[...] for r in lse_refs)
    m = jnp.maximum(jnp.maximum(l0, l1), l2)
    e0, e1, e2 = jnp.exp(l0 - m), jnp.exp(l1 - m), jnp.exp(l2 - m)
    num = e0 * o_refs[0][...] + e1 * o_refs[1][...] + e2 * o_refs[2][...]
    return num / (e0 + e1 + e2)


def _tail(x, u, ya, ob, wg_ref, bm_ref, wap_ref, wout_ref):
    zg = _dot(u, wg_ref[...])
    gb = zg[:, :D_GRP]
    yb = _dot((ob * (gb * _sigmoid(gb))).astype(BF16), wap_ref[...])
    gates = _sigmoid(zg[:, D_GRP:] + bm_ref[...])
    merged = gates[:, :D_MODEL] * ya + gates[:, D_MODEL:] * yb
    return x + _dot(merged.astype(BF16), wout_ref[...])


def _qkv(u, wqkv_ref, gq_ref, gk_ref, store):
    for part, g_ref in enumerate((gq_ref, gk_ref, None)):
        z = _dot(u, wqkv_ref[:, part * D_QKV:(part + 1) * D_QKV])
        zs = [z[:, h * HEAD_DIM:(h + 1) * HEAD_DIM] for h in range(D_QKV // HEAD_DIM)]
        if g_ref is not None:
            ms = [jnp.mean(zh * zh, axis=-1, keepdims=True) for zh in zs]
            zs = [zh * lax.rsqrt(m + NORM_EPS) * g_ref[...] for zh, m in zip(zs, ms)]
        for h, zh in enumerate(zs):
            store(part, h, zh)


N_LRU_IN = 13


def _shift_cache_block(blk_ref, next_ref, fresh_ref, out_ref, batch_rows):
    rpb = blk_ref.shape[0]
    hop = fresh_ref.shape[0]
    out_ref[0:rpb - hop, :] = blk_ref[hop:rpb, :]
    ends_batch = ((pl.program_id(0) + 1) * rpb) % batch_rows == 0

    @pl.when(ends_batch)
    def _():
        out_ref[rpb - hop:rpb, :] = fresh_ref[...]

    @pl.when(jnp.logical_not(ends_batch))
    def _():
        out_ref[rpb - hop:rpb, :] = next_ref[...]


def _lru_body(*refs, B, tt, batch_rows):
    (x_ref, gn_ref, w_ref, cw_ref, cb_ref, wa_ref, ba_ref, wx_ref, bx_ref, lam_ref, wp_ref,
     conv0_ref, h0_ref) = refs[:N_LRU_IN]
    n = len(batch_rows)
    shift_in = refs[N_LRU_IN:N_LRU_IN + 3 * n]
    ya_ref, convo_ref, ho_ref = refs[N_LRU_IN + 3 * n:N_LRU_IN + 3 * n + 3]
    shift_out = refs[N_LRU_IN + 3 * n + 3:N_LRU_IN + 4 * n + 3]
    xs_ref, ga_ref, gs_ref, h_ref = refs[N_LRU_IN + 4 * n + 3:]
    R = B * tt
    hist = (CONV_WIDTH - 1) * B

    @pl.when(pl.program_id(0) == 0)
    def _():
        xs_ref[0:hist, :] = conv0_ref[...]
        h_ref[...] = h0_ref[...]

    for k in range(n):
        _shift_cache_block(*shift_in[3 * k:3 * k + 3], shift_out[k], batch_rows[k])

    xt = jnp.concatenate([x_ref[:, t, :] for t in range(tt)], axis=0)
    u = _rms(xt, gn_ref[...]).astype(BF16)
    xs_ref[hist:hist + R, :] = _dot(u, w_ref[:, :D_LRU])
    ga_ref[...] = _dot(u, w_ref[:, D_LRU:])
    sp = jax.nn.softplus(-lam_ref[...])
    for n in range(N_LRU_BLOCKS):
        cs = slice(n * LRU_BLOCK, (n + 1) * LRU_BLOCK)
        y = cb_ref[:, cs] + sum(xs_ref[j * B:j * B + R, cs] * cw_ref[j:j + 1, cs] for j in range(CONV_WIDTH))
        yb = y.astype(BF16)
        r = _sigmoid(_dot(yb, wa_ref[n]) + ba_ref[:, cs])
        i = _sigmoid(_dot(yb, wx_ref[n]) + bx_ref[:, cs])
        log_a = -LRU_C * r * sp[:, cs]
        a = jnp.exp(log_a)
        th = jnp.tanh(log_a)
        uu = jnp.sqrt(-2.0 * th / (1.0 - th)) * (i * y)
        h = h_ref[:, cs]
        hs = []
        for t in range(tt):
            h = a[t * B:(t + 1) * B] * h + uu[t * B:(t + 1) * B]
            hs.append(h)
        h_ref[:, cs] = h
        g = ga_ref[:, cs]
        gs_ref[:, cs] = (jnp.concatenate(hs, axis=0) * (g * _sigmoid(g))).astype(BF16)
    ya = _dot(gs_ref[...], wp_ref[...])
    for t in range(tt):
        ya_ref[:, t, :] = ya[t * B:(t + 1) * B].astype(ya_ref.dtype)
    tail = xs_ref[R:R + hist, :]
    convo_ref[...] = tail
    xs_ref[0:hist, :] = tail
    ho_ref[...] = h_ref[...]


def _lru_call(x, conv0, h0, gn, w_lru, cw, cb, wa, ba, wx, bx, lam, wp, *, tt, caches=(), fresh=(), n_cache_batch=1):
    B, S, _ = x.shape
    assert S % tt == 0 and tt >= CONV_WIDTH - 1 and tt % 8 == 0 and B % 8 == 0
    R = B * tt
    hist = (CONV_WIDTH - 1) * B
    steps = S // tt
    const = lambda shape: pl.BlockSpec(shape, lambda i: (0,) * len(shape), pipeline_mode=pl.Buffered(1))
    shift_in, shift_out, shift_shapes, batch_rows = [], [], [], []
    for c, f in zip(caches, fresh):
        total = c.shape[0]
        rpb = total // steps
        rows_b = total // n_cache_batch
        hop = f.shape[0] // n_cache_batch
        assert total % steps == 0 and rows_b % rpb == 0 and rpb % hop == 0 and rpb > hop
        shift_in += [
            pl.BlockSpec((rpb, HEAD_DIM), lambda i: (i, 0)),
            pl.BlockSpec((hop, HEAD_DIM), lambda i, rpb=rpb, hop=hop, total=total:
                         (jnp.minimum((i + 1) * (rpb // hop), total // hop - 1), 0)),
            pl.BlockSpec((hop, HEAD_DIM), lambda i, rpb=rpb, rows_b=rows_b: ((i * rpb) // rows_b, 0)),
        ]
        shift_out.append(pl.BlockSpec((rpb, HEAD_DIM), lambda i: (i, 0)))
        shift_shapes.append(jax.ShapeDtypeStruct(c.shape, c.dtype))
        batch_rows.append(rows_b)
    shift_args = [a for c, f in zip(caches, fresh) for a in (c, c, f)]
    return pl.pallas_call(
        functools.partial(_lru_body, B=B, tt=tt, batch_rows=tuple(batch_rows)),
        grid=(steps,),
        in_specs=[
            pl.BlockSpec((B, tt, D_MODEL), lambda i: (0, i, 0)),
            const((1, D_MODEL)), const((D_MODEL, 2 * D_LRU)), const((CONV_WIDTH, D_LRU)), const((1, D_LRU)),
            const((N_LRU_BLOCKS, LRU_BLOCK, LRU_BLOCK)), const((1, D_LRU)),
            const((N_LRU_BLOCKS, LRU_BLOCK, LRU_BLOCK)), const((1, D_LRU)), const((1, D_LRU)),
            const((D_LRU, D_MODEL)), const((hist, D_LRU)), const((B, D_LRU)),
        ] + shift_in,
        out_specs=[
            pl.BlockSpec((B, tt, D_MODEL), lambda i: (0, i, 0)),
            pl.BlockSpec((hist, D_LRU), lambda i: (0, 0)),
            pl.BlockSpec((B, D_LRU), lambda i: (0, 0)),
        ] + shift_out,
        out_shape=[
            jax.ShapeDtypeStruct((B, S, D_MODEL), F32),
            jax.ShapeDtypeStruct((hist, D_LRU), F32),
            jax.ShapeDtypeStruct((B, D_LRU), F32),
        ] + shift_shapes,
        scratch_shapes=[
            pltpu.VMEM((hist + R, D_LRU), F32),
            pltpu.VMEM((R, D_LRU), F32),
            pltpu.VMEM((R, D_LRU), BF16),
            pltpu.VMEM((B, D_LRU), F32),
        ],
        compiler_params=pltpu.CompilerParams(dimension_semantics=("arbitrary",), vmem_limit_bytes=VMEM_LIMIT),
        name="lru_branch",
    )(x, gn, w_lru, cw, cb, wa, ba, wx, bx, lam, wp, conv0, h0, *shift_args)


def _attn_body(x_ref, ya_ref, gn_ref, wqkv_ref, wg_ref, gq_ref, gk_ref, bm_ref, relb_ref, wap_ref, wout_ref,
               y_ref, kv0_ref, kv1_ref, kv2_ref,
               zq_ref, zk_ref, zv_ref, qc0, qc1, qc2, hk0, hv0, hk1, hv1, hk2, hv2,
               o0, o1, o2, l0, l1, l2, bt0, bt1, bt2, s_ref, p_ref, *, S, tq):
    t = pl.program_id(1)
    nt = S // tq

    qcs, hks, hvs = (qc0, qc1, qc2), (hk0, hk1, hk2), (hv0, hv1, hv2)
    o_refs, l_refs, bts, kvs = (o0, o1, o2), (l0, l1, l2), (bt0, bt1, bt2), (kv0_ref, kv1_ref, kv2_ref)
    Qc = tuple(tq // d for d in DILATIONS)
    QB = tuple(min(q, N_BACK) for q in Qc)
    pad1 = KEY_WIN - QB[1]
    n2 = S // DILATIONS[2]

    @pl.when((pl.program_id(0) == 0) & (t == 0))
    def _():
        for ref in (hk0, hv0, hk1, hv1, hk2, hv2):
            ref[...] = jnp.zeros(ref.shape, ref.dtype)
        for g in (0, 1):
            a = lax.broadcasted_iota(jnp.int32, (QB[g], KEY_WIN), 0)
            c = lax.broadcasted_iota(jnp.int32, (QB[g], KEY_WIN), 1)
            j = a + (KEY_WIN - QB[g]) - c
            for h in range(HEADS_PER_GROUP):
                bts[g][h] = _bias_table(relb_ref, g * HEADS_PER_GROUP + h, j * DILATIONS[g], (j >= 0) & (j <= N_BACK))
        i = lax.broadcasted_iota(jnp.int32, (n2, n2), 0)
        c = lax.broadcasted_iota(jnp.int32, (n2, n2), 1)
        for h in range(HEADS_PER_GROUP):
            bt2[h] = _bias_table(relb_ref, 2 * HEADS_PER_GROUP + h, (i - c) * DILATIONS[2],
                                 (i - c >= 0) & (i - c <= N_BACK))

    x = x_ref[...]
    u = _rms(x, gn_ref[...]).astype(BF16)
    z_refs = (zq_ref, zk_ref, zv_ref)

    def store(part, head, value):
        z_refs[part][head] = value

    _qkv(u, wqkv_ref, gq_ref, gk_ref, store)

    for g in range(N_GROUPS):
        keep = min(WINDOWS[g], S)
        rb = min(tq, keep)

        @pl.when(t >= nt - keep // rb)
        def _(g=g, rb=rb):
            for part, z_ref in enumerate((zk_ref, zv_ref)):
                for h in range(HEADS_PER_GROUP):
                    dst = pl.ds(part * HEADS_PER_GROUP + h, rb, stride=KV_ROWS)
                    kvs[g][dst, :] = z_ref[g * HEADS_PER_GROUP + h, tq - rb:tq, :]

    for g in range(N_GROUPS):
        d = DILATIONS[g]
        for r in range(d):
            rows = pl.ds(r, Qc[g], stride=d) if d > 1 else slice(None)
            if g == 0:
                dst = pl.ds(tq, tq)
            elif g == 1:
                dst = pl.ds(pl.multiple_of(pad1 + t * Qc[1], Qc[1]), Qc[1])
            else:
                dst = pl.ds(pl.multiple_of(t * Qc[2], Qc[2]), Qc[2])
            for h in range(HEADS_PER_GROUP):
                hs_ = slice(h * HEAD_DIM, (h + 1) * HEAD_DIM)
                gh = g * HEADS_PER_GROUP + h
                qcs[g][r, :, hs_] = zq_ref[gh, rows, :].astype(BF16)
                hks[g][r, dst, hs_] = zk_ref[gh, rows, :].astype(BF16)
                hvs[g][r, dst, hs_] = zv_ref[gh, rows, :].astype(BF16)

    col = lax.broadcasted_iota(jnp.int32, (1, KEY_WIN), 1)
    rowi = lax.broadcasted_iota(jnp.int32, (tq, 1), 0)
    for g in range(N_GROUPS):
        d = DILATIONS[g]
        nq = Qc[g] // QB[g]
        kw = n2 if g == 2 else KEY_WIN
        units = [(r, qb) for r in range(d) for qb in range(nq)]

        def key_win(qb, g=g):
            if g == 0:
                return pl.ds(tq + qb * QB[0] + QB[0] - KEY_WIN, KEY_WIN)
            if g == 1:
                return pl.ds(pl.multiple_of(t * Qc[1], QB[1]), KEY_WIN)
            return slice(None)

        for r, qb in units:
            cm = slice(r * Qc[g] + qb * QB[g], r * Qc[g] + (qb + 1) * QB[g])
            for h in range(HEADS_PER_GROUP):
                hs_ = slice(h * HEAD_DIM, (h + 1) * HEAD_DIM)
                s_ref[h, cm, 0:kw] = _dot_nt(qcs[g][r, qb * QB[g]:(qb + 1) * QB[g], hs_], hks[g][r, key_win(qb), hs_])

        if g < 2:
            i0 = t * Qc[g] + (rowi & (Qc[g] - QB[g]))
            started = col >= KEY_WIN - QB[g] - i0
        for h in range(HEADS_PER_GROUP):
            bias = bt2[h, pl.ds(pl.multiple_of(t * QB[2], QB[2]), QB[2]), :] if g == 2 else bts[g][h]
            s = s_ref[h, :, 0:kw] * ATTN_SCALE
            s = (s.reshape(len(units), QB[g], kw) + bias[None]).reshape(tq, kw)
            if g < 2:
                s = jnp.where(started, s, NEG)
            m = jnp.max(s, axis=-1, keepdims=True)
            p = jnp.exp(s - m)
            den = jnp.sum(p, axis=-1, keepdims=True)
            p_ref[h, :, 0:kw] = (p * (1.0 / den)).astype(BF16)
            lse = jnp.broadcast_to(m + jnp.log(den), (tq, HEAD_DIM))
            for r in range(d):
                rows = pl.ds(r, Qc[g], stride=d) if d > 1 else slice(None)
                l_refs[g][h, rows, :] = lse[r * Qc[g]:(r + 1) * Qc[g]]

        for r, qb in units:
            cm = slice(r * Qc[g] + qb * QB[g], r * Qc[g] + (qb + 1) * QB[g])
            rows = pl.ds(r + d * qb * QB[g], QB[g], stride=d) if d > 1 else pl.ds(qb * QB[g], QB[g])
            for h in range(HEADS_PER_GROUP):
                hs_ = slice(h * HEAD_DIM, (h + 1) * HEAD_DIM)
                o_refs[g][h, rows, :] = _dot(p_ref[h, cm, 0:kw], hvs[g][r, key_win(qb), hs_])

    hk0[0, 0:tq, :] = hk0[0, tq:2 * tq, :]
    hv0[0, 0:tq, :] = hv0[0, tq:2 * tq, :]

    ob = _merge_groups(o_refs, l_refs)
    ob = jnp.concatenate([ob[h] for h in range(HEADS_PER_GROUP)], axis=1)
    y_ref[...] = _tail(x, u, ya_ref[...].astype(F32), ob, wg_ref, bm_ref, wap_ref, wout_ref)


def _attn_call(x, ya, gn, wqkv, wg, gq, gk, bmerge, relb, wap, wout, *, tq):
    B, S, _ = x.shape
    nt = S // tq
    assert S % tq == 0 and tq % (DILATIONS[2] * 16) == 0 and tq >= N_BACK
    assert S // DILATIONS[2] == N_BACK and tq // DILATIONS[1] <= N_BACK
    Qc = tuple(tq // d for d in DILATIONS)
    QB = tuple(min(q, N_BACK) for q in Qc)
    n2 = S // DILATIONS[2]
    const = lambda shape: pl.BlockSpec(shape, lambda b, t: (0,) * len(shape), pipeline_mode=pl.Buffered(1))
    row = pl.BlockSpec((None, tq, D_MODEL), lambda b, t: (b, t, 0))

    def kv_spec(g):
        keep = min(WINDOWS[g], S)
        rb = min(tq, keep)
        first = nt - keep // rb
        return pl.BlockSpec((None, rb * KV_ROWS, HEAD_DIM), lambda b, t: (b, jnp.maximum(t - first, 0), 0))

    cls = lambda g, rows, dt: pltpu.VMEM((DILATIONS[g], rows, D_GRP), dt)
    scratch = [pltpu.VMEM((D_QKV // HEAD_DIM, tq, HEAD_DIM), F32)] * 3
    scratch += [cls(g, Qc[g], BF16) for g in range(N_GROUPS)]
    scratch += [cls(0, 2 * tq, BF16)] * 2 + [cls(1, KEY_WIN - QB[1] + S // DILATIONS[1], BF16)] * 2 + [cls(2, n2, BF16)] * 2
    scratch += [pltpu.VMEM((HEADS_PER_GROUP, tq, HEAD_DIM), F32)] * 6
    scratch += [pltpu.VMEM((HEADS_PER_GROUP, QB[0], KEY_WIN), F32), pltpu.VMEM((HEADS_PER_GROUP, QB[1], KEY_WIN), F32),
                pltpu.VMEM((HEADS_PER_GROUP, n2, n2), F32)]
    scratch += [pltpu.VMEM((HEADS_PER_GROUP, tq, KEY_WIN), F32), pltpu.VMEM((HEADS_PER_GROUP, tq, KEY_WIN), BF16)]
    return pl.pallas_call(
        functools.partial(_attn_body, S=S, tq=tq),
        grid=(B, nt),
        in_specs=[row, row, const((1, D_MODEL)), const((D_MODEL, 3 * D_QKV)), const((D_MODEL, D_GRP + 2 * D_MODEL)),
                  const((1, HEAD_DIM)), const((1, HEAD_DIM)), const((1, 2 * D_MODEL)),
                  pl.BlockSpec(memory_space=pltpu.SMEM), const((D_GRP, D_MODEL)), const((D_MODEL, D_MODEL))],
        out_specs=[row, kv_spec(0), kv_spec(1), kv_spec(2)],
        out_shape=[jax.ShapeDtypeStruct((B, S, D_MODEL), F32)]
        + [jax.ShapeDtypeStruct((B, min(WINDOWS[g], S) * KV_ROWS, HEAD_DIM), F32) for g in range(N_GROUPS)],
        scratch_shapes=scratch,
        compiler_params=pltpu.CompilerParams(dimension_semantics=("arbitrary", "arbitrary"), vmem_limit_bytes=VMEM_LIMIT),
        name="attn_branch",
    )(x, ya, gn, wqkv, wg, gq, gk, bmerge, relb, wap, wout)


def _qkv_body(x_ref, gn_ref, wqkv_ref, gq_ref, gk_ref, zq_ref, zk_ref, zv_ref, nr0_ref, nr1_ref, nr2_ref):
    M = x_ref.shape[0]
    u = _rms(x_ref[...], gn_ref[...]).astype(BF16)
    z_refs = (zq_ref, zk_ref, zv_ref)
    nr_refs = (nr0_ref, nr1_ref, nr2_ref)

    def store(part, head, value):
        z_refs[part][:, head * HEAD_DIM:(head + 1) * HEAD_DIM] = value
        if part > 0:
            g, h = divmod(head, HEADS_PER_GROUP)
            nr_refs[g][pl.ds((part - 1) * HEADS_PER_GROUP + h, M, stride=KV_ROWS), :] = value

    _qkv(u, wqkv_ref, gq_ref, gk_ref, store)


def _qkv_call(x, gn, wqkv, gq, gk):
    M = x.shape[0]
    out = jax.ShapeDtypeStruct((M, D_QKV), F32)
    rows = jax.ShapeDtypeStruct((M * KV_ROWS, HEAD_DIM), F32)
    return pl.pallas_call(_qkv_body, out_shape=[out, out, out, rows, rows, rows],
                          compiler_params=pltpu.CompilerParams(vmem_limit_bytes=VMEM_LIMIT),
                          name="decode_qkv")(x, gn, wqkv, gq, gk)


def _cache_attn_body(q_ref, k_ref, v_ref, c0_ref, c1_ref, c2_ref, relb_ref, o_ref, lse_ref,
                     kc0, vc0, kc1, vc1, kc2, vc2, bt0, bt1, bt2, *, T):
    cache_refs, kcs, vcs, bts = (c0_ref, c1_ref, c2_ref), (kc0, kc1, kc2), (vc0, vc1, vc2), (bt0, bt1, bt2)
    Ls = tuple(c.shape[0] // KV_ROWS for c in cache_refs)
    Ws = tuple(L + HEAD_DIM for L in Ls)

    @pl.when(pl.program_id(0) == 0)
    def _():
        for g in range(N_GROUPS):
            L, W, d = Ls[g], Ws[g], DILATIONS[g]
            kcs[g][L:W, :] = jnp.zeros((W - L, D_GRP), BF16)
            vcs[g][L:W, :] = jnp.zeros((W - L, D_GRP), BF16)
            tq = lax.broadcasted_iota(jnp.int32, (T, W), 0)
            p = lax.broadcasted_iota(jnp.int32, (T, W), 1)
            dist = L + tq - p
            valid = (dist >= 0) & ((dist & (d - 1)) == 0) & (dist <= N_BACK * d) & (p < L + T)
            for h in range(HEADS_PER_GROUP):
                bts[g][h * T:(h + 1) * T, :] = _bias_table(relb_ref, g * HEADS_PER_GROUP + h, dist, valid)

    for g in range(N_GROUPS):
        L = Ls[g]
        gs = slice(g * D_GRP, (g + 1) * D_GRP)
        for h in range(HEADS_PER_GROUP):
            hs_ = slice(h * HEAD_DIM, (h + 1) * HEAD_DIM)
            kcs[g][0:L, hs_] = cache_refs[g][pl.ds(h, L, stride=KV_ROWS), :].astype(BF16)
            vcs[g][0:L, hs_] = cache_refs[g][pl.ds(HEADS_PER_GROUP + h, L, stride=KV_ROWS), :].astype(BF16)
        k_new, v_new = k_ref[:, gs], v_ref[:, gs]
        kcs[g][L:L + 2 * T, :] = jnp.concatenate([k_new, jnp.zeros_like(k_new)], axis=0).astype(BF16)
        vcs[g][L:L + 2 * T, :] = jnp.concatenate([v_new, jnp.zeros_like(v_new)], axis=0).astype(BF16)
    for g in range(N_GROUPS):
        heads = [slice(g * D_GRP + h * HEAD_DIM, g * D_GRP + (h + 1) * HEAD_DIM) for h in range(HEADS_PER_GROUP)]
        s = jnp.concatenate([_dot_nt(q_ref[:, hs_].astype(BF16), kcs[g][:, h * HEAD_DIM:(h + 1) * HEAD_DIM])
                             for h, hs_ in enumerate(heads)], axis=0)
        s = s * ATTN_SCALE + bts[g][...]
        m = jnp.max(s, axis=-1, keepdims=True)
        p = jnp.exp(s - m)
        den = jnp.sum(p, axis=-1, keepdims=True)
        p = p * (1.0 / den)
        lse = m + jnp.log(den)
        for h, hs_ in enumerate(heads):
            rows = slice(h * T, (h + 1) * T)
            o_ref[:, hs_] = _dot(p[rows].astype(BF16), vcs[g][:, h * HEAD_DIM:(h + 1) * HEAD_DIM])
            lse_ref[:, hs_] = jnp.broadcast_to(lse[rows], (T, HEAD_DIM))


def _cache_attn_call(q, k, v, caches, relb, *, T):
    DB = caches[0].shape[0]
    Ls = tuple(c.shape[1] // KV_ROWS for c in caches)
    assert all(L == N_BACK * d for L, d in zip(Ls, DILATIONS)) and T % 8 == 0
    row = pl.BlockSpec((T, D_QKV), lambda b: (b, 0))
    out = jax.ShapeDtypeStruct((DB * T, D_QKV), F32)
    scratch = []
    for L in Ls:
        scratch += [pltpu.VMEM((L + HEAD_DIM, D_GRP), BF16)] * 2
    scratch += [pltpu.VMEM((HEADS_PER_GROUP * T, L + HEAD_DIM), F32) for L in Ls]
    return pl.pallas_call(
        functools.partial(_cache_attn_body, T=T),
        grid=(DB,),
        in_specs=[row, row, row] + [pl.BlockSpec((None, c.shape[1], HEAD_DIM), lambda b: (b, 0, 0)) for c in caches]
        + [pl.BlockSpec(memory_space=pltpu.SMEM)],
        out_specs=[row, row],
        out_shape=[out, out],
        scratch_shapes=scratch,
        compiler_params=pltpu.CompilerParams(dimension_semantics=("arbitrary",), vmem_limit_bytes=VMEM_LIMIT),
        name="cache_attn",
    )(q, k, v, *caches, relb)


def _tail_body(x_ref, ya_ref, gn_ref, wg_ref, bm_ref, wap_ref, wout_ref, o_ref, lse_ref, y_ref):
    x = x_ref[...]
    u = _rms(x, gn_ref[...]).astype(BF16)
    groups = [slice(g * D_GRP, (g + 1) * D_GRP) for g in range(N_GROUPS)]
    ob = _merge_groups([o_ref.at[:, gs] for gs in groups], [lse_ref.at[:, gs] for gs in groups])
    y_ref[...] = _tail(x, u, ya_ref[...], ob, wg_ref, bm_ref, wap_ref, wout_ref)


def _tail_call(x, ya, gn, wg, bmerge, wap, wout, o, lse):
    return pl.pallas_call(_tail_body, out_shape=jax.ShapeDtypeStruct(x.shape, F32),
                          compiler_params=pltpu.CompilerParams(vmem_limit_bytes=VMEM_LIMIT),
                          name="decode_tail")(x, ya, gn, wg, bmerge, wap, wout, o, lse)


def kernel(x_prompt, x_sample, cache_kv_w128, cache_kv_w512, cache_kv_w2048, state_conv, state_h, g_norm, w_in,
           b_merge, conv_w, conv_b, lru_w_a, lru_b_a, lru_w_x, lru_b_x, lru_lambda, g_q, g_k, rel_bias,
           w_lru_proj, w_attn_proj, w_out):
    assert w_in.shape[0] == 1, "single-layer step"
    B, S, _ = x_prompt.shape
    DB, T, _ = x_sample.shape
    o2 = 2 * D_LRU
    o5 = o2 + 3 * D_QKV
    w_lru = w_in[0, :, :o2].astype(BF16)
    wqkv = w_in[0, :, o2:o5].astype(BF16)
    wg = w_in[0, :, o5:].astype(BF16)
    wa, wx = lru_w_a[0].astype(BF16), lru_w_x[0].astype(BF16)
    wp, wap, wout = w_lru_proj[0].astype(BF16), w_attn_proj[0].astype(BF16), w_out[0].astype(BF16)
    lru_params = (g_norm, w_lru, conv_w[0], conv_b, wa, lru_b_a, wx, lru_b_x, lru_lambda, wp)
    hist = CONV_WIDTH - 1

    caches6 = (cache_kv_w128, cache_kv_w512, cache_kv_w2048)
    caches = [c.reshape(DB, c.shape[2] * KV_ROWS, HEAD_DIM) for c in caches6]
    flat = lambda a: a.reshape(-1, HEAD_DIM)

    xs2 = x_sample.reshape(DB * T, D_MODEL)
    zq, zk, zv, *newrows = _qkv_call(xs2, g_norm, wqkv, g_q, g_k)

    ya_p, conv_p, h_p, *news = _lru_call(x_prompt, jnp.zeros((hist * B, D_LRU), F32), jnp.zeros((B, D_LRU), F32),
                                         *lru_params, tt=32, caches=[flat(c) for c in caches], fresh=newrows,
                                         n_cache_batch=DB)
    y_p, kv0_p, kv1_p, kv2_p = _attn_call(x_prompt, ya_p, g_norm, wqkv, wg, g_q, g_k, b_merge, rel_bias, wap, wout,
                                          tq=256)

    conv0_s = jnp.swapaxes(state_conv[0], 0, 1).reshape(hist * DB, D_LRU)
    ya_s, conv_s, h_s = _lru_call(x_sample, conv0_s, state_h[0], *lru_params, tt=T)
    o_s, lse_s = _cache_attn_call(zq, zk, zv, caches, rel_bias, T=T)
    y_s = _tail_call(xs2, ya_s.reshape(DB * T, D_MODEL), g_norm, wg, b_merge, wap, wout, o_s, lse_s)

    kv_shape = lambda a: a.reshape(1, a.shape[0], a.shape[1] // KV_ROWS, 2, HEADS_PER_GROUP, HEAD_DIM)
    conv_out = lambda c, nb: jnp.swapaxes(c.reshape(hist, nb, D_LRU), 0, 1)[None]
    news = [n.reshape(c.shape) for n, c in zip(news, caches6)]
    return (y_p, y_s.reshape(DB, T, D_MODEL), kv_shape(kv0_p), kv_shape(kv1_p), kv_shape(kv2_p),
            conv_out(conv_p, B), h_p[None], news[0], news[1], news[2], conv_out(conv_s, DB), h_s[None])
```

```python
import functools
import math

import jax
import jax.numpy as jnp
from jax import lax
from jax.experimental import pallas as pl
from jax.experimental.pallas import tpu as pltpu

F32 = jnp.float32
BF16 = jnp.bfloat16

D_MODEL = 1024
D_LRU = 1024
N_LRU_BLOCKS = 8
LRU_BLOCK = D_LRU // N_LRU_BLOCKS
CONV_WIDTH = 4
LRU_C = 8.0
HEAD_DIM = 128
HEADS_PER_GROUP = 4
WINDOWS = (128, 512, 2048)
DILATIONS = (1, 4, 16)
N_GROUPS = 3
N_BACK = 128
D_QKV = N_GROUPS * HEADS_PER_GROUP * HEAD_DIM
D_GRP = HEADS_PER_GROUP * HEAD_DIM
ATTN_SCALE = HEAD_DIM ** -0.5
N_BUCKETS = 32
MAX_DISTANCE = 2048
NORM_EPS = 1e-6
NEG = -1e30
KEY_WIN = 256
KV_ROWS = 2 * HEADS_PER_GROUP
VMEM_LIMIT = 60000 * 1024


def _rms(x, g):
    ms = jnp.mean(x * x, axis=-1, keepdims=True)
    return x * lax.rsqrt(ms + NORM_EPS) * g


def _sigmoid(x):
    return 0.5 * jnp.tanh(0.5 * x) + 0.5


def _dot(a, b):
    return jnp.dot(a, b, preferred_element_type=F32)


def _dot_nt(a, b):
    return lax.dot_general(a, b, (((1,), (1,)), ((), ())), preferred_element_type=F32)


def _bias_table(relb_ref, col, dist, valid):
    max_exact = N_BUCKETS // 2
    n_log = N_BUCKETS - max_exact
    df = jnp.maximum(dist, 1).astype(F32)
    val = jnp.log(df / max_exact) / math.log(MAX_DISTANCE / max_exact) * n_log
    out = jnp.zeros(dist.shape, F32)
    for b in range(max_exact):
        out = jnp.where(dist == b, relb_ref[b, col], out)
    for k in range(n_log):
        out = jnp.where((dist >= max_exact) & (val >= k), relb_ref[max_exact + k, col], out)
    return jnp.where(valid, out, NEG)


def _softmax_pv(s, v):
    m = jnp.max(s, axis=-1, keepdims=True)
    p = jnp.exp(s - m)
    den = jnp.sum(p, axis=-1, keepdims=True)
    o = _dot(p.astype(BF16), v) / den
    return o, m + jnp.log(den)


def _merge_groups(o_refs, lse_refs):
    l0, l1, l2 = (r[...] for r in lse_refs)
    m = jnp.maximum(jnp.maximum(l0, l1), l2)
    e0, e1, e2 = jnp.exp(l0 - m), jnp.exp(l1 - m), jnp.exp(l2 - m)
    num = e0 * o_refs[0][...] + e1 * o_refs[1][...] + e2 * o_refs[2][...]
    return num / (e0 + e1 + e2)


N_GATE_CHUNKS = (D_GRP + 2 * D_MODEL) // D_GRP


def _gate_chunk(u, wg_ref, bm_ref, c):
    z = _dot(u, wg_ref[:, c * D_GRP:(c + 1) * D_GRP])
    if c == 0:
        return z * _sigmoid(z)
    return _sigmoid(z + bm_ref[:, (c - 1) * D_GRP:c * D_GRP])


def _tail(x, ya, ob, gate, wap_ref, wout_ref):
    yb = _dot((ob * gate[:, :D_GRP]).astype(BF16), wap_ref[...])
    merged = gate[:, D_GRP:D_GRP + D_MODEL] * ya + gate[:, D_GRP + D_MODEL:] * yb
    return x + _dot(merged.astype(BF16), wout_ref[...])


def _qkv(u, wqkv_ref, gq_ref, gk_ref, store):
    for part, g_ref in enumerate((gq_ref, gk_ref, None)):
        z = _dot(u, wqkv_ref[:, part * D_QKV:(part + 1) * D_QKV])
        zs = [z[:, h * HEAD_DIM:(h + 1) * HEAD_DIM] for h in range(D_QKV // HEAD_DIM)]
        if g_ref is not None:
            ms = [jnp.mean(zh * zh, axis=-1, keepdims=True) for zh in zs]
            zs = [zh * lax.rsqrt(m + NORM_EPS) * g_ref[...] for zh, m in zip(zs, ms)]
        for h, zh in enumerate(zs):
            store(part, h, zh)


def _lru_body(x_ref, gn_ref, w_ref, cw_ref, cb_ref, wa_ref, ba_ref, wx_ref, bx_ref, lam_ref, wp_ref,
              conv0_ref, h0_ref, ya_ref, convo_ref, ho_ref, xs_ref, ga_ref, a_ref, hs_ref, h_ref, *, B, tt):
    R = B * tt
    hist = (CONV_WIDTH - 1) * B

    @pl.when(pl.program_id(0) == 0)
    def _():
        xs_ref[0:hist, :] = conv0_ref[...]
        h_ref[...] = h0_ref[...]

    xt =jnp.concatenate([x_ref[:, t, :] for t in range(tt)], axis=0)
    u = _rms(xt, gn_ref[...]).astype(BF16)
    xs_ref[hist:hist + R, :] = _dot(u, w_ref[:, :D_LRU])
    ga_ref[...] = _dot(u, w_ref[:, D_LRU:])
    sp = jax.nn.softplus(-lam_ref[...])
    for n in range(N_LRU_BLOCKS):
        cs = slice(n * LRU_BLOCK, (n + 1) * LRU_BLOCK)
        y = cb_ref[:, cs] + sum(xs_ref[j * B:j * B + R, cs] * cw_ref[j:j + 1, cs] for j in range(CONV_WIDTH))
        yb = y.astype(BF16)
        r = _sigmoid(_dot(yb, wa_ref[n]) + ba_ref[:, cs])
        i = _sigmoid(_dot(yb, wx_ref[n]) + bx_ref[:, cs])
        log_a = -LRU_C * r * sp[:, cs]
        a = jnp.exp(log_a)
        th = jnp.tanh(log_a)
        a_ref[:, cs] = a
        hs_ref[:, cs] = jnp.sqrt(-2.0 * th) * lax.rsqrt(1.0 - th) * (i * y)
    h = h_ref[...]
    for t in range(tt):
        rows = slice(t * B, (t + 1) * B)
        h = a_ref[rows, :] * h + hs_ref[rows, :]
        hs_ref[rows, :] = h
    h_ref[...] = h
    g = ga_ref[...]
    ya = _dot((hs_ref[...] * (g * _sigmoid(g))).astype(BF16), wp_ref[...])
    for t in range(tt):
        ya_ref[:, t, :] = ya[t * B:(t + 1) * B].astype(ya_ref.dtype)
    tail = xs_ref[R:R + hist, :]
    convo_ref[...] = tail
    xs_ref[0:hist, :] = tail
    ho_ref[...] = h_ref[...]


def _lru_call(x, conv0, h0, gn, w_lru, cw, cb, wa, ba, wx, bx, lam, wp, *, tt):
    B, S, _ = x.shape
    assert S % tt == 0 and tt >= CONV_WIDTH - 1 and tt % 8 == 0 and B % 8 == 0
    R = B * tt
    hist = (CONV_WIDTH - 1) * B
    const = lambda shape: pl.BlockSpec(shape, lambda i: (0,) * len(shape), pipeline_mode=pl.Buffered(1))
    return pl.pallas_call(
        functools.partial(_lru_body, B=B, tt=tt),
        grid=(S // tt,),
        in_specs=[
            pl.BlockSpec((B, tt, D_MODEL), lambda i: (0, i, 0)),
            const((1, D_MODEL)), const((D_MODEL, 2 * D_LRU)), const((CONV_WIDTH, D_LRU)), const((1, D_LRU)),
            const((N_LRU_BLOCKS, LRU_BLOCK, LRU_BLOCK)), const((1, D_LRU)),
            const((N_LRU_BLOCKS, LRU_BLOCK, LRU_BLOCK)), const((1, D_LRU)), const((1, D_LRU)),
            const((D_LRU, D_MODEL)), const((hist, D_LRU)), const((B, D_LRU)),
        ],
        out_specs=[
            pl.BlockSpec((B, tt, D_MODEL), lambda i: (0, i, 0)),
            pl.BlockSpec((hist, D_LRU), lambda i: (0, 0)),
            pl.BlockSpec((B, D_LRU), lambda i: (0, 0)),
        ],
        out_shape=[
            jax.ShapeDtypeStruct((B, S, D_MODEL), F32),
            jax.ShapeDtypeStruct((hist, D_LRU), F32),
            jax.ShapeDtypeStruct((B, D_LRU), F32),
        ],
        scratch_shapes=[
            pltpu.VMEM((hist + R, D_LRU), F32),
            pltpu.VMEM((R, D_LRU), F32),
            pltpu.VMEM((R, D_LRU), F32),
            pltpu.VMEM((R, D_LRU), F32),
            pltpu.VMEM((B, D_LRU), F32),
        ],
        compiler_params=pltpu.CompilerParams(dimension_semantics=("arbitrary",), vmem_limit_bytes=VMEM_LIMIT),
        name="lru_branch",
    )(x, gn, w_lru, cw, cb, wa, ba, wx, bx, lam, wp, conv0, h0)


SHIFT_CHUNK_ROWS = 2048
N_SHIFT_BUFS = 2


class _CacheShift:
    def __init__(self, n_steps, n_batch, cache_refs, fresh_refs, out_refs, buf_ref, in_sem, out_sem):
        self.refs = list(zip(cache_refs, fresh_refs, out_refs))
        self.buf, self.in_sem, self.out_sem = buf_ref, in_sem, out_sem
        self.geom = []
        self.chunks = []
        for g, (c, f, _) in enumerate(self.refs):
            share, rows_b, hop = c.shape[0] // n_steps, c.shape[0] // n_batch, f.shape[0] // n_batch
            self.geom.append((share, rows_b, hop))
            n = min(share, SHIFT_CHUNK_ROWS)
            self.chunks += [(g, off, n) for off in range(0, share, n)]
        assert len(self.chunks) % N_SHIFT_BUFS == 0

    def _chunk(self, step, k):
        g, off, n = self.chunks[k]
        share, rows_b, hop = self.geom[g]
        r = pl.multiple_of(step * share + off, hop)
        slot = k % N_SHIFT_BUFS
        return g, n, r, rows_b, hop, slot

    def fetch(self, step, k, do):
        g, n, r, rows_b, hop, slot = self._chunk(step, k)
        c, f, _ = self.refs[g]
        ends_batch = (r + n) % rows_b == 0

        @pl.when(jnp.logical_not(ends_batch))
        def _():
            do(pltpu.make_async_copy(c.at[pl.ds(r + hop, n), :], self.buf.at[slot, pl.ds(0, n), :], self.in_sem.at[slot]))

        @pl.when(ends_batch)
        def _():
            fresh = f.at[pl.ds(pl.multiple_of(r // rows_b * hop, hop), hop), :]
            do(pltpu.make_async_copy(c.at[pl.ds(r + hop, n - hop), :], self.buf.at[slot, pl.ds(0, n - hop), :],
                                     self.in_sem.at[slot]))
            do(pltpu.make_async_copy(fresh, self.buf.at[slot, pl.ds(n - hop, hop), :], self.in_sem.at[slot]))

    def drain(self, step, k, do):
        g, n, r, _, _, slot = self._chunk(step, k)
        do(pltpu.make_async_copy(self.buf.at[slot, pl.ds(0, n), :], self.refs[g][2].at[pl.ds(r, n), :],
                                 self.out_sem.at[slot]))

    def point(self, step, last_step, k):
        start, wait = (lambda cp: cp.start()), (lambda cp: cp.wait())
        nk = len(self.chunks)
        if k == 0:
            pl.when(step == 0)(lambda: self.fetch(step, 0, start))
        self.fetch(step, k, wait)
        self.drain(step, k, start)
        if k > 0:
            self.drain(step, k - 1, wait)
        else:
            pl.when(step > 0)(lambda: self.drain(step - 1, nk - 1, wait))
        if k + 1 < nk:
            self.fetch(step, k + 1, start)
        else:
            pl.when(step < last_step)(lambda: self.fetch(step + 1, 0, start))
            pl.when(step == last_step)(lambda: self.drain(step, k, wait))


def _attn_body(x_ref, ya_ref, gn_ref, wqkv_ref, wg_ref, gq_ref, gk_ref, bm_ref, relb_ref, wap_ref, wout_ref,
               c0_ref, c1_ref, c2_ref, f0_ref, f1_ref, f2_ref,
               y_ref, kv0_ref, kv1_ref, kv2_ref, n0_ref, n1_ref, n2_ref,
               zq_ref, zk_ref, zv_ref, qc0, qc1, qc2, hk0, hv0, hk1, hv1, hk2, hv2,
               o0, o1, o2, l0, l1, l2, bt0, bt1, bt2, s_ref, p_ref, bounce_ref, in_sem, out_sem,
               *, B, S, tq, n_cache_batch):
    t = pl.program_id(1)
    nt = S // tq

    step, last_step = pl.program_id(0) * nt + t, B * nt - 1
    shift = _CacheShift(B * nt, n_cache_batch, (c0_ref, c1_ref, c2_ref), (f0_ref, f1_ref, f2_ref),
                        (n0_ref, n1_ref, n2_ref), bounce_ref, in_sem, out_sem)
    assert len(shift.chunks) == 3 + N_GROUPS
    shift.point(step, last_step, 0)

    qcs, hks, hvs = (qc0, qc1, qc2), (hk0, hk1, hk2), (hv0, hv1, hv2)
    o_refs, l_refs, bts, kvs = (o0, o1, o2), (l0, l1, l2), (bt0, bt1, bt2), (kv0_ref, kv1_ref, kv2_ref)
    Qc = tuple(tq // d for d in DILATIONS)
    QB = tuple(min(q, N_BACK) for q in Qc)
    pad1 = KEY_WIN - QB[1]
    n2 = S // DILATIONS[2]

    @pl.when((pl.program_id(0) == 0) & (t == 0))
    def _():
        for ref in (hk0, hv0, hk1, hv1, hk2, hv2):
            ref[...] = jnp.zeros(ref.shape, ref.dtype)
        for g in (0, 1):
            a = lax.broadcasted_iota(jnp.int32, (QB[g], KEY_WIN), 0)
            c = lax.broadcasted_iota(jnp.int32, (QB[g], KEY_WIN), 1)
            j = a + (KEY_WIN - QB[g]) - c
            for h in range(HEADS_PER_GROUP):
                bts[g][h] = _bias_table(relb_ref, g * HEADS_PER_GROUP + h, j * DILATIONS[g], (j >= 0) & (j <= N_BACK))
        i = lax.broadcasted_iota(jnp.int32, (n2, n2), 0)
        c = lax.broadcasted_iota(jnp.int32, (n2, n2), 1)
        for h in range(HEADS_PER_GROUP):
            bt2[h] = _bias_table(relb_ref, 2 * HEADS_PER_GROUP + h, (i - c) * DILATIONS[2],
                                 (i - c >= 0) & (i - c <= N_BACK))

    x = x_ref[...]
    u = _rms(x, gn_ref[...]).astype(BF16)
    z_refs = (zq_ref, zk_ref, zv_ref)

    def store(part, head, value):
        z_refs[part][head] = value

    _qkv(u, wqkv_ref, gq_ref, gk_ref, store)
    shift.point(step, last_step, 1)

    for g in range(N_GROUPS):
        keep = min(WINDOWS[g], S)
        rb = min(tq, keep)

        @pl.when(t >= nt - keep // rb)
        def _(g=g, rb=rb):
            for part, z_ref in enumerate((zk_ref, zv_ref)):
                for h in range(HEADS_PER_GROUP):
                    dst = pl.ds(part * HEADS_PER_GROUP + h, rb, stride=KV_ROWS)
                    kvs[g][dst, :] = z_ref[g * HEADS_PER_GROUP + h, tq - rb:tq, :]

    for g in range(N_GROUPS):
        d = DILATIONS[g]
        for r in range(d):
            rows = pl.ds(r, Qc[g], stride=d) if d > 1 else slice(None)
            if g == 0:
                dst = pl.ds(tq, tq)
            elif g == 1:
                dst = pl.ds(pl.multiple_of(pad1 + t * Qc[1], Qc[1]), Qc[1])
            else:
                dst = pl.ds(pl.multiple_of(t * Qc[2], Qc[2]), Qc[2])
            for h in range(HEADS_PER_GROUP):
                hs_ = slice(h * HEAD_DIM, (h + 1) * HEAD_DIM)
                gh = g * HEADS_PER_GROUP + h
                qcs[g][r, :, hs_] = zq_ref[gh, rows, :].astype(BF16)
                hks[g][r, dst, hs_] = zk_ref[gh, rows, :].astype(BF16)
                hvs[g][r, dst, hs_] = zv_ref[gh, rows, :].astype(BF16)

    shift.point(step, last_step, 2)

    col = lax.broadcasted_iota(jnp.int32, (1, KEY_WIN), 1)
    rowi = lax.broadcasted_iota(jnp.int32, (tq, 1), 0)
    for g in range(N_GROUPS):
        d = DILATIONS[g]
        nq = Qc[g] // QB[g]
        kw = n2 if g == 2 else KEY_WIN
        units = [(r, qb) for r in range(d) for qb in range(nq)]

        def key_win(qb, g=g):
            if g == 0:
                return pl.ds(tq + qb * QB[0] + QB[0] - KEY_WIN, KEY_WIN)
            if g == 1:
                return pl.ds(pl.multiple_of(t * Qc[1], QB[1]), KEY_WIN)
            return slice(None)

        for r, qb in units:
            cm = slice(r * Qc[g] + qb * QB[g], r * Qc[g] + (qb + 1) * QB[g])
            for h in range(HEADS_PER_GROUP):
                hs_ = slice(h * HEAD_DIM, (h + 1) * HEAD_DIM)
                s_ref[h, cm, 0:kw] = _dot_nt(qcs[g][r, qb * QB[g]:(qb + 1) * QB[g], hs_], hks[g][r, key_win(qb), hs_])

        if g < 2:
            i0 = t * Qc[g] + (rowi & (Qc[g] - QB[g]))
            started = col >= KEY_WIN - QB[g] - i0
        for h in range(HEADS_PER_GROUP):
            bias = bt2[h, pl.ds(pl.multiple_of(t * QB[2], QB[2]), QB[2]), :] if g == 2 else bts[g][h]
            s = s_ref[h, :, 0:kw] * ATTN_SCALE
            s = (s.reshape(len(units), QB[g], kw) + bias[None]).reshape(tq, kw)
            if g < 2:
                s = jnp.where(started, s, NEG)
            m = jnp.max(s, axis=-1, keepdims=True)
            p = jnp.exp(s - m)
            den = jnp.sum(p, axis=-1, keepdims=True)
            p_ref[h, :, 0:kw] = (p * (1.0 / den)).astype(BF16)
            lse = jnp.broadcast_to(m + jnp.log(den), (tq, HEAD_DIM))
            for r in range(d):
                rows = pl.ds(r, Qc[g], stride=d) if d > 1 else slice(None)
                l_refs[g][h, rows, :] = lse[r * Qc[g]:(r + 1) * Qc[g]]

        for r, qb in units:
            cm = slice(r * Qc[g] + qb * QB[g], r * Qc[g] + (qb + 1) * QB[g])
            rows = pl.ds(r + d * qb * QB[g], QB[g], stride=d) if d > 1 else pl.ds(qb * QB[g], QB[g])
            for h in range(HEADS_PER_GROUP):
                hs_ = slice(h * HEAD_DIM, (h + 1) * HEAD_DIM)
                o_refs[g][h, rows, :] = _dot(p_ref[h, cm, 0:kw], hvs[g][r, key_win(qb), hs_])
        shift.point(step, last_step, 3 + g)

    hk0[0, 0:tq, :] = hk0[0, tq:2 * tq, :]
    hv0[0, 0:tq, :] = hv0[0, tq:2 * tq, :]

    ob = _merge_groups(o_refs, l_refs)
    ob = jnp.concatenate([ob[h] for h in range(HEADS_PER_GROUP)], axis=1)
    gate = jnp.concatenate([_gate_chunk(u, wg_ref, bm_ref, c) for c in range(N_GATE_CHUNKS)], axis=1)
    y_ref[...] = _tail(x, ya_ref[...].astype(F32), ob, gate, wap_ref, wout_ref)


def _attn_call(x, ya, gn, wqkv, wg, gq, gk, bmerge, relb, wap, wout, caches, fresh, *, tq, n_cache_batch):
    B, S, _ = x.shape
    nt = S // tq
    assert S % tq == 0 and tq % (DILATIONS[2] * 16) == 0 and tq >= N_BACK
    assert S // DILATIONS[2] == N_BACK and tq // DILATIONS[1] <= N_BACK
    Qc = tuple(tq // d for d in DILATIONS)
    QB = tuple(min(q, N_BACK) for q in Qc)
    n2 = S // DILATIONS[2]
    const = lambda shape: pl.BlockSpec(shape, lambda b, t: (0,) * len(shape), pipeline_mode=pl.Buffered(1))
    row = pl.BlockSpec((None, tq, D_MODEL), lambda b, t: (b, t, 0))

    def kv_spec(g):
        keep = min(WINDOWS[g], S)
        rb = min(tq, keep)
        first = nt - keep // rb
        return pl.BlockSpec((None, rb * KV_ROWS, HEAD_DIM), lambda b, t: (b, jnp.maximum(t - first, 0), 0))

    cls = lambda g, rows, dt: pltpu.VMEM((DILATIONS[g], rows, D_GRP), dt)
    scratch = [pltpu.VMEM((D_QKV // HEAD_DIM, tq, HEAD_DIM), F32)] * 3
    scratch += [cls(g, Qc[g], BF16) for g in range(N_GROUPS)]
    scratch += [cls(0, 2 * tq, BF16)] * 2 + [cls(1, KEY_WIN - QB[1] + S // DILATIONS[1], BF16)] * 2 + [cls(2, n2, BF16)] * 2
    scratch += [pltpu.VMEM((HEADS_PER_GROUP, tq, HEAD_DIM), F32)] * 6
    scratch += [pltpu.VMEM((HEADS_PER_GROUP, QB[0], KEY_WIN), F32), pltpu.VMEM((HEADS_PER_GROUP, QB[1], KEY_WIN), F32),
                pltpu.VMEM((HEADS_PER_GROUP, n2, n2), F32)]
    scratch += [pltpu.VMEM((HEADS_PER_GROUP, tq, KEY_WIN), F32), pltpu.VMEM((HEADS_PER_GROUP, tq, KEY_WIN), BF16)]
    scratch += [pltpu.VMEM((N_SHIFT_BUFS, SHIFT_CHUNK_ROWS, HEAD_DIM), F32),
                pltpu.SemaphoreType.DMA((N_SHIFT_BUFS,)), pltpu.SemaphoreType.DMA((N_SHIFT_BUFS,))]
    for c, f in zip(caches, fresh):
        rpb, rows_b, hop = c.shape[0] // (B * nt), c.shape[0] // n_cache_batch, f.shape[0] // n_cache_batch
        assert c.shape[0] % (B * nt) == 0 and rows_b % rpb == 0 and rpb % hop == 0 and rpb > hop and hop % 8 == 0
    hbm = pl.BlockSpec(memory_space=pl.ANY)
    return pl.pallas_call(
        functools.partial(_attn_body, B=B, S=S, tq=tq, n_cache_batch=n_cache_batch),
        grid=(B, nt),
        in_specs=[row, row, const((1, D_MODEL)), const((D_MODEL, 3 * D_QKV)), const((D_MODEL, D_GRP + 2 * D_MODEL)),
                  const((1, HEAD_DIM)), const((1, HEAD_DIM)), const((1, 2 * D_MODEL)),
                  pl.BlockSpec(memory_space=pltpu.SMEM), const((D_GRP, D_MODEL)), const((D_MODEL, D_MODEL))]
        + [hbm] * (2 * N_GROUPS),
        out_specs=[row, kv_spec(0), kv_spec(1), kv_spec(2)] + [hbm] * N_GROUPS,
        out_shape=[jax.ShapeDtypeStruct((B, S, D_MODEL), F32)]
        + [jax.ShapeDtypeStruct((B, min(WINDOWS[g], S) * KV_ROWS, HEAD_DIM), F32) for g in range(N_GROUPS)]
        + [jax.ShapeDtypeStruct(c.shape, c.dtype) for c in caches],
        scratch_shapes=scratch,
        compiler_params=pltpu.CompilerParams(dimension_semantics=("arbitrary", "arbitrary"), vmem_limit_bytes=VMEM_LIMIT),
        name="attn_branch",
    )(x, ya, gn, wqkv, wg, gq, gk, bmerge, relb, wap, wout, *caches, *fresh)


def _qkv_body(x_ref, gn_ref, wqkv_ref, gq_ref, gk_ref, zq_ref, zk_ref, zv_ref, nr0_ref, nr1_ref, nr2_ref):
    M = x_ref.shape[0]
    u = _rms(x_ref[...], gn_ref[...]).astype(BF16)
    z_refs = (zq_ref, zk_ref, zv_ref)
    nr_refs = (nr0_ref, nr1_ref, nr2_ref)

    def store(part, head, value):
        z_refs[part][:, head * HEAD_DIM:(head + 1) * HEAD_DIM] = value
        if part > 0:
            g, h = divmod(head, HEADS_PER_GROUP)
            nr_refs[g][pl.ds((part - 1) * HEADS_PER_GROUP + h, M, stride=KV_ROWS), :] = value

    _qkv(u, wqkv_ref, gq_ref, gk_ref, store)


def _qkv_call(x, gn, wqkv, gq, gk):
    M = x.shape[0]
    out = jax.ShapeDtypeStruct((M, D_QKV), F32)
    rows = jax.ShapeDtypeStruct((M * KV_ROWS, HEAD_DIM), F32)
    return pl.pallas_call(_qkv_body, out_shape=[out, out, out, rows, rows, rows],
                          compiler_params=pltpu.CompilerParams(vmem_limit_bytes=VMEM_LIMIT),
                          name="decode_qkv")(x, gn, wqkv, gq, gk)


def _cache_attn_body(q_ref, k_ref, v_ref, c0_ref, c1_ref, c2_ref, relb_ref, o_ref, lse_ref,
                     kc0, vc0, kc1, vc1, kc2, vc2, bt0, bt1, bt2, *, T):
    cache_refs, kcs, vcs, bts = (c0_ref, c1_ref, c2_ref), (kc0, kc1, kc2), (vc0, vc1, vc2), (bt0, bt1, bt2)
    Ls = tuple(c.shape[0] // KV_ROWS for c in cache_refs)
    Ws = tuple(L + HEAD_DIM for L in Ls)

    @pl.when(pl.program_id(0) == 0)
    def _():
        for g in range(N_GROUPS):
            L, W, d = Ls[g], Ws[g], DILATIONS[g]
            kcs[g][L:W, :] = jnp.zeros((W - L, D_GRP), BF16)
            vcs[g][L:W, :] = jnp.zeros((W - L, D_GRP), BF16)
            tq = lax.broadcasted_iota(jnp.int32, (T, W), 0)
            p = lax.broadcasted_iota(jnp.int32, (T, W), 1)
            dist = L + tq - p
            valid = (dist >= 0) & ((dist & (d - 1)) == 0) & (dist <= N_BACK * d) & (p < L + T)
            for h in range(HEADS_PER_GROUP):
                bts[g][h * T:(h + 1) * T, :] = _bias_table(relb_ref, g * HEADS_PER_GROUP + h, dist, valid)

    for g in range(N_GROUPS):
        L = Ls[g]
        gs = slice(g * D_GRP, (g + 1) * D_GRP)
        for h in range(HEADS_PER_GROUP):
            hs_ = slice(h * HEAD_DIM, (h + 1) * HEAD_DIM)
            kcs[g][0:L, hs_] = cache_refs[g][pl.ds(h, L, stride=KV_ROWS), :].astype(BF16)
            vcs[g][0:L, hs_] = cache_refs[g][pl.ds(HEADS_PER_GROUP + h, L, stride=KV_ROWS), :].astype(BF16)
        k_new, v_new = k_ref[:, gs], v_ref[:, gs]
        kcs[g][L:L + 2 * T, :] = jnp.concatenate([k_new, jnp.zeros_like(k_new)], axis=0).astype(BF16)
        vcs[g][L:L + 2 * T, :] = jnp.concatenate([v_new, jnp.zeros_like(v_new)], axis=0).astype(BF16)
    for g in range(N_GROUPS):
        heads = [slice(g * D_GRP + h * HEAD_DIM, g * D_GRP + (h + 1) * HEAD_DIM) for h in range(HEADS_PER_GROUP)]
        s = jnp.concatenate([_dot_nt(q_ref[:, hs_].astype(BF16), kcs[g][:, h * HEAD_DIM:(h + 1) * HEAD_DIM])
                             for h, hs_ in enumerate(heads)], axis=0)
        s = s * ATTN_SCALE + bts[g][...]
        m = jnp.max(s, axis=-1, keepdims=True)
        p = jnp.exp(s - m)
        den = jnp.sum(p, axis=-1, keepdims=True)
        p = p * (1.0 / den)
        lse = m + jnp.log(den)
        for h, hs_ in enumerate(heads):
            rows = slice(h * T, (h + 1) * T)
            o_ref[:, hs_] = _dot(p[rows].astype(BF16), vcs[g][:, h * HEAD_DIM:(h + 1) * HEAD_DIM])
            lse_ref[:, hs_] = jnp.broadcast_to(lse[rows], (T, HEAD_DIM))


def _cache_attn_call(q, k, v, caches, relb, *, T):
    DB = caches[0].shape[0]
    Ls = tuple(c.shape[1] // KV_ROWS for c in caches)
    assert all(L == N_BACK * d for L, d in zip(Ls, DILATIONS)) and T % 8 == 0
    row = pl.BlockSpec((T, D_QKV), lambda b: (b, 0))
    out = jax.ShapeDtypeStruct((DB * T, D_QKV), F32)
    scratch = []
    for L in Ls:
        scratch += [pltpu.VMEM((L + HEAD_DIM, D_GRP), BF16)] * 2
    scratch += [pltpu.VMEM((HEADS_PER_GROUP * T, L + HEAD_DIM), F32) for L in Ls]
    return pl.pallas_call(
        functools.partial(_cache_attn_body, T=T),
        grid=(DB,),
        in_specs=[row, row, row] + [pl.BlockSpec((None, c.shape[1], HEAD_DIM), lambda b: (b, 0, 0)) for c in caches]
        + [pl.BlockSpec(memory_space=pltpu.SMEM)],
        out_specs=[row, row],
        out_shape=[out, out],
        scratch_shapes=scratch,
        compiler_params=pltpu.CompilerParams(dimension_semantics=("arbitrary",), vmem_limit_bytes=VMEM_LIMIT),
        name="cache_attn",
    )(q, k, v, *caches, relb)


def _tail_body(x_ref, ya_ref, gn_ref, wg_ref, bm_ref, wap_ref, wout_ref, o_ref, lse_ref, y_ref):
    x = x_ref[...]
    u = _rms(x, gn_ref[...]).astype(BF16)
    groups = [slice(g * D_GRP, (g + 1) * D_GRP) for g in range(N_GROUPS)]
    ob = _merge_groups([o_ref.at[:, gs] for gs in groups], [lse_ref.at[:, gs] for gs in groups])
    gate = jnp.concatenate([_gate_chunk(u, wg_ref, bm_ref, c) for c in range(N_GATE_CHUNKS)], axis=1)
    y_ref[...] = _tail(x, ya_ref[...], ob, gate, wap_ref, wout_ref)


def _tail_call(x, ya, gn, wg, bmerge, wap, wout, o, lse):
    return pl.pallas_call(_tail_body, out_shape=jax.ShapeDtypeStruct(x.shape, F32),
                          compiler_params=pltpu.CompilerParams(vmem_limit_bytes=VMEM_LIMIT),
                          name="decode_tail")(x, ya, gn, wg, bmerge, wap, wout, o, lse)


def kernel(x_prompt, x_sample, cache_kv_w128, cache_kv_w512, cache_kv_w2048, state_conv, state_h, g_norm, w_in,
           b_merge, conv_w, conv_b, lru_w_a, lru_b_a, lru_w_x, lru_b_x, lru_lambda, g_q, g_k, rel_bias,
           w_lru_proj, w_attn_proj, w_out):
    assert w_in.shape[0] == 1, "single-layer step"
    B, S, _ = x_prompt.shape
    DB, T, _ = x_sample.shape
    o2 = 2 * D_LRU
    o5 = o2 + 3 * D_QKV
    w_lru = w_in[0, :, :o2].astype(BF16)
    wqkv = w_in[0, :, o2:o5].astype(BF16)
    wg = w_in[0, :, o5:].astype(BF16)
    wa, wx = lru_w_a[0].astype(BF16), lru_w_x[0].astype(BF16)
    wp, wap, wout = w_lru_proj[0].astype(BF16), w_attn_proj[0].astype(BF16), w_out[0].astype(BF16)
    lru_params = (g_norm, w_lru, conv_w[0], conv_b, wa, lru_b_a, wx, lru_b_x, lru_lambda, wp)
    hist = CONV_WIDTH - 1

    caches6 = (cache_kv_w128, cache_kv_w512, cache_kv_w2048)
    caches = [c.reshape(DB, c.shape[2] * KV_ROWS, HEAD_DIM) for c in caches6]
    flat = lambda a: a.reshape(-1, HEAD_DIM)

    xs2 = x_sample.reshape(DB * T, D_MODEL)
    zq, zk, zv, *newrows = _qkv_call(xs2, g_norm, wqkv, g_q, g_k)

    ya_p, conv_p, h_p = _lru_call(x_prompt, jnp.zeros((hist * B, D_LRU), F32), jnp.zeros((B, D_LRU), F32),
                                  *lru_params, tt=32)
    y_p, kv0_p, kv1_p, kv2_p, *news = _attn_call(x_prompt, ya_p, g_norm, wqkv, wg, g_q, g_k, b_merge, rel_bias,
                                                 wap, wout, [flat(c) for c in caches], newrows, tq=256,
                                                 n_cache_batch=DB)

    conv0_s = jnp.swapaxes(state_conv[0], 0, 1).reshape(hist * DB, D_LRU)
    ya_s, conv_s, h_s = _lru_call(x_sample, conv0_s, state_h[0], *lru_params, tt=T)
    o_s, lse_s = _cache_attn_call(zq, zk, zv, caches, rel_bias, T=T)
    y_s = _tail_call(xs2, ya_s.reshape(DB * T, D_MODEL), g_norm, wg, b_merge, wap, wout, o_s, lse_s)

    kv_shape = lambda a: a.reshape(1, a.shape[0], a.shape[1] // KV_ROWS, 2, HEADS_PER_GROUP, HEAD_DIM)
    conv_out = lambda c, nb: jnp.swapaxes(c.reshape(hist, nb, D_LRU), 0, 1)[None]
    news = [n.reshape(c.shape) for n, c in zip(news, caches6)]
    return (y_p, y_s.reshape(DB, T, D_MODEL), kv_shape(kv0_p), kv_shape(kv1_p), kv_shape(kv2_p),
            conv_out(conv_p, B), h_p[None], news[0], news[1], news[2], conv_out(conv_s, DB), h_s[None])
```

```python
import functools
import math

import jax
import jax.numpy as jnp
from jax import lax
from jax.experimental import pallas as pl
from jax.experimental.pallas import tpu as pltpu

F32 = jnp.float32
BF16 = jnp.bfloat16

D_MODEL = 1024
D_LRU = 1024
N_LRU_BLOCKS = 8
LRU_BLOCK = D_LRU // N_LRU_BLOCKS
CONV_WIDTH = 4
LRU_C = 8.0
HEAD_DIM = 128
HEADS_PER_GROUP = 4
WINDOWS = (128, 512, 2048)
DILATIONS = (1, 4, 16)
N_GROUPS = 3
N_BACK = 128
D_QKV = N_GROUPS * HEADS_PER_GROUP * HEAD_DIM
D_GRP = HEADS_PER_GROUP * HEAD_DIM
ATTN_SCALE = HEAD_DIM ** -0.5
N_BUCKETS = 32
MAX_DISTANCE = 2048
NORM_EPS = 1e-6
NEG = -1e30
KEY_WIN = 256
KV_ROWS = 2 * HEADS_PER_GROUP
VMEM_LIMIT = 60000 * 1024


def _rms(x, g):
    ms = jnp.mean(x * x, axis=-1, keepdims=True)
    return x * lax.rsqrt(ms + NORM_EPS) * g


def _sigmoid(x):
    return 0.5 * jnp.tanh(0.5 * x) + 0.5


def _dot(a, b):
    return jnp.dot(a, b, preferred_element_type=F32)


def _dot_nt(a, b):
    return lax.dot_general(a, b, (((1,), (1,)), ((), ())), preferred_element_type=F32)


def _bias_table(relb_ref, col, dist, valid):
    max_exact = N_BUCKETS // 2
    n_log = N_BUCKETS - max_exact
    df = jnp.maximum(dist, 1).astype(F32)
    val = jnp.log(df / max_exact) / math.log(MAX_DISTANCE / max_exact) * n_log
    out = jnp.zeros(dist.shape, F32)
    for b in range(max_exact):
        out = jnp.where(dist == b, relb_ref[b, col], out)
    for k in range(n_log):
        out = jnp.where((dist >= max_exact) & (val >= k), relb_ref[max_exact + k, col], out)
    return jnp.where(valid, out, NEG)


def _softmax_pv(s, v):
    m = jnp.max(s, axis=-1, keepdims=True)
    p = jnp.exp(s - m)
    den = jnp.sum(p, axis=-1, keepdims=True)
    o = _dot(p.astype(BF16), v) / den
    return o, m + jnp.log(den)


def _merge_groups(o_refs, lse_refs):
    l0, l1, l2 = (r[...] for r in lse_refs)
    m = jnp.maximum(jnp.maximum(l0, l1), l2)
    e0, e1, e2 = jnp.exp(l0 - m), jnp.exp(l1 - m), jnp.exp(l2 - m)
    num = e0 * o_refs[0][...] + e1 * o_refs[1][...] + e2 * o_refs[2][...]
    return num / (e0 + e1 + e2)


N_GATE_CHUNKS = (D_GRP + 2 * D_MODEL) // D_GRP


def _gate_chunk(u, wg_ref, bm_ref, c):
    z = _dot(u, wg_ref[:, c * D_GRP:(c + 1) * D_GRP])
    if c == 0:
        return z * _sigmoid(z)
    return _sigmoid(z + bm_ref[:, (c - 1) * D_GRP:c * D_GRP])


def _tail(x, ya, ob, gate, wap_ref, wout_ref):
    yb = _dot((ob * gate[:, :D_GRP]).astype(BF16), wap_ref[...])
    merged = gate[:, D_GRP:D_GRP + D_MODEL] * ya + gate[:, D_GRP + D_MODEL:] * yb
    return x + _dot(merged.astype(BF16), wout_ref[...])


def _qkv(u, wqkv_ref, gq_ref, gk_ref, store):
    for part, g_ref in enumerate((gq_ref, gk_ref, None)):
        z = _dot(u, wqkv_ref[:, part * D_QKV:(part + 1) * D_QKV])
        zs = [z[:, h * HEAD_DIM:(h + 1) * HEAD_DIM] for h in range(D_QKV // HEAD_DIM)]
        if g_ref is not None:
            ms = [jnp.mean(zh * zh, axis=-1, keepdims=True) for zh in zs]
            zs = [zh * lax.rsqrt(m + NORM_EPS) * g_ref[...] for zh, m in zip(zs, ms)]
        for h, zh in enumerate(zs):
            store(part, h, zh)


def _lru_body(x_ref, gn_ref, w_ref, cw_ref, cb_ref, wa_ref, ba_ref, wx_ref, bx_ref, lam_ref, wp_ref,
              conv0_ref, h0_ref, ya_ref, convo_ref, ho_ref, xs_ref, ga_ref, a_ref, hs_ref, h_ref, *, B, tt):
    R = B * tt
    hist = (CONV_WIDTH - 1) * B

    @pl.when(pl.program_id(0) == 0)
    def _():
        xs_ref[0:hist, :] = conv0_ref[...]
        h_ref[...] = h0_ref[...]

    xt =jnp.concatenate([x_ref[:, t, :] for t in range(tt)], axis=0)
    u = _rms(xt, gn_ref[...]).astype(BF16)
    xs_ref[hist:hist + R, :] = _dot(u, w_ref[:, :D_LRU])
    ga_ref[...] = _dot(u, w_ref[:, D_LRU:])
    sp = jax.nn.softplus(-lam_ref[...])
    for n in range(N_LRU_BLOCKS):
        cs = slice(n * LRU_BLOCK, (n + 1) * LRU_BLOCK)
        y = cb_ref[:, cs] + sum(xs_ref[j * B:j * B + R, cs] * cw_ref[j:j + 1, cs] for j in range(CONV_WIDTH))
        yb = y.astype(BF16)
        r = _sigmoid(_dot(yb, wa_ref[n]) + ba_ref[:, cs])
        i = _sigmoid(_dot(yb, wx_ref[n]) + bx_ref[:, cs])
        log_a = -LRU_C * r * sp[:, cs]
        a = jnp.exp(log_a)
        th = jnp.tanh(log_a)
        a_ref[:, cs] = a
        hs_ref[:, cs] = jnp.sqrt(-2.0 * th) * lax.rsqrt(1.0 - th) * (i * y)
    h = h_ref[...]
    for t in range(tt):
        rows = slice(t * B, (t + 1) * B)
        h = a_ref[rows, :] * h + hs_ref[rows, :]
        hs_ref[rows, :] = h
    h_ref[...] = h
    g = ga_ref[...]
    ya = _dot((hs_ref[...] * (g * _sigmoid(g))).astype(BF16), wp_ref[...])
    for t in range(tt):
        ya_ref[:, t, :] = ya[t * B:(t + 1) * B].astype(ya_ref.dtype)
    tail = xs_ref[R:R + hist, :]
    convo_ref[...] = tail
    xs_ref[0:hist, :] = tail
    ho_ref[...] = h_ref[...]


def _lru_call(x, conv0, h0, gn, w_lru, cw, cb, wa, ba, wx, bx, lam, wp, *, tt):
    B, S, _ = x.shape
    assert S % tt == 0 and tt >= CONV_WIDTH - 1 and tt % 8 == 0 and B % 8 == 0
    R = B * tt
    hist = (CONV_WIDTH - 1) * B
    const = lambda shape: pl.BlockSpec(shape, lambda i: (0,) * len(shape), pipeline_mode=pl.Buffered(1))
    return pl.pallas_call(
        functools.partial(_lru_body, B=B, tt=tt),
        grid=(S // tt,),
        in_specs=[
            pl.BlockSpec((B, tt, D_MODEL), lambda i: (0, i, 0)),
            const((1, D_MODEL)), const((D_MODEL, 2 * D_LRU)), const((CONV_WIDTH, D_LRU)), const((1, D_LRU)),
            const((N_LRU_BLOCKS, LRU_BLOCK, LRU_BLOCK)), const((1, D_LRU)),
            const((N_LRU_BLOCKS, LRU_BLOCK, LRU_BLOCK)), const((1, D_LRU)), const((1, D_LRU)),
            const((D_LRU, D_MODEL)), const((hist, D_LRU)), const((B, D_LRU)),
        ],
        out_specs=[
            pl.BlockSpec((B, tt, D_MODEL), lambda i: (0, i, 0)),
            pl.BlockSpec((hist, D_LRU), lambda i: (0, 0)),
            pl.BlockSpec((B, D_LRU), lambda i: (0, 0)),
        ],
        out_shape=[
            jax.ShapeDtypeStruct((B, S, D_MODEL), F32),
            jax.ShapeDtypeStruct((hist, D_LRU), F32),
            jax.ShapeDtypeStruct((B, D_LRU), F32),
        ],
        scratch_shapes=[
            pltpu.VMEM((hist + R, D_LRU), F32),
            pltpu.VMEM((R, D_LRU), F32),
            pltpu.VMEM((R, D_LRU), F32),
            pltpu.VMEM((R, D_LRU), F32),
            pltpu.VMEM((B, D_LRU), F32),
        ],
        compiler_params=pltpu.CompilerParams(dimension_semantics=("arbitrary",), vmem_limit_bytes=VMEM_LIMIT),
        name="lru_branch",
    )(x, gn, w_lru, cw, cb, wa, ba, wx, bx, lam, wp, conv0, h0)


SHIFT_CHUNK_ROWS = 2048
N_SHIFT_BUFS = 2


class _CacheShift:
    def __init__(self, n_steps, n_batch, cache_refs, fresh_refs, out_refs, buf_ref, in_sem, out_sem):
        self.refs = list(zip(cache_refs, fresh_refs, out_refs))
        self.buf, self.in_sem, self.out_sem = buf_ref, in_sem, out_sem
        self.geom = []
        self.chunks = []
        for g, (c, f, _) in enumerate(self.refs):
            share, rows_b, hop = c.shape[0] // n_steps, c.shape[0] // n_batch, f.shape[0] // n_batch
            self.geom.append((share, rows_b, hop))
            n = min(share, SHIFT_CHUNK_ROWS)
            self.chunks += [(g, off, n) for off in range(0, share, n)]

    def _chunk(self, step, k):
        g, off, n = self.chunks[k]
        share, rows_b, hop = self.geom[g]
        r = pl.multiple_of(step * share + off, hop)
        slot = k % N_SHIFT_BUFS
        return g, n, r, rows_b, hop, slot

    def fetch(self, step, k, do):
        g, n, r, rows_b, hop, slot = self._chunk(step, k)
        c, f, _ = self.refs[g]
        ends_batch = (r + n) % rows_b == 0
        spare = SHIFT_CHUNK_ROWS
        behind = pl.multiple_of(jnp.minimum(r + n, c.shape[0] - hop), hop)
        fresh = pl.multiple_of(r // rows_b * hop, hop)
        to_behind = pl.multiple_of(jnp.where(ends_batch, spare, n - hop), hop)
        to_fresh = pl.multiple_of(jnp.where(ends_batch, n - hop, spare), hop)
        sem = self.in_sem.at[slot]
        do(pltpu.make_async_copy(c.at[pl.ds(r + hop, n - hop), :], self.buf.at[slot, pl.ds(0, n - hop), :], sem))
        do(pltpu.make_async_copy(c.at[pl.ds(behind, hop), :], self.buf.at[slot, pl.ds(to_behind, hop), :], sem))
        do(pltpu.make_async_copy(f.at[pl.ds(fresh, hop), :], self.buf.at[slot, pl.ds(to_fresh, hop), :], sem))

    def drain(self, step, k, do):
        g, n, r, _, _, slot = self._chunk(step, k)
        do(pltpu.make_async_copy(self.buf.at[slot, pl.ds(0, n), :], self.refs[g][2].at[pl.ds(r, n), :],
                                 self.out_sem.at[slot]))

    def point(self, step, k):
        start, wait = (lambda cp: cp.start()), (lambda cp: cp.wait())
        nk = len(self.chunks)
        if 1 <= k <= nk:
            self.fetch(step, k - 1, wait)
            self.drain(step, k - 1, start)
        if 2 <= k <= nk + 1:
            self.drain(step, k - 2, wait)
        if k < nk:
            self.fetch(step, k, start)


def _attn_body(x_ref, ya_ref, gn_ref, wqkv_ref, wg_ref, gq_ref, gk_ref, bm_ref, relb_ref, wap_ref, wout_ref,
               c0_ref, c1_ref, c2_ref, f0_ref, f1_ref, f2_ref,
               y_ref, kv0_ref, kv1_ref, kv2_ref, n0_ref, n1_ref, n2_ref,
               zq_ref, zk_ref, zv_ref, qc0, qc1, qc2, hk0, hv0, hk1, hv1, hk2, hv2,
               o0, o1, o2, l0, l1, l2, bt0, bt1, bt2, s_ref, p_ref, bounce_ref, in_sem, out_sem,
               *, B, S, tq, n_cache_batch):
    t = pl.program_id(1)
    nt = S // tq

    step = pl.program_id(0) * nt + t
    shift = _CacheShift(B * nt, n_cache_batch, (c0_ref, c1_ref, c2_ref), (f0_ref, f1_ref, f2_ref),
                        (n0_ref, n1_ref, n2_ref), bounce_ref, in_sem, out_sem)
    assert len(shift.chunks) == 3 + N_GROUPS
    shift.point(step, 0)

    qcs, hks, hvs = (qc0, qc1, qc2), (hk0, hk1, hk2), (hv0, hv1, hv2)
    o_refs, l_refs, bts, kvs = (o0, o1, o2), (l0, l1, l2), (bt0, bt1, bt2), (kv0_ref, kv1_ref, kv2_ref)
    Qc = tuple(tq // d for d in DILATIONS)
    QB = tuple(min(q, N_BACK) for q in Qc)
    pad1 = KEY_WIN - QB[1]
    n2 = S // DILATIONS[2]

    @pl.when((pl.program_id(0) == 0) & (t == 0))
    def _():
        for ref in (hk0, hv0, hk1, hv1, hk2, hv2):
            ref[...] = jnp.zeros(ref.shape, ref.dtype)
        for g in (0, 1):
            a = lax.broadcasted_iota(jnp.int32, (QB[g], KEY_WIN), 0)
            c = lax.broadcasted_iota(jnp.int32, (QB[g], KEY_WIN), 1)
            j = a + (KEY_WIN - QB[g]) - c
            for h in range(HEADS_PER_GROUP):
                bts[g][h] = _bias_table(relb_ref, g * HEADS_PER_GROUP + h, j * DILATIONS[g], (j >= 0) & (j <= N_BACK))
        i = lax.broadcasted_iota(jnp.int32, (n2, n2), 0)
        c = lax.broadcasted_iota(jnp.int32, (n2, n2), 1)
        for h in range(HEADS_PER_GROUP):
            bt2[h] = _bias_table(relb_ref, 2 * HEADS_PER_GROUP + h, (i - c) * DILATIONS[2],
                                 (i - c >= 0) & (i - c <= N_BACK))

    x = x_ref[...]
    u = _rms(x, gn_ref[...]).astype(BF16)
    z_refs = (zq_ref, zk_ref, zv_ref)

    def store(part, head, value):
        z_refs[part][head] = value

    _qkv(u, wqkv_ref, gq_ref, gk_ref, store)
    shift.point(step, 1)

    for g in range(N_GROUPS):
        rb = min(tq, WINDOWS[g], S)
        for part, z_ref in enumerate((zk_ref, zv_ref)):
            for h in range(HEADS_PER_GROUP):
                dst = pl.ds(part * HEADS_PER_GROUP + h, rb, stride=KV_ROWS)
                kvs[g][dst, :] = z_ref[g * HEADS_PER_GROUP + h, tq - rb:tq, :]

    for g in range(N_GROUPS):
        d = DILATIONS[g]
        for r in range(d):
            rows = pl.ds(r, Qc[g], stride=d) if d > 1 else slice(None)
            if g == 0:
                dst = pl.ds(tq, tq)
            elif g == 1:
                dst = pl.ds(pl.multiple_of(pad1 + t * Qc[1], Qc[1]), Qc[1])
            else:
                dst = pl.ds(pl.multiple_of(t * Qc[2], Qc[2]), Qc[2])
            for h in range(HEADS_PER_GROUP):
                hs_ = slice(h * HEAD_DIM, (h + 1) * HEAD_DIM)
                gh = g * HEADS_PER_GROUP + h
                qcs[g][r, :, hs_] = zq_ref[gh, rows, :].astype(BF16)
                hks[g][r, dst, hs_] = zk_ref[gh, rows, :].astype(BF16)
                hvs[g][r, dst, hs_] = zv_ref[gh, rows, :].astype(BF16)

    shift.point(step, 2)

    col = lax.broadcasted_iota(jnp.int32, (1, KEY_WIN), 1)
    rowi = lax.broadcasted_iota(jnp.int32, (tq, 1), 0)
    for g in range(N_GROUPS):
        d = DILATIONS[g]
        nq = Qc[g] // QB[g]
        kw = n2 if g == 2 else KEY_WIN
        units = [(r, qb) for r in range(d) for qb in range(nq)]

        def key_win(qb, g=g):
            if g == 0:
                return pl.ds(tq + qb * QB[0] + QB[0] - KEY_WIN, KEY_WIN)
            if g == 1:
                return pl.ds(pl.multiple_of(t * Qc[1], QB[1]), KEY_WIN)
            return slice(None)

        for r, qb in units:
            cm = slice(r * Qc[g] + qb * QB[g], r * Qc[g] + (qb + 1) * QB[g])
            for h in range(HEADS_PER_GROUP):
                hs_ = slice(h * HEAD_DIM, (h + 1) * HEAD_DIM)
                s_ref[h, cm, 0:kw] = _dot_nt(qcs[g][r, qb * QB[g]:(qb + 1) * QB[g], hs_], hks[g][r, key_win(qb), hs_])

        if g < 2:
            i0 = t * Qc[g] + (rowi & (Qc[g] - QB[g]))
            started = col >= KEY_WIN - QB[g] - i0
        for h in range(HEADS_PER_GROUP):
            bias = bt2[h, pl.ds(pl.multiple_of(t * QB[2], QB[2]), QB[2]), :] if g == 2 else bts[g][h]
            s = s_ref[h, :, 0:kw] * ATTN_SCALE
            s = (s.reshape(len(units), QB[g], kw) + bias[None]).reshape(tq, kw)
            if g < 2:
                s = jnp.where(started, s, NEG)
            m = jnp.max(s, axis=-1, keepdims=True)
            p = jnp.exp(s - m)
            den = jnp.sum(p, axis=-1, keepdims=True)
            p_ref[h, :, 0:kw] = (p * (1.0 / den)).astype(BF16)
            lse = jnp.broadcast_to(m + jnp.log(den), (tq, HEAD_DIM))
            for r in range(d):
                rows = pl.ds(r, Qc[g], stride=d) if d > 1 else slice(None)
                l_refs[g][h, rows, :] = lse[r * Qc[g]:(r + 1) * Qc[g]]

        for r, qb in units:
            cm = slice(r * Qc[g] + qb * QB[g], r * Qc[g] + (qb + 1) * QB[g])
            rows = pl.ds(r + d * qb * QB[g], QB[g], stride=d) if d > 1 else pl.ds(qb * QB[g], QB[g])
            for h in range(HEADS_PER_GROUP):
                hs_ = slice(h * HEAD_DIM, (h + 1) * HEAD_DIM)
                o_refs[g][h, rows, :] = _dot(p_ref[h, cm, 0:kw], hvs[g][r, key_win(qb), hs_])
        shift.point(step, 3 + g)

    hk0[0, 0:tq, :] = hk0[0, tq:2 * tq, :]
    hv0[0, 0:tq, :] = hv0[0, tq:2 * tq, :]

    ob = _merge_groups(o_refs, l_refs)
    ob = jnp.concatenate([ob[h] for h in range(HEADS_PER_GROUP)], axis=1)
    shift.point(step, 3 + N_GROUPS)
    gate = jnp.concatenate([_gate_chunk(u, wg_ref, bm_ref, c) for c in range(N_GATE_CHUNKS)], axis=1)
    y_ref[...] = _tail(x, ya_ref[...].astype(F32), ob, gate, wap_ref, wout_ref)
    shift.point(step, 3 + N_GROUPS + 1)


def _attn_call(x, ya, gn, wqkv, wg, gq, gk, bmerge, relb, wap, wout, caches, fresh, *, tq, n_cache_batch):
    B, S, _ = x.shape
    nt = S // tq
    assert S % tq == 0 and tq % (DILATIONS[2] * 16) == 0 and tq >= N_BACK
    assert S // DILATIONS[2] == N_BACK and tq // DILATIONS[1] <= N_BACK
    Qc = tuple(tq // d for d in DILATIONS)
    QB = tuple(min(q, N_BACK) for q in Qc)
    n2 = S // DILATIONS[2]
    const = lambda shape: pl.BlockSpec(shape, lambda b, t: (0,) * len(shape), pipeline_mode=pl.Buffered(1))
    row = pl.BlockSpec((None, tq, D_MODEL), lambda b, t: (b, t, 0))

    def kv_spec(g):
        keep = min(WINDOWS[g], S)
        rb = min(tq, keep)
        first = nt - keep // rb
        return pl.BlockSpec((None, rb * KV_ROWS, HEAD_DIM), lambda b, t: (b, jnp.maximum(t - first, 0), 0))

    cls = lambda g, rows, dt: pltpu.VMEM((DILATIONS[g], rows, D_GRP), dt)
    scratch = [pltpu.VMEM((D_QKV // HEAD_DIM, tq, HEAD_DIM), F32)] * 3
    scratch += [cls(g, Qc[g], BF16) for g in range(N_GROUPS)]
    scratch += [cls(0, 2 * tq, BF16)] * 2 + [cls(1, KEY_WIN - QB[1] + S // DILATIONS[1], BF16)] * 2 + [cls(2, n2, BF16)] * 2
    scratch += [pltpu.VMEM((HEADS_PER_GROUP, tq, HEAD_DIM), F32)] * 6
    scratch += [pltpu.VMEM((HEADS_PER_GROUP, QB[0], KEY_WIN), F32), pltpu.VMEM((HEADS_PER_GROUP, QB[1], KEY_WIN), F32),
                pltpu.VMEM((HEADS_PER_GROUP, n2, n2), F32)]
    scratch += [pltpu.VMEM((HEADS_PER_GROUP, tq, KEY_WIN), F32), pltpu.VMEM((HEADS_PER_GROUP, tq, KEY_WIN), BF16)]
    hops = {f.shape[0] // n_cache_batch for f in fresh}
    assert len(hops) == 1
    hop = hops.pop()
    for c in caches:
        rpb, rows_b = c.shape[0] // (B * nt), c.shape[0] // n_cache_batch
        assert c.shape[0] % (B * nt) == 0 and rows_b % rpb == 0 and rpb % hop == 0 and rpb > hop and hop % 8 == 0
    scratch += [pltpu.VMEM((N_SHIFT_BUFS, SHIFT_CHUNK_ROWS + hop, HEAD_DIM), F32),
                pltpu.SemaphoreType.DMA((N_SHIFT_BUFS,)), pltpu.SemaphoreType.DMA((N_SHIFT_BUFS,))]
    hbm = pl.BlockSpec(memory_space=pl.ANY)
    return pl.pallas_call(
        functools.partial(_attn_body, B=B, S=S, tq=tq, n_cache_batch=n_cache_batch),
        grid=(B, nt),
        in_specs=[row, row, const((1, D_MODEL)), const((D_MODEL, 3 * D_QKV)), const((D_MODEL, D_GRP + 2 * D_MODEL)),
                  const((1, HEAD_DIM)), const((1, HEAD_DIM)), const((1, 2 * D_MODEL)),
                  pl.BlockSpec(memory_space=pltpu.SMEM), const((D_GRP, D_MODEL)), const((D_MODEL, D_MODEL))]
        + [hbm] * (2 * N_GROUPS),
        out_specs=[row, kv_spec(0), kv_spec(1), kv_spec(2)] + [hbm] * N_GROUPS,
        out_shape=[jax.ShapeDtypeStruct((B, S, D_MODEL), F32)]
        + [jax.ShapeDtypeStruct((B, min(WINDOWS[g], S) * KV_ROWS, HEAD_DIM), F32) for g in range(N_GROUPS)]
        + [jax.ShapeDtypeStruct(c.shape, c.dtype) for c in caches],
        scratch_shapes=scratch,
        compiler_params=pltpu.CompilerParams(dimension_semantics=("arbitrary", "arbitrary"), vmem_limit_bytes=VMEM_LIMIT),
        name="attn_branch",
    )(x, ya, gn, wqkv, wg, gq, gk, bmerge, relb, wap, wout, *caches, *fresh)


def _qkv_body(x_ref, gn_ref, wqkv_ref, gq_ref, gk_ref, zq_ref, zk_ref, zv_ref, nr0_ref, nr1_ref, nr2_ref):
    M = x_ref.shape[0]
    u = _rms(x_ref[...], gn_ref[...]).astype(BF16)
    z_refs = (zq_ref, zk_ref, zv_ref)
    nr_refs = (nr0_ref, nr1_ref, nr2_ref)

    def store(part, head, value):
        z_refs[part][:, head * HEAD_DIM:(head + 1) * HEAD_DIM] = value
        if part > 0:
            g, h = divmod(head, HEADS_PER_GROUP)
            nr_refs[g][pl.ds((part - 1) * HEADS_PER_GROUP + h, M, stride=KV_ROWS), :] = value

    _qkv(u, wqkv_ref, gq_ref, gk_ref, store)


def _qkv_call(x, gn, wqkv, gq, gk):
    M = x.shape[0]
    out = jax.ShapeDtypeStruct((M, D_QKV), F32)
    rows = jax.ShapeDtypeStruct((M * KV_ROWS, HEAD_DIM), F32)
    return pl.pallas_call(_qkv_body, out_shape=[out, out, out, rows, rows, rows],
                          compiler_params=pltpu.CompilerParams(vmem_limit_bytes=VMEM_LIMIT),
                          name="decode_qkv")(x, gn, wqkv, gq, gk)


def _cache_attn_body(q_ref, k_ref, v_ref, c0_ref, c1_ref, c2_ref, relb_ref, o_ref, lse_ref,
                     kc0, vc0, kc1, vc1, kc2, vc2, bt0, bt1, bt2, *, T):
    cache_refs, kcs, vcs, bts = (c0_ref, c1_ref, c2_ref), (kc0, kc1, kc2), (vc0, vc1, vc2), (bt0, bt1, bt2)
    Ls = tuple(c.shape[0] // KV_ROWS for c in cache_refs)
    Ws = tuple(L + HEAD_DIM for L in Ls)

    @pl.when(pl.program_id(0) == 0)
    def _():
        for g in range(N_GROUPS):
            L, W, d = Ls[g], Ws[g], DILATIONS[g]
            kcs[g][L:W, :] = jnp.zeros((W - L, D_GRP), BF16)
            vcs[g][L:W, :] = jnp.zeros((W - L, D_GRP), BF16)
            tq = lax.broadcasted_iota(jnp.int32, (T, W), 0)
            p = lax.broadcasted_iota(jnp.int32, (T, W), 1)
            dist = L + tq - p
            valid = (dist >= 0) & ((dist & (d - 1)) == 0) & (dist <= N_BACK * d) & (p < L + T)
            for h in range(HEADS_PER_GROUP):
                bts[g][h * T:(h + 1) * T, :] = _bias_table(relb_ref, g * HEADS_PER_GROUP + h, dist, valid)

    for g in range(N_GROUPS):
        L = Ls[g]
        gs = slice(g * D_GRP, (g + 1) * D_GRP)
        for h in range(HEADS_PER_GROUP):
            hs_ = slice(h * HEAD_DIM, (h + 1) * HEAD_DIM)
            kcs[g][0:L, hs_] = cache_refs[g][pl.ds(h, L, stride=KV_ROWS), :].astype(BF16)
            vcs[g][0:L, hs_] = cache_refs[g][pl.ds(HEADS_PER_GROUP + h, L, stride=KV_ROWS), :].astype(BF16)
        k_new, v_new = k_ref[:, gs], v_ref[:, gs]
        kcs[g][L:L + 2 * T, :] = jnp.concatenate([k_new, jnp.zeros_like(k_new)], axis=0).astype(BF16)
        vcs[g][L:L + 2 * T, :] = jnp.concatenate([v_new, jnp.zeros_like(v_new)], axis=0).astype(BF16)
    for g in range(N_GROUPS):
        heads = [slice(g * D_GRP + h * HEAD_DIM, g * D_GRP + (h + 1) * HEAD_DIM) for h in range(HEADS_PER_GROUP)]
        s = jnp.concatenate([_dot_nt(q_ref[:, hs_].astype(BF16), kcs[g][:, h * HEAD_DIM:(h + 1) * HEAD_DIM])
                             for h, hs_ in enumerate(heads)], axis=0)
        s = s * ATTN_SCALE + bts[g][...]
        m = jnp.max(s, axis=-1, keepdims=True)
        p = jnp.exp(s - m)
        den = jnp.sum(p, axis=-1, keepdims=True)
        p = p * (1.0 / den)
        lse = m + jnp.log(den)
        for h, hs_ in enumerate(heads):
            rows = slice(h * T, (h + 1) * T)
            o_ref[:, hs_] = _dot(p[rows].astype(BF16), vcs[g][:, h * HEAD_DIM:(h + 1) * HEAD_DIM])
            lse_ref[:, hs_] = jnp.broadcast_to(lse[rows], (T, HEAD_DIM))


def _cache_attn_call(q, k, v, caches, relb, *, T):
    DB = caches[0].shape[0]
    Ls = tuple(c.shape[1] // KV_ROWS for c in caches)
    assert all(L == N_BACK * d for L, d in zip(Ls, DILATIONS)) and T % 8 == 0
    row = pl.BlockSpec((T, D_QKV), lambda b: (b, 0))
    out = jax.ShapeDtypeStruct((DB * T, D_QKV), F32)
    scratch = []
    for L in Ls:
        scratch += [pltpu.VMEM((L + HEAD_DIM, D_GRP), BF16)] * 2
    scratch += [pltpu.VMEM((HEADS_PER_GROUP * T, L + HEAD_DIM), F32) for L in Ls]
    return pl.pallas_call(
        functools.partial(_cache_attn_body, T=T),
        grid=(DB,),
        in_specs=[row, row, row] + [pl.BlockSpec((None, c.shape[1], HEAD_DIM), lambda b: (b, 0, 0)) for c in caches]
        + [pl.BlockSpec(memory_space=pltpu.SMEM)],
        out_specs=[row, row],
        out_shape=[out, out],
        scratch_shapes=scratch,
        compiler_params=pltpu.CompilerParams(dimension_semantics=("arbitrary",), vmem_limit_bytes=VMEM_LIMIT),
        name="cache_attn",
    )(q, k, v, *caches, relb)


def _tail_body(x_ref, ya_ref, gn_ref, wg_ref, bm_ref, wap_ref, wout_ref, o_ref, lse_ref, y_ref):
    x = x_ref[...]
    u = _rms(x, gn_ref[...]).astype(BF16)
    groups = [slice(g * D_GRP, (g + 1) * D_GRP) for g in range(N_GROUPS)]
    ob = _merge_groups([o_ref.at[:, gs] for gs in groups], [lse_ref.at[:, gs] for gs in groups])
    gate = jnp.concatenate([_gate_chunk(u, wg_ref, bm_ref, c) for c in range(N_GATE_CHUNKS)], axis=1)
    y_ref[...] = _tail(x, ya_ref[...], ob, gate, wap_ref, wout_ref)


def _tail_call(x, ya, gn, wg, bmerge, wap, wout, o, lse):
    return pl.pallas_call(_tail_body, out_shape=jax.ShapeDtypeStruct(x.shape, F32),
                          compiler_params=pltpu.CompilerParams(vmem_limit_bytes=VMEM_LIMIT),
                          name="decode_tail")(x, ya, gn, wg, bmerge, wap, wout, o, lse)


def kernel(x_prompt, x_sample, cache_kv_w128, cache_kv_w512, cache_kv_w2048, state_conv, state_h, g_norm, w_in,
           b_merge, conv_w, conv_b, lru_w_a, lru_b_a, lru_w_x, lru_b_x, lru_lambda, g_q, g_k, rel_bias,
           w_lru_proj, w_attn_proj, w_out):
    assert w_in.shape[0] == 1, "single-layer step"
    B, S, _ = x_prompt.shape
    DB, T, _ = x_sample.shape
    o2 = 2 * D_LRU
    o5 = o2 + 3 * D_QKV
    w_lru = w_in[0, :, :o2].astype(BF16)
    wqkv = w_in[0, :, o2:o5].astype(BF16)
    wg = w_in[0, :, o5:].astype(BF16)
    wa, wx = lru_w_a[0].astype(BF16), lru_w_x[0].astype(BF16)
    wp, wap, wout = w_lru_proj[0].astype(BF16), w_attn_proj[0].astype(BF16), w_out[0].astype(BF16)
    lru_params = (g_norm, w_lru, conv_w[0], conv_b, wa, lru_b_a, wx, lru_b_x, lru_lambda, wp)
    hist = CONV_WIDTH - 1

    caches6 = (cache_kv_w128, cache_kv_w512, cache_kv_w2048)
    caches = [c.reshape(DB, c.shape[2] * KV_ROWS, HEAD_DIM) for c in caches6]
    flat = lambda a: a.reshape(-1, HEAD_DIM)

    xs2 = x_sample.reshape(DB * T, D_MODEL)
    zq, zk, zv, *newrows = _qkv_call(xs2, g_norm, wqkv, g_q, g_k)

    ya_p, conv_p, h_p = _lru_call(x_prompt, jnp.zeros((hist * B, D_LRU), F32), jnp.zeros((B, D_LRU), F32),
                                  *lru_params, tt=32)
    y_p, kv0_p, kv1_p, kv2_p, *news = _attn_call(x_prompt, ya_p, g_norm, wqkv, wg, g_q, g_k, b_merge, rel_bias,
                                                 wap, wout, [flat(c) for c in caches], newrows, tq=256,
                                                 n_cache_batch=DB)

    conv0_s = jnp.swapaxes(state_conv[0], 0, 1).reshape(hist * DB, D_LRU)
    ya_s, conv_s, h_s = _lru_call(x_sample, conv0_s, state_h[0], *lru_params, tt=T)
    o_s, lse_s = _cache_attn_call(zq, zk, zv, caches, rel_bias, T=T)
    y_s = _tail_call(xs2, ya_s.reshape(DB * T, D_MODEL), g_norm, wg, b_merge, wap, wout, o_s, lse_s)

    kv_shape = lambda a: a.reshape(1, a.shape[0], a.shape[1] // KV_ROWS, 2, HEADS_PER_GROUP, HEAD_DIM)
    conv_out = lambda c, nb: jnp.swapaxes(c.reshape(hist, nb, D_LRU), 0, 1)[None]
    news = [n.reshape(c.shape) for n, c in zip(news, caches6)]
    return (y_p, y_s.reshape(DB, T, D_MODEL), kv_shape(kv0_p), kv_shape(kv1_p), kv_shape(kv2_p),
            conv_out(conv_p, B), h_p[None], news[0], news[1], news[2], conv_out(conv_s, DB), h_s[None])
```

```python
import functools
import math

import jax
import jax.numpy as jnp
from jax import lax
from jax.experimental import pallas as pl
from jax.experimental.pallas import tpu as pltpu

F32 = jnp.float32
BF16 = jnp.bfloat16

D_MODEL = 1024
D_LRU = 1024
N_LRU_BLOCKS = 8
LRU_BLOCK = D_LRU // N_LRU_BLOCKS
CONV_WIDTH = 4
LRU_C = 8.0
HEAD_DIM = 128
HEADS_PER_GROUP = 4
WINDOWS = (128, 512, 2048)
DILATIONS = (1, 4, 16)
N_GROUPS = 3
N_BACK = 128
D_QKV = N_GROUPS * HEADS_PER_GROUP * HEAD_DIM
D_GRP = HEADS_PER_GROUP * HEAD_DIM
ATTN_SCALE = HEAD_DIM ** -0.5
N_BUCKETS = 32
MAX_DISTANCE = 2048
NORM_EPS = 1e-6
NEG = -1e30
KEY_WIN = 256
KV_ROWS = 2 * HEADS_PER_GROUP
VMEM_LIMIT = 60000 * 1024


def _rms(x, g):
    ms = jnp.mean(x * x, axis=-1, keepdims=True)
    return x * lax.rsqrt(ms + NORM_EPS) * g


def _sigmoid(x):
    return 0.5 * jnp.tanh(0.5 * x) + 0.5


def _dot(a, b):
    return jnp.dot(a, b, preferred_element_type=F32)


def _dot_nt(a, b):
    return lax.dot_general(a, b, (((1,), (1,)), ((), ())), preferred_element_type=F32)


def _bias_table(relb_ref, col, dist, valid):
    max_exact = N_BUCKETS // 2
    n_log = N_BUCKETS - max_exact
    df = jnp.maximum(dist, 1).astype(F32)
    val = jnp.log(df / max_exact) / math.log(MAX_DISTANCE / max_exact) * n_log
    out = jnp.zeros(dist.shape, F32)
    for b in range(max_exact):
        out = jnp.where(dist == b, relb_ref[b, col], out)
    for k in range(n_log):
        out = jnp.where((dist >= max_exact) & (val >= k), relb_ref[max_exact + k, col], out)
    return jnp.where(valid, out, NEG)


def _softmax_pv(s, v):
    m = jnp.max(s, axis=-1, keepdims=True)
    p = jnp.exp(s - m)
    den = jnp.sum(p, axis=-1, keepdims=True)
    o = _dot(p.astype(BF16), v) / den
    return o, m + jnp.log(den)


def _merge_groups(o_refs, lse_refs):
    l0, l1, l2 = (r[...] for r in lse_refs)
    m = jnp.maximum(jnp.maximum(l0, l1), l2)
    e0, e1, e2 = jnp.exp(l0 - m), jnp.exp(l1 - m), jnp.exp(l2 - m)
    num = e0 * o_refs[0][...] + e1 * o_refs[1][...] + e2 * o_refs[2][...]
    return num / (e0 + e1 + e2)


N_GATE_CHUNKS = (D_GRP + 2 * D_MODEL) // D_GRP


def _gate_chunk(u, wg_ref, bm_ref, c):
    z = _dot(u, wg_ref[:, c * D_GRP:(c + 1) * D_GRP])
    if c == 0:
        return z * _sigmoid(z)
    return _sigmoid(z + bm_ref[:, (c - 1) * D_GRP:c * D_GRP])


def _tail(x, ya, ob, gate, wap_ref, wout_ref):
    yb = _dot((ob * gate[:, :D_GRP]).astype(BF16), wap_ref[...])
    merged = gate[:, D_GRP:D_GRP + D_MODEL] * ya + gate[:, D_GRP + D_MODEL:] * yb
    return x + _dot(merged.astype(BF16), wout_ref[...])


def _qkv(u, wqkv_ref, gq_ref, gk_ref, store):
    for part, g_ref in enumerate((gq_ref, gk_ref, None)):
        z = _dot(u, wqkv_ref[:, part * D_QKV:(part + 1) * D_QKV])
        zs = [z[:, h * HEAD_DIM:(h + 1) * HEAD_DIM] for h in range(D_QKV // HEAD_DIM)]
        if g_ref is not None:
            ms = [jnp.mean(zh * zh, axis=-1, keepdims=True) for zh in zs]
            zs = [zh * lax.rsqrt(m + NORM_EPS) * g_ref[...] for zh, m in zip(zs, ms)]
        for h, zh in enumerate(zs):
            store(part, h, zh)


N_LRU_IN = 13


def _shift_cache_block(blk_ref, next_ref, fresh_ref, out_ref, batch_rows):
    rpb = blk_ref.shape[0]
    hop = fresh_ref.shape[0]
    out_ref[0:rpb - hop, :] = blk_ref[hop:rpb, :]
    ends_batch = ((pl.program_id(0) + 1) * rpb) % batch_rows == 0
    out_ref[rpb - hop:rpb, :] = jnp.where(ends_batch, fresh_ref[...], next_ref[...])


def _lru_body(*refs, B, tt, batch_rows):
    (x_ref, gn_ref, w_ref, cw_ref, cb_ref, wa_ref, ba_ref, wx_ref, bx_ref, lam_ref, wp_ref,
     conv0_ref, h0_ref) = refs[:N_LRU_IN]
    n_shift = len(batch_rows)
    shift_in = refs[N_LRU_IN:N_LRU_IN + 3 * n_shift]
    ya_ref, convo_ref, ho_ref = refs[N_LRU_IN + 3 * n_shift:N_LRU_IN + 3 * n_shift + 3]
    shift_out = refs[N_LRU_IN + 3 * n_shift + 3:N_LRU_IN + 4 * n_shift + 3]
    xs_ref, ga_ref, a_ref, hs_ref, h_ref = refs[N_LRU_IN + 4 * n_shift + 3:]
    R = B * tt
    hist = (CONV_WIDTH - 1) * B

    @pl.when(pl.program_id(0) == 0)
    def _():
        xs_ref[0:hist, :] = conv0_ref[...]
        h_ref[...] = h0_ref[...]

    for k in range(n_shift):
        _shift_cache_block(*shift_in[3 * k:3 * k + 3], shift_out[k], batch_rows[k])

    xt = jnp.concatenate([x_ref[:, t, :] for t in range(tt)], axis=0)
    u = _rms(xt, gn_ref[...]).astype(BF16)
    xs_ref[hist:hist + R, :] = _dot(u, w_ref[:, :D_LRU])
    ga_ref[...] = _dot(u, w_ref[:, D_LRU:])
    sp = jax.nn.softplus(-lam_ref[...])
    for n in range(N_LRU_BLOCKS):
        cs = slice(n * LRU_BLOCK, (n + 1) * LRU_BLOCK)
        y = cb_ref[:, cs] + sum(xs_ref[j * B:j * B + R, cs] * cw_ref[j:j + 1, cs] for j in range(CONV_WIDTH))
        yb = y.astype(BF16)
        r = _sigmoid(_dot(yb, wa_ref[n]) + ba_ref[:, cs])
        i = _sigmoid(_dot(yb, wx_ref[n]) + bx_ref[:, cs])
        log_a = -LRU_C * r * sp[:, cs]
        a = jnp.exp(log_a)
        th = jnp.tanh(log_a)
        a_ref[:, cs] = a
        hs_ref[:, cs] = jnp.sqrt(-2.0 * th) * lax.rsqrt(1.0 - th) * (i * y)
    h = h_ref[...]
    for t in range(tt):
        rows = slice(t * B, (t + 1) * B)
        h = a_ref[rows, :] * h + hs_ref[rows, :]
        hs_ref[rows, :] = h
    h_ref[...] = h
    g = ga_ref[...]
    ya = _dot((hs_ref[...] * (g * _sigmoid(g))).astype(BF16), wp_ref[...])
    for t in range(tt):
        ya_ref[:, t, :] = ya[t * B:(t + 1) * B].astype(ya_ref.dtype)
    tail = xs_ref[R:R + hist, :]
    convo_ref[...] = tail
    xs_ref[0:hist, :] = tail
    ho_ref[...] = h_ref[...]


def _lru_call(x, conv0, h0, gn, w_lru, cw, cb, wa, ba, wx, bx, lam, wp, *, tt, caches=(), fresh=(), n_cache_batch=1):
    B, S, _ = x.shape
    assert S % tt == 0 and tt >= CONV_WIDTH - 1 and tt % 8 == 0 and B % 8 == 0
    R = B * tt
    hist = (CONV_WIDTH - 1) * B
    steps = S // tt
    const = lambda shape: pl.BlockSpec(shape, lambda i: (0,) * len(shape), pipeline_mode=pl.Buffered(1))
    shift_in, shift_out, shift_shapes, batch_rows = [], [], [], []
    for c, f in zip(caches, fresh):
        total = c.shape[0]
        rpb = total // steps
        rows_b = total // n_cache_batch
        hop = f.shape[0] // n_cache_batch
        assert total % steps == 0 and rows_b % rpb == 0 and rpb % hop == 0 and rpb > hop
        shift_in += [
            pl.BlockSpec((rpb, HEAD_DIM), lambda i: (i, 0)),
            pl.BlockSpec((hop, HEAD_DIM), lambda i, rpb=rpb, hop=hop, total=total:
                         (jnp.minimum((i + 1) * (rpb // hop), total // hop - 1), 0)),
            pl.BlockSpec((hop, HEAD_DIM), lambda i, rpb=rpb, rows_b=rows_b: ((i * rpb) // rows_b, 0)),
        ]
        shift_out.append(pl.BlockSpec((rpb, HEAD_DIM), lambda i: (i, 0)))
        shift_shapes.append(jax.ShapeDtypeStruct(c.shape, c.dtype))
        batch_rows.append(rows_b)
    shift_args = [a for c, f in zip(caches, fresh) for a in (c, c, f)]
    return pl.pallas_call(
        functools.partial(_lru_body, B=B, tt=tt, batch_rows=tuple(batch_rows)),
        grid=(steps,),
        in_specs=[
            pl.BlockSpec((B, tt, D_MODEL), lambda i: (0, i, 0)),
            const((1, D_MODEL)), const((D_MODEL, 2 * D_LRU)), const((CONV_WIDTH, D_LRU)), const((1, D_LRU)),
            const((N_LRU_BLOCKS, LRU_BLOCK, LRU_BLOCK)), const((1, D_LRU)),
            const((N_LRU_BLOCKS, LRU_BLOCK, LRU_BLOCK)), const((1, D_LRU)), const((1, D_LRU)),
            const((D_LRU, D_MODEL)), const((hist, D_LRU)), const((B, D_LRU)),
        ] + shift_in,
        out_specs=[
            pl.BlockSpec((B, tt, D_MODEL), lambda i: (0, i, 0)),
            pl.BlockSpec((hist, D_LRU), lambda i: (0, 0)),
            pl.BlockSpec((B, D_LRU), lambda i: (0, 0)),
        ] + shift_out,
        out_shape=[
            jax.ShapeDtypeStruct((B, S, D_MODEL), F32),
            jax.ShapeDtypeStruct((hist, D_LRU), F32),
            jax.ShapeDtypeStruct((B, D_LRU), F32),
        ] + shift_shapes,
        scratch_shapes=[
            pltpu.VMEM((hist + R, D_LRU), F32),
            pltpu.VMEM((R, D_LRU), F32),
            pltpu.VMEM((R, D_LRU), F32),
            pltpu.VMEM((R, D_LRU), F32),
            pltpu.VMEM((B, D_LRU), F32),
        ],
        compiler_params=pltpu.CompilerParams(dimension_semantics=("arbitrary",), vmem_limit_bytes=VMEM_LIMIT),
        name="lru_branch",
    )(x, gn, w_lru, cw, cb, wa, ba, wx, bx, lam, wp, conv0, h0, *shift_args)


def _attn_body(x_ref, ya_ref, gn_ref, wqkv_ref, wg_ref, gq_ref, gk_ref, bm_ref, relb_ref, wap_ref, wout_ref,
               y_ref, kv0_ref, kv1_ref, kv2_ref,
               zq_ref, zk_ref, zv_ref, qc0, qc1, qc2, hk0, hv0, hk1, hv1, hk2, hv2,
               o0, o1, o2, l0, l1, l2, bt0, bt1, bt2, s_ref, p_ref, *, S, tq):
    t = pl.program_id(1)
    nt = S // tq

    qcs, hks, hvs = (qc0, qc1, qc2), (hk0, hk1, hk2), (hv0, hv1, hv2)
    o_refs, l_refs, bts, kvs = (o0, o1, o2), (l0, l1, l2), (bt0, bt1, bt2), (kv0_ref, kv1_ref, kv2_ref)
    Qc = tuple(tq // d for d in DILATIONS)
    QB = tuple(min(q, N_BACK) for q in Qc)
    pad1 = KEY_WIN - QB[1]
    n2 = S // DILATIONS[2]

    @pl.when((pl.program_id(0) == 0) & (t == 0))
    def _():
        for ref in (hk0, hv0, hk1, hv1, hk2, hv2):
            ref[...] = jnp.zeros(ref.shape, ref.dtype)
        for g in (0, 1):
            a = lax.broadcasted_iota(jnp.int32, (QB[g], KEY_WIN), 0)
            c = lax.broadcasted_iota(jnp.int32, (QB[g], KEY_WIN), 1)
            j = a + (KEY_WIN - QB[g]) - c
            for h in range(HEADS_PER_GROUP):
                bts[g][h] = _bias_table(relb_ref, g * HEADS_PER_GROUP + h, j * DILATIONS[g], (j >= 0) & (j <= N_BACK))
        i = lax.broadcasted_iota(jnp.int32, (n2, n2), 0)
        c = lax.broadcasted_iota(jnp.int32, (n2, n2), 1)
        for h in range(HEADS_PER_GROUP):
            bt2[h] = _bias_table(relb_ref, 2 * HEADS_PER_GROUP + h, (i - c) * DILATIONS[2],
                                 (i - c >= 0) & (i - c <= N_BACK))

    x = x_ref[...]
    u = _rms(x, gn_ref[...]).astype(BF16)
    z_refs = (zq_ref, zk_ref, zv_ref)

    def store(part, head, value):
        z_refs[part][head] = value

    _qkv(u, wqkv_ref, gq_ref, gk_ref, store)

    for g in range(N_GROUPS):
        rb = min(tq, WINDOWS[g], S)
        for part, z_ref in enumerate((zk_ref, zv_ref)):
            for h in range(HEADS_PER_GROUP):
                dst = pl.ds(part * HEADS_PER_GROUP + h, rb, stride=KV_ROWS)
                kvs[g][dst, :] = z_ref[g * HEADS_PER_GROUP + h, tq - rb:tq, :]

    for g in range(N_GROUPS):
        d = DILATIONS[g]
        for r in range(d):
            rows = pl.ds(r, Qc[g], stride=d) if d > 1 else slice(None)
            if g == 0:
                dst = pl.ds(tq, tq)
            elif g == 1:
                dst = pl.ds(pl.multiple_of(pad1 + t * Qc[1], Qc[1]), Qc[1])
            else:
                dst = pl.ds(pl.multiple_of(t * Qc[2], Qc[2]), Qc[2])
            for h in range(HEADS_PER_GROUP):
                hs_ = slice(h * HEAD_DIM, (h + 1) * HEAD_DIM)
                gh = g * HEADS_PER_GROUP + h
                qcs[g][r, :, hs_] = zq_ref[gh, rows, :].astype(BF16)
                hks[g][r, dst, hs_] = zk_ref[gh, rows, :].astype(BF16)
                hvs[g][r, dst, hs_] = zv_ref[gh, rows, :].astype(BF16)

    col = lax.broadcasted_iota(jnp.int32, (1, KEY_WIN), 1)
    rowi = lax.broadcasted_iota(jnp.int32, (tq, 1), 0)
    for g in range(N_GROUPS):
        d = DILATIONS[g]
        nq = Qc[g] // QB[g]
        kw = n2 if g == 2 else KEY_WIN
        units = [(r, qb) for r in range(d) for qb in range(nq)]

        def key_win(qb, g=g):
            if g == 0:
                return pl.ds(tq + qb * QB[0] + QB[0] - KEY_WIN, KEY_WIN)
            if g == 1:
                return pl.ds(pl.multiple_of(t * Qc[1], QB[1]), KEY_WIN)
            return slice(None)

        for r, qb in units:
            cm = slice(r * Qc[g] + qb * QB[g], r * Qc[g] + (qb + 1) * QB[g])
            for h in range(HEADS_PER_GROUP):
                hs_ = slice(h * HEAD_DIM, (h + 1) * HEAD_DIM)
                s_ref[h, cm, 0:kw] = _dot_nt(qcs[g][r, qb * QB[g]:(qb + 1) * QB[g], hs_], hks[g][r, key_win(qb), hs_])

        if g < 2:
            i0 = t * Qc[g] + (rowi & (Qc[g] - QB[g]))
            started = col >= KEY_WIN - QB[g] - i0
        for h in range(HEADS_PER_GROUP):
            bias = bt2[h, pl.ds(pl.multiple_of(t * QB[2], QB[2]), QB[2]), :] if g == 2 else bts[g][h]
            s = s_ref[h, :, 0:kw] * ATTN_SCALE
            s = (s.reshape(len(units), QB[g], kw) + bias[None]).reshape(tq, kw)
            if g < 2:
                s = jnp.where(started, s, NEG)
            m = jnp.max(s, axis=-1, keepdims=True)
            p = jnp.exp(s - m)
            den = jnp.sum(p, axis=-1, keepdims=True)
            p_ref[h, :, 0:kw] = (p * (1.0 / den)).astype(BF16)
            lse = jnp.broadcast_to(m + jnp.log(den), (tq, HEAD_DIM))
            for r in range(d):
                rows = pl.ds(r, Qc[g], stride=d) if d > 1 else slice(None)
                l_refs[g][h, rows, :] = lse[r * Qc[g]:(r + 1) * Qc[g]]

        for r, qb in units:
            cm = slice(r * Qc[g] + qb * QB[g], r * Qc[g] + (qb + 1) * QB[g])
            rows = pl.ds(r + d * qb * QB[g], QB[g], stride=d) if d > 1 else pl.ds(qb * QB[g], QB[g])
            for h in range(HEADS_PER_GROUP):
                hs_ = slice(h * HEAD_DIM, (h + 1) * HEAD_DIM)
                o_refs[g][h, rows, :] = _dot(p_ref[h, cm, 0:kw], hvs[g][r, key_win(qb), hs_])

    hk0[0, 0:tq, :] = hk0[0, tq:2 * tq, :]
    hv0[0, 0:tq, :] = hv0[0, tq:2 * tq, :]

    ob = _merge_groups(o_refs, l_refs)
    ob = jnp.concatenate([ob[h] for h in range(HEADS_PER_GROUP)], axis=1)
    gate = jnp.concatenate([_gate_chunk(u, wg_ref, bm_ref, c) for c in range(N_GATE_CHUNKS)], axis=1)
    y_ref[...] = _tail(x, ya_ref[...].astype(F32), ob, gate, wap_ref, wout_ref)


def _attn_call(x, ya, gn, wqkv, wg, gq, gk, bmerge, relb, wap, wout, *, tq):
    B, S, _ = x.shape
    nt = S // tq
    assert S % tq == 0 and tq % (DILATIONS[2] * 16) == 0 and tq >= N_BACK
    assert S // DILATIONS[2] == N_BACK and tq // DILATIONS[1] <= N_BACK
    Qc = tuple(tq // d for d in DILATIONS)
    QB = tuple(min(q, N_BACK) for q in Qc)
    n2 = S // DILATIONS[2]
    const = lambda shape: pl.BlockSpec(shape, lambda b, t: (0,) * len(shape), pipeline_mode=pl.Buffered(1))
    row = pl.BlockSpec((None, tq, D_MODEL), lambda b, t: (b, t, 0))

    def kv_spec(g):
        keep = min(WINDOWS[g], S)
        rb = min(tq, keep)
        first = nt - keep // rb
        return pl.BlockSpec((None, rb * KV_ROWS, HEAD_DIM), lambda b, t: (b, jnp.maximum(t - first, 0), 0))

    cls = lambda g, rows, dt: pltpu.VMEM((DILATIONS[g], rows, D_GRP), dt)
    scratch = [pltpu.VMEM((D_QKV // HEAD_DIM, tq, HEAD_DIM), F32)] * 3
    scratch += [cls(g, Qc[g], BF16) for g in range(N_GROUPS)]
    scratch += [cls(0, 2 * tq, BF16)] * 2 + [cls(1, KEY_WIN - QB[1] + S // DILATIONS[1], BF16)] * 2 + [cls(2, n2, BF16)] * 2
    scratch += [pltpu.VMEM((HEADS_PER_GROUP, tq, HEAD_DIM), F32)] * 6
    scratch += [pltpu.VMEM((HEADS_PER_GROUP, QB[0], KEY_WIN), F32), pltpu.VMEM((HEADS_PER_GROUP, QB[1], KEY_WIN), F32),
                pltpu.VMEM((HEADS_PER_GROUP, n2, n2), F32)]
    scratch += [pltpu.VMEM((HEADS_PER_GROUP, tq, KEY_WIN), F32), pltpu.VMEM((HEADS_PER_GROUP, tq, KEY_WIN), BF16)]
    return pl.pallas_call(
        functools.partial(_attn_body, S=S, tq=tq),
        grid=(B, nt),
        in_specs=[row, row, const((1, D_MODEL)), const((D_MODEL, 3 * D_QKV)), const((D_MODEL, D_GRP + 2 * D_MODEL)),
                  const((1, HEAD_DIM)), const((1, HEAD_DIM)), const((1, 2 * D_MODEL)),
                  pl.BlockSpec(memory_space=pltpu.SMEM), const((D_GRP, D_MODEL)), const((D_MODEL, D_MODEL))],
        out_specs=[row, kv_spec(0), kv_spec(1), kv_spec(2)],
        out_shape=[jax.ShapeDtypeStruct((B, S, D_MODEL), F32)]
        + [jax.ShapeDtypeStruct((B, min(WINDOWS[g], S) * KV_ROWS, HEAD_DIM), F32) for g in range(N_GROUPS)],
        scratch_shapes=scratch,
        compiler_params=pltpu.CompilerParams(dimension_semantics=("arbitrary", "arbitrary"), vmem_limit_bytes=VMEM_LIMIT),
        name="attn_branch",
    )(x, ya, gn, wqkv, wg, gq, gk, bmerge, relb, wap, wout)


def _qkv_body(x_ref, gn_ref, wqkv_ref, gq_ref, gk_ref, zq_ref, zk_ref, zv_ref, nr0_ref, nr1_ref, nr2_ref):
    M = x_ref.shape[0]
    u = _rms(x_ref[...], gn_ref[...]).astype(BF16)
    z_refs = (zq_ref, zk_ref, zv_ref)
    nr_refs = (nr0_ref, nr1_ref, nr2_ref)

    def store(part, head, value):
        z_refs[part][:, head * HEAD_DIM:(head + 1) * HEAD_DIM] = value
        if part > 0:
            g, h = divmod(head, HEADS_PER_GROUP)
            nr_refs[g][pl.ds((part - 1) * HEADS_PER_GROUP + h, M, stride=KV_ROWS), :] = value

    _qkv(u, wqkv_ref, gq_ref, gk_ref, store)


def _qkv_call(x, gn, wqkv, gq, gk):
    M = x.shape[0]
    out = jax.ShapeDtypeStruct((M, D_QKV), F32)
    rows = jax.ShapeDtypeStruct((M * KV_ROWS, HEAD_DIM), F32)
    return pl.pallas_call(_qkv_body, out_shape=[out, out, out, rows, rows, rows],
                          compiler_params=pltpu.CompilerParams(vmem_limit_bytes=VMEM_LIMIT),
                          name="decode_qkv")(x, gn, wqkv, gq, gk)


def _cache_attn_body(q_ref, k_ref, v_ref, c0_ref, c1_ref, c2_ref, relb_ref, o_ref, lse_ref,
                     kc0, vc0, kc1, vc1, kc2, vc2, bt0, bt1, bt2, *, T):
    cache_refs, kcs, vcs, bts = (c0_ref, c1_ref, c2_ref), (kc0, kc1, kc2), (vc0, vc1, vc2), (bt0, bt1, bt2)
    Ls = tuple(c.shape[0] // KV_ROWS for c in cache_refs)
    Ws = tuple(L + HEAD_DIM for L in Ls)

    @pl.when(pl.program_id(0) == 0)
    def _():
        for g in range(N_GROUPS):
            L, W, d = Ls[g], Ws[g], DILATIONS[g]
            kcs[g][L:W, :] = jnp.zeros((W - L, D_GRP), BF16)
            vcs[g][L:W, :] = jnp.zeros((W - L, D_GRP), BF16)
            tq = lax.broadcasted_iota(jnp.int32, (T, W), 0)
            p = lax.broadcasted_iota(jnp.int32, (T, W), 1)
            dist = L + tq - p
            valid = (dist >= 0) & ((dist & (d - 1)) == 0) & (dist <= N_BACK * d) & (p < L + T)
            for h in range(HEADS_PER_GROUP):
                bts[g][h * T:(h + 1) * T, :] = _bias_table(relb_ref, g * HEADS_PER_GROUP + h, dist, valid)

    for g in range(N_GROUPS):
        L = Ls[g]
        gs = slice(g * D_GRP, (g + 1) * D_GRP)
        for h in range(HEADS_PER_GROUP):
            hs_ = slice(h * HEAD_DIM, (h + 1) * HEAD_DIM)
            kcs[g][0:L, hs_] = cache_refs[g][pl.ds(h, L, stride=KV_ROWS), :].astype(BF16)
            vcs[g][0:L, hs_] = cache_refs[g][pl.ds(HEADS_PER_GROUP + h, L, stride=KV_ROWS), :].astype(BF16)
        k_new, v_new = k_ref[:, gs], v_ref[:, gs]
        kcs[g][L:L + 2 * T, :] = jnp.concatenate([k_new, jnp.zeros_like(k_new)], axis=0).astype(BF16)
        vcs[g][L:L + 2 * T, :] = jnp.concatenate([v_new, jnp.zeros_like(v_new)], axis=0).astype(BF16)
    for g in range(N_GROUPS):
        heads = [slice(g * D_GRP + h * HEAD_DIM, g * D_GRP + (h + 1) * HEAD_DIM) for h in range(HEADS_PER_GROUP)]
        s = jnp.concatenate([_dot_nt(q_ref[:, hs_].astype(BF16), kcs[g][:, h * HEAD_DIM:(h + 1) * HEAD_DIM])
                             for h, hs_ in enumerate(heads)], axis=0)
        s = s * ATTN_SCALE + bts[g][...]
        m = jnp.max(s, axis=-1, keepdims=True)
        p = jnp.exp(s - m)
        den = jnp.sum(p, axis=-1, keepdims=True)
        p = p * (1.0 / den)
        lse = m + jnp.log(den)
        for h, hs_ in enumerate(heads):
            rows = slice(h * T, (h + 1) * T)
            o_ref[:, hs_] = _dot(p[rows].astype(BF16), vcs[g][:, h * HEAD_DIM:(h + 1) * HEAD_DIM])
            lse_ref[:, hs_] = jnp.broadcast_to(lse[rows], (T, HEAD_DIM))


def _cache_attn_call(q, k, v, caches, relb, *, T):
    DB = caches[0].shape[0]
    Ls = tuple(c.shape[1] // KV_ROWS for c in caches)
    assert all(L == N_BACK * d for L, d in zip(Ls, DILATIONS)) and T % 8 == 0
    row = pl.BlockSpec((T, D_QKV), lambda b: (b, 0))
    out = jax.ShapeDtypeStruct((DB * T, D_QKV), F32)
    scratch = []
    for L in Ls:
        scratch += [pltpu.VMEM((L + HEAD_DIM, D_GRP), BF16)] * 2
    scratch += [pltpu.VMEM((HEADS_PER_GROUP * T, L + HEAD_DIM), F32) for L in Ls]
    return pl.pallas_call(
        functools.partial(_cache_attn_body, T=T),
        grid=(DB,),
        in_specs=[row, row, row] + [pl.BlockSpec((None, c.shape[1], HEAD_DIM), lambda b: (b, 0, 0)) for c in caches]
        + [pl.BlockSpec(memory_space=pltpu.SMEM)],
        out_specs=[row, row],
        out_shape=[out, out],
        scratch_shapes=scratch,
        compiler_params=pltpu.CompilerParams(dimension_semantics=("arbitrary",), vmem_limit_bytes=VMEM_LIMIT),
        name="cache_attn",
    )(q, k, v, *caches, relb)


def _tail_body(x_ref, ya_ref, gn_ref, wg_ref, bm_ref, wap_ref, wout_ref, o_ref, lse_ref, y_ref):
    x = x_ref[...]
    u = _rms(x, gn_ref[...]).astype(BF16)
    groups = [slice(g * D_GRP, (g + 1) * D_GRP) for g in range(N_GROUPS)]
    ob = _merge_groups([o_ref.at[:, gs] for gs in groups], [lse_ref.at[:, gs] for gs in groups])
    gate = jnp.concatenate([_gate_chunk(u, wg_ref, bm_ref, c) for c in range(N_GATE_CHUNKS)], axis=1)
    y_ref[...] = _tail(x, ya_ref[...], ob, gate, wap_ref, wout_ref)


def _tail_call(x, ya, gn, wg, bmerge, wap, wout, o, lse):
    return pl.pallas_call(_tail_body, out_shape=jax.ShapeDtypeStruct(x.shape, F32),
                          compiler_params=pltpu.CompilerParams(vmem_limit_bytes=VMEM_LIMIT),
                          name="decode_tail")(x, ya, gn, wg, bmerge, wap, wout, o, lse)


def kernel(x_prompt, x_sample, cache_kv_w128, cache_kv_w512, cache_kv_w2048, state_conv, state_h, g_norm, w_in,
           b_merge, conv_w, conv_b, lru_w_a, lru_b_a, lru_w_x, lru_b_x, lru_lambda, g_q, g_k, rel_bias,
           w_lru_proj, w_attn_proj, w_out):
    assert w_in.shape[0] == 1, "single-layer step"
    B, S, _ = x_prompt.shape
    DB, T, _ = x_sample.shape
    o2 = 2 * D_LRU
    o5 = o2 + 3 * D_QKV
    w_lru = w_in[0, :, :o2].astype(BF16)
    wqkv = w_in[0, :, o2:o5].astype(BF16)
    wg = w_in[0, :, o5:].astype(BF16)
    wa, wx = lru_w_a[0].astype(BF16), lru_w_x[0].astype(BF16)
    wp, wap, wout = w_lru_proj[0].astype(BF16), w_attn_proj[0].astype(BF16), w_out[0].astype(BF16)
    lru_params = (g_norm, w_lru, conv_w[0], conv_b, wa, lru_b_a, wx, lru_b_x, lru_lambda, wp)
    hist = CONV_WIDTH - 1

    caches6 = (cache_kv_w128, cache_kv_w512, cache_kv_w2048)
    caches = [c.reshape(DB, c.shape[2] * KV_ROWS, HEAD_DIM) for c in caches6]
    flat = lambda a: a.reshape(-1, HEAD_DIM)

    xs2 = x_sample.reshape(DB * T, D_MODEL)
    zq, zk, zv, *newrows = _qkv_call(xs2, g_norm, wqkv, g_q, g_k)

    ya_p, conv_p, h_p, *news = _lru_call(x_prompt, jnp.zeros((hist * B, D_LRU), F32), jnp.zeros((B, D_LRU), F32),
                                         *lru_params, tt=32, caches=[flat(c) for c in caches], fresh=newrows,
                                         n_cache_batch=DB)
    y_p, kv0_p, kv1_p, kv2_p = _attn_call(x_prompt, ya_p, g_norm, wqkv, wg, g_q, g_k, b_merge, rel_bias, wap, wout,
                                          tq=256)

    conv0_s = jnp.swapaxes(state_conv[0], 0, 1).reshape(hist * DB, D_LRU)
    ya_s, conv_s, h_s = _lru_call(x_sample, conv0_s, state_h[0], *lru_params, tt=T)
    o_s, lse_s = _cache_attn_call(zq, zk, zv, caches, rel_bias, T=T)
    y_s = _tail_call(xs2, ya_s.reshape(DB * T, D_MODEL), g_norm, wg, b_merge, wap, wout, o_s, lse_s)

    kv_shape = lambda a: a.reshape(1, a.shape[0], a.shape[1] // KV_ROWS, 2, HEADS_PER_GROUP, HEAD_DIM)
    conv_out = lambda c, nb: jnp.swapaxes(c.reshape(hist, nb, D_LRU), 0, 1)[None]
    news = [n.reshape(c.shape) for n, c in zip(news, caches6)]
    return (y_p, y_s.reshape(DB, T, D_MODEL), kv_shape(kv0_p), kv_shape(kv1_p), kv_shape(kv2_p),
            conv_out(conv_p, B), h_p[None], news[0], news[1], news[2], conv_out(conv_s, DB), h_s[None])
```

```python
import functools
import math

import jax
import jax.numpy as jnp
from jax import lax
from jax.experimental import pallas as pl
from jax.experimental.pallas import tpu as pltpu

F32 = jnp.float32
BF16 = jnp.bfloat16

D_MODEL = 1024
D_LRU = 1024
N_LRU_BLOCKS = 8
LRU_BLOCK = D_LRU // N_LRU_BLOCKS
CONV_WIDTH = 4
LRU_C = 8.0
HEAD_DIM = 128
HEADS_PER_GROUP = 4
WINDOWS = (128, 512, 2048)
DILATIONS = (1, 4, 16)
N_GROUPS = 3
N_BACK = 128
D_QKV = N_GROUPS * HEADS_PER_GROUP * HEAD_DIM
D_GRP = HEADS_PER_GROUP * HEAD_DIM
ATTN_SCALE = HEAD_DIM ** -0.5
N_BUCKETS = 32
MAX_DISTANCE = 2048
NORM_EPS = 1e-6
NEG = -1e30
KEY_WIN = 256
KV_ROWS = 2 * HEADS_PER_GROUP
VMEM_LIMIT = 60000 * 1024


def _rms(x, g):
    ms = jnp.mean(x * x, axis=-1, keepdims=True)
    return x * lax.rsqrt(ms + NORM_EPS) * g


def _sigmoid(x):
    return 0.5 * jnp.tanh(0.5 * x) + 0.5


def _dot(a, b):
    return jnp.dot(a, b, preferred_element_type=F32)


def _dot_nt(a, b):
    return lax.dot_general(a, b, (((1,), (1,)), ((), ())), preferred_element_type=F32)


def _bias_table(relb_ref, col, dist, valid):
    max_exact = N_BUCKETS // 2
    n_log = N_BUCKETS - max_exact
    df = jnp.maximum(dist, 1).astype(F32)
    val = jnp.log(df / max_exact) / math.log(MAX_DISTANCE / max_exact) * n_log
    out = jnp.zeros(dist.shape, F32)
    for b in range(max_exact):
        out = jnp.where(dist == b, relb_ref[b, col], out)
    for k in range(n_log):
        out = jnp.where((dist >= max_exact) & (val >= k), relb_ref[max_exact + k, col], out)
    return jnp.where(valid, out, NEG)


def _merge_groups(o_refs, lse_refs):
    lses = [r[...] for r in lse_refs]
    m = functools.reduce(jnp.maximum, lses)
    es = [jnp.exp(l - m) for l in lses]
    num = sum(e * r[...] for e, r in zip(es, o_refs))
    return num / sum(es)


N_GATE_CHUNKS = (D_GRP + 2 * D_MODEL) // D_GRP


def _gate_chunk(u, wg_ref, bm_ref, c):
    z = _dot(u, wg_ref[:, c * D_GRP:(c + 1) * D_GRP])
    if c == 0:
        return z * _sigmoid(z)
    return _sigmoid(z + bm_ref[:, (c - 1) * D_GRP:c * D_GRP])


def _tail(x, ya, ob, gate, wap_ref, wout_ref):
    yb = _dot((ob * gate[:, :D_GRP]).astype(BF16), wap_ref[...])
    merged = gate[:, D_GRP:D_GRP + D_MODEL] * ya + gate[:, D_GRP + D_MODEL:] * yb
    return x + _dot(merged.astype(BF16), wout_ref[...])


def _qkv(u, wqkv_ref, gq_ref, gk_ref, store):
    for part, g_ref in enumerate((gq_ref, gk_ref, None)):
        z = _dot(u, wqkv_ref[:, part * D_QKV:(part + 1) * D_QKV])
        zs = [z[:, h * HEAD_DIM:(h + 1) * HEAD_DIM] for h in range(D_QKV // HEAD_DIM)]
        if g_ref is not None:
            ms = [jnp.mean(zh * zh, axis=-1, keepdims=True) for zh in zs]
            zs = [zh * lax.rsqrt(m + NORM_EPS) * g_ref[...] for zh, m in zip(zs, ms)]
        for h, zh in enumerate(zs):
            store(part, h, zh)


N_LRU_IN = 13


def _shift_cache_block(blk_ref, next_ref, fresh_ref, out_ref, batch_rows):
    rpb = blk_ref.shape[0]
    hop = fresh_ref.shape[0]
    out_ref[0:rpb - hop, :] = blk_ref[hop:rpb, :]
    ends_batch = ((pl.program_id(0) + 1) * rpb) % batch_rows == 0
    out_ref[rpb - hop:rpb, :] = jnp.where(ends_batch, fresh_ref[...], next_ref[...])


def _decode_attn_part(q_ref, relb_ref, blk_refs, fresh_refs, o_ref, lse_ref, kcs, vcs, bts, spb):
    T = q_ref.shape[0]
    part = pl.program_id(0) % spb
    Lhs = tuple(blk.shape[0] // KV_ROWS for blk in blk_refs)

    @pl.when(pl.program_id(0) == 0)
    def _():
        for g in range(N_GROUPS):
            Lh, d = Lhs[g], DILATIONS[g]
            L, W = Lh * spb, Lh + HEAD_DIM
            kcs[g][Lh:W, :] = jnp.zeros((W - Lh, D_GRP), BF16)
            vcs[g][Lh:W, :] = jnp.zeros((W - Lh, D_GRP), BF16)
            tq = lax.broadcasted_iota(jnp.int32, (T, W), 0)
            c = lax.broadcasted_iota(jnp.int32, (T, W), 1)
            for j in range(spb):
                p = jnp.where(c < Lh, j * Lh + c, L + c - Lh)
                present = (c < Lh) | ((c < Lh + T) & (j == spb - 1))
                dist = L + tq - p
                valid = present & (dist >= 0) & ((dist & (d - 1)) == 0) & (dist <= N_BACK * d)
                for h in range(HEADS_PER_GROUP):
                    bts[g][j, h * T:(h + 1) * T, :] = _bias_table(relb_ref, g * HEADS_PER_GROUP + h, dist, valid)

    heads = [slice(h * HEAD_DIM, (h + 1) * HEAD_DIM) for h in range(HEADS_PER_GROUP)]
    for g in range(N_GROUPS):
        Lh = Lhs[g]
        for h, hs_ in enumerate(heads):
            kcs[g][0:Lh, hs_] = blk_refs[g][pl.ds(h, Lh, stride=KV_ROWS), :].astype(BF16)
            vcs[g][0:Lh, hs_] = blk_refs[g][pl.ds(HEADS_PER_GROUP + h, Lh, stride=KV_ROWS), :].astype(BF16)
        k_new = jnp.concatenate([fresh_refs[g][pl.ds(h, T, stride=KV_ROWS), :] for h in range(HEADS_PER_GROUP)], axis=1)
        v_new = jnp.concatenate([fresh_refs[g][pl.ds(HEADS_PER_GROUP + h, T, stride=KV_ROWS), :]
                                 for h in range(HEADS_PER_GROUP)], axis=1)
        kcs[g][Lh:Lh + 2 * T, :] = jnp.concatenate([k_new, jnp.zeros_like(k_new)], axis=0).astype(BF16)
        vcs[g][Lh:Lh + 2 * T, :] = jnp.concatenate([v_new, jnp.zeros_like(v_new)], axis=0).astype(BF16)
    for g in range(N_GROUPS):
        cols = [slice(g * D_GRP + h * HEAD_DIM, g * D_GRP + (h + 1) * HEAD_DIM) for h in range(HEADS_PER_GROUP)]
        s = jnp.concatenate([_dot_nt(q_ref[:, cs].astype(BF16), kcs[g][:, hs_]) for cs, hs_ in zip(cols, heads)], axis=0)
        s = s * ATTN_SCALE + bts[g][part]
        m = jnp.max(s, axis=-1, keepdims=True)
        p = jnp.exp(s - m)
        den = jnp.sum(p, axis=-1, keepdims=True)
        p = p * (1.0 / den)
        lse = m + jnp.log(den)
        for h, (cs, hs_) in enumerate(zip(cols, heads)):
            rows = slice(h * T, (h + 1) * T)
            o_ref[:, cs] = _dot(p[rows].astype(BF16), vcs[g][:, hs_])
            lse_ref[:, cs] = jnp.broadcast_to(lse[rows], (T, HEAD_DIM))


def _lru_body(*refs, B, tt, batch_rows, spb):
    (x_ref, gn_ref, w_ref, cw_ref, cb_ref, wa_ref, ba_ref, wx_ref, bx_ref, lam_ref, wp_ref,
     conv0_ref, h0_ref) = refs[:N_LRU_IN]
    n_shift = len(batch_rows)
    n_dec = 2 if n_shift else 0
    i0 = N_LRU_IN
    shift_in, i0 = refs[i0:i0 + 3 * n_shift], i0 + 3 * n_shift
    dec_in, i0 = refs[i0:i0 + n_dec], i0 + n_dec
    (ya_ref, convo_ref, ho_ref), i0 = refs[i0:i0 + 3], i0 + 3
    shift_out, i0 = refs[i0:i0 + n_shift], i0 + n_shift
    dec_out, i0 = refs[i0:i0 + n_dec], i0 + n_dec
    (xs_ref, ga_ref, a_ref, hs_ref, h_ref), i0 = refs[i0:i0 + 5], i0 + 5
    dec_scratch = refs[i0:]
    R = B * tt
    hist = (CONV_WIDTH - 1) * B

    @pl.when(pl.program_id(0) == 0)
    def _():
        xs_ref[0:hist, :] = conv0_ref[...]
        h_ref[...] = h0_ref[...]

    for k in range(n_shift):
        _shift_cache_block(*shift_in[3 * k:3 * k + 3], shift_out[k], batch_rows[k])
    if n_shift:
        _decode_attn_part(*dec_in, shift_in[0::3], shift_in[2::3], *dec_out,
                          dec_scratch[0:2 * n_shift:2], dec_scratch[1:2 * n_shift:2], dec_scratch[2 * n_shift:], spb)

    xt = jnp.concatenate([x_ref[:, t, :] for t in range(tt)], axis=0)
    u = _rms(xt, gn_ref[...]).astype(BF16)
    xs_ref[hist:hist + R, :] = _dot(u, w_ref[:, :D_LRU])
    ga_ref[...] = _dot(u, w_ref[:, D_LRU:])
    sp = jax.nn.softplus(-lam_ref[...])
    for n in range(N_LRU_BLOCKS):
        cs = slice(n * LRU_BLOCK, (n + 1) * LRU_BLOCK)
        y = cb_ref[:, cs] + sum(xs_ref[j * B:j * B + R, cs] * cw_ref[j:j + 1, cs] for j in range(CONV_WIDTH))
        yb = y.astype(BF16)
        r = _sigmoid(_dot(yb, wa_ref[n]) + ba_ref[:, cs])
        i = _sigmoid(_dot(yb, wx_ref[n]) + bx_ref[:, cs])
        log_a = -LRU_C * r * sp[:, cs]
        a = jnp.exp(log_a)
        th = jnp.tanh(log_a)
        a_ref[:, cs] = a
        hs_ref[:, cs] = jnp.sqrt(-2.0 * th) * lax.rsqrt(1.0 - th) * (i * y)
    h = h_ref[...]
    for t in range(tt):
        rows = slice(t * B, (t + 1) * B)
        h = a_ref[rows, :] * h + hs_ref[rows, :]
        hs_ref[rows, :] = h
    h_ref[...] = h
    g = ga_ref[...]
    ya = _dot((hs_ref[...] * (g * _sigmoid(g))).astype(BF16), wp_ref[...])
    for t in range(tt):
        ya_ref[:, t, :] = ya[t * B:(t + 1) * B].astype(ya_ref.dtype)
    tail = xs_ref[R:R + hist, :]
    convo_ref[...] = tail
    xs_ref[0:hist, :] = tail
    ho_ref[...] = h_ref[...]


def _lru_call(x, conv0, h0, gn, w_lru, cw, cb, wa, ba, wx, bx, lam, wp, *, tt, caches=(), fresh=(), n_cache_batch=1,
              q_dec=None, relb=None):
    B, S, _ = x.shape
    assert S % tt == 0 and tt >= CONV_WIDTH - 1 and tt % 8 == 0 and B % 8 == 0
    R = B * tt
    hist = (CONV_WIDTH - 1) * B
    steps = S // tt
    spb = steps // n_cache_batch
    const = lambda shape: pl.BlockSpec(shape, lambda i: (0,) * len(shape), pipeline_mode=pl.Buffered(1))
    shift_in, shift_out, shift_shapes, batch_rows = [], [], [], []
    for c, f in zip(caches, fresh):
        total = c.shape[0]
        rpb = total // steps
        rows_b = total // n_cache_batch
        hop = f.shape[0] // n_cache_batch
        assert total % steps == 0 and rows_b == spb * rpb and rpb % hop == 0 and rpb > hop
        shift_in += [
            pl.BlockSpec((rpb, HEAD_DIM), lambda i: (i, 0)),
            pl.BlockSpec((hop, HEAD_DIM), lambda i, rpb=rpb, hop=hop, total=total:
                         (jnp.minimum((i + 1) * (rpb // hop), total // hop - 1), 0)),
            pl.BlockSpec((hop, HEAD_DIM), lambda i, rpb=rpb, rows_b=rows_b: ((i * rpb) // rows_b, 0)),
        ]
        shift_out.append(pl.BlockSpec((rpb, HEAD_DIM), lambda i: (i, 0)))
        shift_shapes.append(jax.ShapeDtypeStruct(c.shape, c.dtype))
        batch_rows.append(rows_b)
    shift_args = [a for c, f in zip(caches, fresh) for a in (c, c, f)]
    dec_in, dec_out, dec_shapes, dec_scratch = [], [], [], []
    if caches:
        assert len(caches) == N_GROUPS and q_dec.shape[0] % n_cache_batch == 0
        T = q_dec.shape[0] // n_cache_batch
        part = jax.ShapeDtypeStruct((steps * T, D_QKV), F32)
        dec_in = [pl.BlockSpec((T, D_QKV), lambda i: (i // spb, 0)), pl.BlockSpec(memory_space=pltpu.SMEM)]
        dec_out = [pl.BlockSpec((T, D_QKV), lambda i: (i, 0))] * 2
        dec_shapes = [part, part]
        shift_args += [q_dec, relb]
        widths = [c.shape[0] // steps // KV_ROWS + HEAD_DIM for c in caches]
        for w in widths:
            dec_scratch += [pltpu.VMEM((w, D_GRP), BF16)] * 2
        dec_scratch += [pltpu.VMEM((spb, HEADS_PER_GROUP * T, w), F32) for w in widths]
    return pl.pallas_call(
        functools.partial(_lru_body, B=B, tt=tt, batch_rows=tuple(batch_rows), spb=spb),
        grid=(steps,),
        in_specs=[
            pl.BlockSpec((B, tt, D_MODEL), lambda i: (0, i, 0)),
            const((1, D_MODEL)), const((D_MODEL, 2 * D_LRU)), const((CONV_WIDTH, D_LRU)), const((1, D_LRU)),
            const((N_LRU_BLOCKS, LRU_BLOCK, LRU_BLOCK)), const((1, D_LRU)),
            const((N_LRU_BLOCKS, LRU_BLOCK, LRU_BLOCK)), const((1, D_LRU)), const((1, D_LRU)),
            const((D_LRU, D_MODEL)), const((hist, D_LRU)), const((B, D_LRU)),
        ] + shift_in + dec_in,
        out_specs=[
            pl.BlockSpec((B, tt, D_MODEL), lambda i: (0, i, 0)),
            pl.BlockSpec((hist, D_LRU), lambda i: (0, 0)),
            pl.BlockSpec((B, D_LRU), lambda i: (0, 0)),
        ] + shift_out + dec_out,
        out_shape=[
            jax.ShapeDtypeStruct((B, S, D_MODEL), F32),
            jax.ShapeDtypeStruct((hist, D_LRU), F32),
            jax.ShapeDtypeStruct((B, D_LRU), F32),
        ] + shift_shapes + dec_shapes,
        scratch_shapes=[
            pltpu.VMEM((hist + R, D_LRU), F32),
            pltpu.VMEM((R, D_LRU), F32),
            pltpu.VMEM((R, D_LRU), F32),
            pltpu.VMEM((R, D_LRU), F32),
            pltpu.VMEM((B, D_LRU), F32),
        ] + dec_scratch,
        compiler_params=pltpu.CompilerParams(dimension_semantics=("arbitrary",), vmem_limit_bytes=VMEM_LIMIT),
        name="lru_branch",
    )(x, gn, w_lru, cw, cb, wa, ba, wx, bx, lam, wp, conv0, h0, *shift_args)


def _attn_body(x_ref, ya_ref, gn_ref, wqkv_ref, wg_ref, gq_ref, gk_ref, bm_ref, relb_ref, wap_ref, wout_ref,
               y_ref, kv0_ref, kv1_ref, kv2_ref,
               zq_ref, zk_ref, zv_ref, qc0, qc1, qc2, hk0, hv0, hk1, hv1, hk2, hv2,
               o0, o1, o2, l0, l1, l2, bt0, bt1, bt2, s_ref, p_ref, *, S, tq):
    t = pl.program_id(1)
    nt = S // tq

    qcs, hks, hvs = (qc0, qc1, qc2), (hk0, hk1, hk2), (hv0, hv1, hv2)
    o_refs, l_refs, bts, kvs = (o0, o1, o2), (l0, l1, l2), (bt0, bt1, bt2), (kv0_ref, kv1_ref, kv2_ref)
    Qc = tuple(tq // d for d in DILATIONS)
    QB = tuple(min(q, N_BACK) for q in Qc)
    pad1 = KEY_WIN - QB[1]
    n2 = S // DILATIONS[2]

    @pl.when((pl.program_id(0) == 0) & (t == 0))
    def _():
        for ref in (hk0, hv0, hk1, hv1, hk2, hv2):
            ref[...] = jnp.zeros(ref.shape, ref.dtype)
        for g in (0, 1):
            a = lax.broadcasted_iota(jnp.int32, (QB[g], KEY_WIN), 0)
            c = lax.broadcasted_iota(jnp.int32, (QB[g], KEY_WIN), 1)
            j = a + (KEY_WIN - QB[g]) - c
            for h in range(HEADS_PER_GROUP):
                bts[g][h] = _bias_table(relb_ref, g * HEADS_PER_GROUP + h, j * DILATIONS[g], (j >= 0) & (j <= N_BACK))
        i = lax.broadcasted_iota(jnp.int32, (n2, n2), 0)
        c = lax.broadcasted_iota(jnp.int32, (n2, n2), 1)
        for h in range(HEADS_PER_GROUP):
            bt2[h] = _bias_table(relb_ref, 2 * HEADS_PER_GROUP + h, (i - c) * DILATIONS[2],
                                 (i - c >= 0) & (i - c <= N_BACK))

    x = x_ref[...]
    u = _rms(x, gn_ref[...]).astype(BF16)
    z_refs = (zq_ref, zk_ref, zv_ref)

    def store(part, head, value):
        z_refs[part][head] = value

    _qkv(u, wqkv_ref, gq_ref, gk_ref, store)

    for g in range(N_GROUPS):
        keep = min(WINDOWS[g], S)
        rb = min(tq, keep)

        @pl.when(t >= nt - keep // rb)
        def _(g=g, rb=rb):
            for part, z_ref in enumerate((zk_ref, zv_ref)):
                for h in range(HEADS_PER_GROUP):
                    dst = pl.ds(part * HEADS_PER_GROUP + h, rb, stride=KV_ROWS)
                    kvs[g][dst, :] = z_ref[g * HEADS_PER_GROUP + h, tq - rb:tq, :]

    for g in range(N_GROUPS):
        d = DILATIONS[g]
        for r in range(d):
            rows = pl.ds(r, Qc[g], stride=d) if d > 1 else slice(None)
            if g == 0:
                dst = pl.ds(tq, tq)
            elif g == 1:
                dst = pl.ds(pl.multiple_of(pad1 + t * Qc[1], Qc[1]), Qc[1])
            else:
                dst = pl.ds(pl.multiple_of(t * Qc[2], Qc[2]), Qc[2])
            for h in range(HEADS_PER_GROUP):
                hs_ = slice(h * HEAD_DIM, (h + 1) * HEAD_DIM)
                gh = g * HEADS_PER_GROUP + h
                qcs[g][r, :, hs_] = zq_ref[gh, rows, :].astype(BF16)
                hks[g][r, dst, hs_] = zk_ref[gh, rows, :].astype(BF16)
                hvs[g][r, dst, hs_] = zv_ref[gh, rows, :].astype(BF16)

    col = lax.broadcasted_iota(jnp.int32, (1, KEY_WIN), 1)
    rowi = lax.broadcasted_iota(jnp.int32, (tq, 1), 0)
    for g in range(N_GROUPS):
        d = DILATIONS[g]
        nq = Qc[g] // QB[g]
        kw = n2 if g == 2 else KEY_WIN
        units = [(r, qb) for r in range(d) for qb in range(nq)]

        def key_win(qb, g=g):
            if g == 0:
                return pl.ds(tq + qb * QB[0] + QB[0] - KEY_WIN, KEY_WIN)
            if g == 1:
                return pl.ds(pl.multiple_of(t * Qc[1], QB[1]), KEY_WIN)
            return slice(None)

        for r, qb in units:
            cm = slice(r * Qc[g] + qb * QB[g], r * Qc[g] + (qb + 1) * QB[g])
            for h in range(HEADS_PER_GROUP):
                hs_ = slice(h * HEAD_DIM, (h + 1) * HEAD_DIM)
                s_ref[h, cm, 0:kw] = _dot_nt(qcs[g][r, qb * QB[g]:(qb + 1) * QB[g], hs_], hks[g][r, key_win(qb), hs_])

        if g < 2:
            i0 = t * Qc[g] + (rowi & (Qc[g] - QB[g]))
            started = col >= KEY_WIN - QB[g] - i0
        for h in range(HEADS_PER_GROUP):
            bias = bt2[h, pl.ds(pl.multiple_of(t * QB[2], QB[2]), QB[2]), :] if g == 2 else bts[g][h]
            s = s_ref[h, :, 0:kw] * ATTN_SCALE
            s = (s.reshape(len(units), QB[g], kw) + bias[None]).reshape(tq, kw)
            if g < 2:
                s = jnp.where(started, s, NEG)
            m = jnp.max(s, axis=-1, keepdims=True)
            p = jnp.exp(s - m)
            den = jnp.sum(p, axis=-1, keepdims=True)
            p_ref[h, :, 0:kw] = (p * (1.0 / den)).astype(BF16)
            lse = jnp.broadcast_to(m + jnp.log(den), (tq, HEAD_DIM))
            for r in range(d):
                rows = pl.ds(r, Qc[g], stride=d) if d > 1 else slice(None)
                l_refs[g][h, rows, :] = lse[r * Qc[g]:(r + 1) * Qc[g]]

        for r, qb in units:
            cm = slice(r * Qc[g] + qb * QB[g], r * Qc[g] + (qb + 1) * QB[g])
            rows = pl.ds(r + d * qb * QB[g], QB[g], stride=d) if d > 1 else pl.ds(qb * QB[g], QB[g])
            for h in range(HEADS_PER_GROUP):
                hs_ = slice(h * HEAD_DIM, (h + 1) * HEAD_DIM)
                o_refs[g][h, rows, :] = _dot(p_ref[h, cm, 0:kw], hvs[g][r, key_win(qb), hs_])

    hk0[0, 0:tq, :] = hk0[0, tq:2 * tq, :]
    hv0[0, 0:tq, :] = hv0[0, tq:2 * tq, :]

    ob = _merge_groups(o_refs, l_refs)
    ob = jnp.concatenate([ob[h] for h in range(HEADS_PER_GROUP)], axis=1)
    gate = jnp.concatenate([_gate_chunk(u, wg_ref, bm_ref, c) for c in range(N_GATE_CHUNKS)], axis=1)
    y_ref[...] = _tail(x, ya_ref[...].astype(F32), ob, gate, wap_ref, wout_ref)


def _attn_call(x, ya, gn, wqkv, wg, gq, gk, bmerge, relb, wap, wout, *, tq):
    B, S, _ = x.shape
    nt = S // tq
    assert S % tq == 0 and tq % (DILATIONS[2] * 16) == 0 and tq >= N_BACK
    assert S // DILATIONS[2] == N_BACK and tq // DILATIONS[1] <= N_BACK
    Qc = tuple(tq // d for d in DILATIONS)
    QB = tuple(min(q, N_BACK) for q in Qc)
    n2 = S // DILATIONS[2]
    const = lambda shape: pl.BlockSpec(shape, lambda b, t: (0,) * len(shape), pipeline_mode=pl.Buffered(1))
    row = pl.BlockSpec((None, tq, D_MODEL), lambda b, t: (b, t, 0))

    def kv_spec(g):
        keep = min(WINDOWS[g], S)
        rb = min(tq, keep)
        first = nt - keep // rb
        return pl.BlockSpec((None, rb * KV_ROWS, HEAD_DIM), lambda b, t: (b, jnp.maximum(t - first, 0), 0))

    cls = lambda g, rows, dt: pltpu.VMEM((DILATIONS[g], rows, D_GRP), dt)
    scratch = [pltpu.VMEM((D_QKV // HEAD_DIM, tq, HEAD_DIM), F32)] * 3
    scratch += [cls(g, Qc[g], BF16) for g in range(N_GROUPS)]
    scratch += [cls(0, 2 * tq, BF16)] * 2 + [cls(1, KEY_WIN - QB[1] + S // DILATIONS[1], BF16)] * 2 + [cls(2, n2, BF16)] * 2
    scratch += [pltpu.VMEM((HEADS_PER_GROUP, tq, HEAD_DIM), F32)] * 6
    scratch += [pltpu.VMEM((HEADS_PER_GROUP, QB[0], KEY_WIN), F32), pltpu.VMEM((HEADS_PER_GROUP, QB[1], KEY_WIN), F32),
                pltpu.VMEM((HEADS_PER_GROUP, n2, n2), F32)]
    scratch += [pltpu.VMEM((HEADS_PER_GROUP, tq, KEY_WIN), F32), pltpu.VMEM((HEADS_PER_GROUP, tq, KEY_WIN), BF16)]
    return pl.pallas_call(
        functools.partial(_attn_body, S=S, tq=tq),
        grid=(B, nt),
        in_specs=[row, row, const((1, D_MODEL)), const((D_MODEL, 3 * D_QKV)), const((D_MODEL, D_GRP + 2 * D_MODEL)),
                  const((1, HEAD_DIM)), const((1, HEAD_DIM)), const((1, 2 * D_MODEL)),
                  pl.BlockSpec(memory_space=pltpu.SMEM), const((D_GRP, D_MODEL)), const((D_MODEL, D_MODEL))],
        out_specs=[row, kv_spec(0), kv_spec(1), kv_spec(2)],
        out_shape=[jax.ShapeDtypeStruct((B, S, D_MODEL), F32)]
        + [jax.ShapeDtypeStruct((B, min(WINDOWS[g], S) * KV_ROWS, HEAD_DIM), F32) for g in range(N_GROUPS)],
        scratch_shapes=scratch,
        compiler_params=pltpu.CompilerParams(dimension_semantics=("arbitrary", "arbitrary"), vmem_limit_bytes=VMEM_LIMIT),
        name="attn_branch",
    )(x, ya, gn, wqkv, wg, gq, gk, bmerge, relb, wap, wout)


def _qkv_body(x_ref, gn_ref, wqkv_ref, gq_ref, gk_ref, zq_ref, nr0_ref, nr1_ref, nr2_ref):
    M = x_ref.shape[0]
    u = _rms(x_ref[...], gn_ref[...]).astype(BF16)
    nr_refs = (nr0_ref, nr1_ref, nr2_ref)

    def store(part, head, value):
        if part == 0:
            zq_ref[:, head * HEAD_DIM:(head + 1) * HEAD_DIM] = value
        else:
            g, h = divmod(head, HEADS_PER_GROUP)
            nr_refs[g][pl.ds((part - 1) * HEADS_PER_GROUP + h, M, stride=KV_ROWS), :] = value

    _qkv(u, wqkv_ref, gq_ref, gk_ref, store)


def _qkv_call(x, gn, wqkv, gq, gk):
    M = x.shape[0]
    rows = jax.ShapeDtypeStruct((M * KV_ROWS, HEAD_DIM), F32)
    return pl.pallas_call(_qkv_body, out_shape=[jax.ShapeDtypeStruct((M, D_QKV), F32), rows, rows, rows],
                          compiler_params=pltpu.CompilerParams(vmem_limit_bytes=VMEM_LIMIT),
                          name="decode_qkv")(x, gn, wqkv, gq, gk)


def _tail_body(x_ref, ya_ref, gn_ref, wg_ref, bm_ref, wap_ref, wout_ref, o_ref, lse_ref, y_ref):
    x = x_ref[...]
    u = _rms(x, gn_ref[...]).astype(BF16)
    pieces = [(j, slice(g * D_GRP, (g + 1) * D_GRP)) for j in range(o_ref.shape[0]) for g in range(N_GROUPS)]
    ob = _merge_groups([o_ref.at[j, :, gs] for j, gs in pieces], [lse_ref.at[j, :, gs] for j, gs in pieces])
    gate = jnp.concatenate([_gate_chunk(u, wg_ref, bm_ref, c) for c in range(N_GATE_CHUNKS)], axis=1)
    y_ref[...] = _tail(x, ya_ref[...], ob, gate, wap_ref, wout_ref)


def _tail_call(x, ya, gn, wg, bmerge, wap, wout, o, lse):
    return pl.pallas_call(_tail_body, out_shape=jax.ShapeDtypeStruct(x.shape, F32),
                          compiler_params=pltpu.CompilerParams(vmem_limit_bytes=VMEM_LIMIT),
                          name="decode_tail")(x, ya, gn, wg, bmerge, wap, wout, o, lse)


def kernel(x_prompt, x_sample, cache_kv_w128, cache_kv_w512, cache_kv_w2048, state_conv, state_h, g_norm, w_in,
           b_merge, conv_w, conv_b, lru_w_a, lru_b_a, lru_w_x, lru_b_x, lru_lambda, g_q, g_k, rel_bias,
           w_lru_proj, w_attn_proj, w_out):
    assert w_in.shape[0] == 1, "single-layer step"
    B, S, _ = x_prompt.shape
    DB, T, _ = x_sample.shape
    o2 = 2 * D_LRU
    o5 = o2 + 3 * D_QKV
    w_lru = w_in[0, :, :o2].astype(BF16)
    wqkv = w_in[0, :, o2:o5].astype(BF16)
    wg = w_in[0, :, o5:].astype(BF16)
    wa, wx = lru_w_a[0].astype(BF16), lru_w_x[0].astype(BF16)
    wp, wap, wout = w_lru_proj[0].astype(BF16), w_attn_proj[0].astype(BF16), w_out[0].astype(BF16)
    lru_params = (g_norm, w_lru, conv_w[0], conv_b, wa, lru_b_a, wx, lru_b_x, lru_lambda, wp)
    hist = CONV_WIDTH - 1

    caches6 = (cache_kv_w128, cache_kv_w512, cache_kv_w2048)
    caches = [c.reshape(-1, HEAD_DIM) for c in caches6]

    xs2 = x_sample.reshape(DB * T, D_MODEL)
    zq, *newrows = _qkv_call(xs2, g_norm, wqkv, g_q, g_k)

    tt = 32
    ya_p, conv_p, h_p, *rest = _lru_call(x_prompt, jnp.zeros((hist * B, D_LRU), F32), jnp.zeros((B, D_LRU), F32),
                                         *lru_params, tt=tt, caches=caches, fresh=newrows, n_cache_batch=DB,
                                         q_dec=zq, relb=rel_bias)
    news, (o_parts, lse_parts) = rest[:N_GROUPS], rest[N_GROUPS:]
    y_p, kv0_p, kv1_p, kv2_p = _attn_call(x_prompt, ya_p, g_norm, wqkv, wg, g_q, g_k, b_merge, rel_bias, wap, wout,
                                          tq=256)

    conv0_s = jnp.swapaxes(state_conv[0], 0, 1).reshape(hist * DB, D_LRU)
    ya_s, conv_s, h_s = _lru_call(x_sample, conv0_s, state_h[0], *lru_params, tt=T)
    spb = S // tt // DB
    by_part = lambda a: jnp.swapaxes(a.reshape(DB, spb, T, D_QKV), 0, 1).reshape(spb, DB * T, D_QKV)
    y_s = _tail_call(xs2, ya_s.reshape(DB * T, D_MODEL), g_norm, wg, b_merge, wap, wout,
                     by_part(o_parts), by_part(lse_parts))

    kv_shape = lambda a: a.reshape(1, a.shape[0], a.shape[1] // KV_ROWS, 2, HEADS_PER_GROUP, HEAD_DIM)
    conv_out = lambda c, nb: jnp.swapaxes(c.reshape(hist, nb, D_LRU), 0, 1)[None]
    news = [n.reshape(c.shape) for n, c in zip(news, caches6)]
    return (y_p, y_s.reshape(DB, T, D_MODEL), kv_shape(kv0_p), kv_shape(kv1_p), kv_shape(kv2_p),
            conv_out(conv_p, B), h_p[None], news[0], news[1], news[2], conv_out(conv_s, DB), h_s[None])
```

```python
import functools
import math

import jax
import jax.numpy as jnp
from jax import lax
from jax.experimental import pallas as pl
from jax.experimental.pallas import tpu as pltpu

F32 = jnp.float32
BF16 = jnp.bfloat16

D_MODEL = 1024
D_LRU = 1024
N_LRU_BLOCKS = 8
LRU_BLOCK = D_LRU // N_LRU_BLOCKS
CONV_WIDTH = 4
LRU_C = 8.0
HEAD_DIM = 128
HEADS_PER_GROUP = 4
WINDOWS = (128, 512, 2048)
DILATIONS = (1, 4, 16)
N_GROUPS = 3
N_BACK = 128
D_QKV = N_GROUPS * HEADS_PER_GROUP * HEAD_DIM
D_GRP = HEADS_PER_GROUP * HEAD_DIM
ATTN_SCALE = HEAD_DIM ** -0.5
N_BUCKETS = 32
MAX_DISTANCE = 2048
NORM_EPS = 1e-6
NEG = -1e30
KEY_WIN = 256
KV_ROWS = 2 * HEADS_PER_GROUP
VMEM_LIMIT = 60000 * 1024


def _rms(x, g):
    ms = jnp.mean(x * x, axis=-1, keepdims=True)
    return x * lax.rsqrt(ms + NORM_EPS) * g


def _sigmoid(x):
    return 0.5 * jnp.tanh(0.5 * x) + 0.5


def _dot(a, b):
    return jnp.dot(a, b, preferred_element_type=F32)


def _dot_nt(a, b):
    return lax.dot_general(a, b, (((1,), (1,)), ((), ())), preferred_element_type=F32)


def _bias_table(relb_ref, col, dist, valid):
    max_exact = N_BUCKETS // 2
    n_log = N_BUCKETS - max_exact
    df = jnp.maximum(dist, 1).astype(F32)
    val = jnp.log(df / max_exact) / math.log(MAX_DISTANCE / max_exact) * n_log
    out = jnp.zeros(dist.shape, F32)
    for b in range(max_exact):
        out = jnp.where(dist == b, relb_ref[b, col], out)
    for k in range(n_log):
        out = jnp.where((dist >= max_exact) & (val >= k), relb_ref[max_exact + k, col], out)
    return jnp.where(valid, out, NEG)


def _merge_groups(o_refs, lse_refs):
    lses = [r[...] for r in lse_refs]
    m = functools.reduce(jnp.maximum, lses)
    es = [jnp.exp(l - m) for l in lses]
    num = sum(e * r[...] for e, r in zip(es, o_refs))
    return num / sum(es)


N_GATE_CHUNKS = (D_GRP + 2 * D_MODEL) // D_GRP


def _gate_chunk(u, wg_ref, bm_ref, c):
    z = _dot(u, wg_ref[:, c * D_GRP:(c + 1) * D_GRP])
    if c == 0:
        return z * _sigmoid(z)
    return _sigmoid(z + bm_ref[:, (c - 1) * D_GRP:c * D_GRP])


def _tail(x, ya, ob, gate, wap_ref, wout_ref):
    yb = _dot((ob * gate[:, :D_GRP]).astype(BF16), wap_ref[...])
    merged = gate[:, D_GRP:D_GRP + D_MODEL] * ya + gate[:, D_GRP + D_MODEL:] * yb
    return x + _dot(merged.astype(BF16), wout_ref[...])


def _qkv(u, wqkv_ref, gq_ref, gk_ref, store):
    for part, g_ref in enumerate((gq_ref, gk_ref, None)):
        z = _dot(u, wqkv_ref[:, part * D_QKV:(part + 1) * D_QKV])
        zs = [z[:, h * HEAD_DIM:(h + 1) * HEAD_DIM] for h in range(D_QKV // HEAD_DIM)]
        if g_ref is not None:
            ms = [jnp.mean(zh * zh, axis=-1, keepdims=True) for zh in zs]
            zs = [zh * lax.rsqrt(m + NORM_EPS) * g_ref[...] for zh, m in zip(zs, ms)]
        for h, zh in enumerate(zs):
            store(part, h, zh)


N_LRU_IN = 13


def _shift_cache_block(blk_ref, next_ref, fresh_ref, out_ref, batch_rows):
    rpb = blk_ref.shape[0]
    hop = fresh_ref.shape[0]
    out_ref[0:rpb - hop, :] = blk_ref[hop:rpb, :]
    ends_batch = ((pl.program_id(0) + 1) * rpb) % batch_rows == 0
    out_ref[rpb - hop:rpb, :] = jnp.where(ends_batch, fresh_ref[...], next_ref[...])


def _decode_keys(positions, d, T):
    return positions // d * T if d >= 2 * T else positions


def _decode_attn_part(q_ref, relb_ref, blk_refs, fresh_refs, o_ref, lse_ref, kcs, vcs, bts, spb):
    T = q_ref.shape[0]
    assert T & (T - 1) == 0
    part = pl.program_id(0) % spb
    Lhs = tuple(blk.shape[0] // KV_ROWS for blk in blk_refs)
    Ks = tuple(_decode_keys(Lh, d, T) for Lh, d in zip(Lhs, DILATIONS))

    @pl.when(pl.program_id(0) == 0)
    def _():
        for g in range(N_GROUPS):
            Lh, K, d = Lhs[g], Ks[g], DILATIONS[g]
            L, W = Lh * spb, K + HEAD_DIM
            kcs[g][K:W, :] = jnp.zeros((W - K, D_GRP), BF16)
            vcs[g][K:W, :] = jnp.zeros((W - K, D_GRP), BF16)
            tq = lax.broadcasted_iota(jnp.int32, (T, W), 0)
            c = lax.broadcasted_iota(jnp.int32, (T, W), 1)
            within = c if K == Lh else (c >> (T.bit_length() - 1)) * d + (c & (T - 1))
            for j in range(spb):
                p = jnp.where(c < K, j * Lh + within, L + c - K)
                present = (c < K) | ((c < K + T) & (j == spb - 1))
                dist = L + tq - p
                valid = present & (dist >= 0) & ((dist & (d - 1)) == 0) & (dist <= N_BACK * d)
                for h in range(HEADS_PER_GROUP):
                    bts[g][j, h * T:(h + 1) * T, :] = _bias_table(relb_ref, g * HEADS_PER_GROUP + h, dist, valid)

    heads = [slice(h * HEAD_DIM, (h + 1) * HEAD_DIM) for h in range(HEADS_PER_GROUP)]

    def gather(g):
        Lh, K, d = Lhs[g], Ks[g], DILATIONS[g]
        if K == Lh:
            rows_of = lambda r: blk_refs[g][pl.ds(r, Lh, stride=KV_ROWS), :]
        else:
            periods = blk_refs[g].reshape(Lh // d, d * KV_ROWS, HEAD_DIM)
            rows_of = lambda r: periods[:, pl.ds(r, T, stride=KV_ROWS), :].reshape(K, HEAD_DIM)
        for h, hs_ in enumerate(heads):
            kcs[g][0:K, hs_] = rows_of(h).astype(BF16)
            vcs[g][0:K, hs_] = rows_of(HEADS_PER_GROUP + h).astype(BF16)
        k_new = jnp.concatenate([fresh_refs[g][pl.ds(h, T, stride=KV_ROWS), :] for h in range(HEADS_PER_GROUP)], axis=1)
        v_new = jnp.concatenate([fresh_refs[g][pl.ds(HEADS_PER_GROUP + h, T, stride=KV_ROWS), :]
                                 for h in range(HEADS_PER_GROUP)], axis=1)
        kcs[g][K:K + 2 * T, :] = jnp.concatenate([k_new, jnp.zeros_like(k_new)], axis=0).astype(BF16)
        vcs[g][K:K + 2 * T, :] = jnp.concatenate([v_new, jnp.zeros_like(v_new)], axis=0).astype(BF16)

    def attend(g):
        cols = [slice(g * D_GRP + h * HEAD_DIM, g * D_GRP + (h + 1) * HEAD_DIM) for h in range(HEADS_PER_GROUP)]
        s = jnp.concatenate([_dot_nt(q_ref[:, cs].astype(BF16), kcs[g][:, hs_]) for cs, hs_ in zip(cols, heads)], axis=0)
        s = s * ATTN_SCALE + bts[g][part]
        m = jnp.max(s, axis=-1, keepdims=True)
        p = jnp.exp(s - m)
        den = jnp.sum(p, axis=-1, keepdims=True)
        p = p * (1.0 / den)
        lse = m + jnp.log(den)
        for h, (cs, hs_) in enumerate(zip(cols, heads)):
            rows = slice(h * T, (h + 1) * T)
            o_ref[:, cs] = _dot(p[rows].astype(BF16), vcs[g][:, hs_])
            lse_ref[:, cs] = jnp.broadcast_to(lse[rows], (T, HEAD_DIM))

    return [functools.partial(f, g) for g in range(N_GROUPS) for f in (gather, attend)]


def _lru_body(*refs, B, tt, batch_rows, spb):
    (x_ref, gn_ref, w_ref, cw_ref, cb_ref, wa_ref, ba_ref, wx_ref, bx_ref, lam_ref, wp_ref,
     conv0_ref, h0_ref) = refs[:N_LRU_IN]
    n_shift = len(batch_rows)
    n_dec = 2 if n_shift else 0
    i0 = N_LRU_IN
    shift_in, i0 = refs[i0:i0 + 3 * n_shift], i0 + 3 * n_shift
    dec_in, i0 = refs[i0:i0 + n_dec], i0 + n_dec
    (ya_ref, convo_ref, ho_ref), i0 = refs[i0:i0 + 3], i0 + 3
    shift_out, i0 = refs[i0:i0 + n_shift], i0 + n_shift
    dec_out, i0 = refs[i0:i0 + n_dec], i0 + n_dec
    (xs_ref, ga_ref, a_ref, hs_ref, h_ref), i0 = refs[i0:i0 + 5], i0 + 5
    dec_scratch = refs[i0:]
    R = B * tt
    hist = (CONV_WIDTH - 1) * B

    @pl.when(pl.program_id(0) == 0)
    def _():
        xs_ref[0:hist, :] = conv0_ref[...]
        h_ref[...] = h0_ref[...]

    for k in range(n_shift):
        _shift_cache_block(*shift_in[3 * k:3 * k + 3], shift_out[k], batch_rows[k])
    decode_stages = []
    if n_shift:
        decode_stages = _decode_attn_part(*dec_in, shift_in[0::3], shift_in[2::3], *dec_out, dec_scratch[0:2 * n_shift:2],
                                          dec_scratch[1:2 * n_shift:2], dec_scratch[2 * n_shift:], spb)
    assert len(decode_stages) <= N_LRU_BLOCKS

    xt = jnp.concatenate([x_ref[:, t, :] for t in range(tt)], axis=0)
    u = _rms(xt, gn_ref[...]).astype(BF16)
    xs_ref[hist:hist + R, :] = _dot(u, w_ref[:, :D_LRU])
    ga_ref[...] = _dot(u, w_ref[:, D_LRU:])
    sp = jax.nn.softplus(-lam_ref[...])
    for n in range(N_LRU_BLOCKS):
        if n < len(decode_stages):
            decode_stages[n]()
        cs = slice(n * LRU_BLOCK, (n + 1) * LRU_BLOCK)
        y = cb_ref[:, cs] + sum(xs_ref[j * B:j * B + R, cs] * cw_ref[j:j + 1, cs] for j in range(CONV_WIDTH))
        yb = y.astype(BF16)
        r = _sigmoid(_dot(yb, wa_ref[n]) + ba_ref[:, cs])
        i = _sigmoid(_dot(yb, wx_ref[n]) + bx_ref[:, cs])
        log_a = -LRU_C * r * sp[:, cs]
        a = jnp.exp(log_a)
        th = jnp.tanh(log_a)
        a_ref[:, cs] = a
        hs_ref[:, cs] = jnp.sqrt(-2.0 * th) * lax.rsqrt(1.0 - th) * (i * y)
    h = h_ref[...]
    for t in range(tt):
        rows = slice(t * B, (t + 1) * B)
        h = a_ref[rows, :] * h + hs_ref[rows, :]
        hs_ref[rows, :] = h
    h_ref[...] = h
    g = ga_ref[...]
    ya = _dot((hs_ref[...] * (g * _sigmoid(g))).astype(BF16), wp_ref[...])
    for t in range(tt):
        ya_ref[:, t, :] = ya[t * B:(t + 1) * B].astype(ya_ref.dtype)
    tail = xs_ref[R:R + hist, :]
    convo_ref[...] = tail
    xs_ref[0:hist, :] = tail
    ho_ref[...] = h_ref[...]


def _lru_call(x, conv0, h0, gn, w_lru, cw, cb, wa, ba, wx, bx, lam, wp, *, tt, caches=(), fresh=(), n_cache_batch=1,
              q_dec=None, relb=None):
    B, S, _ = x.shape
    assert S % tt == 0 and tt >= CONV_WIDTH - 1 and tt % 8 == 0 and B % 8 == 0
    R = B * tt
    hist = (CONV_WIDTH - 1) * B
    steps = S // tt
    spb = steps // n_cache_batch
    const = lambda shape: pl.BlockSpec(shape, lambda i: (0,) * len(shape), pipeline_mode=pl.Buffered(1))
    shift_in, shift_out, shift_shapes, batch_rows = [], [], [], []
    for c, f in zip(caches, fresh):
        total = c.shape[0]
        rpb = total // steps
        rows_b = total // n_cache_batch
        hop = f.shape[0] // n_cache_batch
        assert total % steps == 0 and rows_b == spb * rpb and rpb % hop == 0 and rpb > hop
        shift_in += [
            pl.BlockSpec((rpb, HEAD_DIM), lambda i: (i, 0)),
            pl.BlockSpec((hop, HEAD_DIM), lambda i, rpb=rpb, hop=hop, total=total:
                         (jnp.minimum((i + 1) * (rpb // hop), total // hop - 1), 0)),
            pl.BlockSpec((hop, HEAD_DIM), lambda i, rpb=rpb, rows_b=rows_b: ((i * rpb) // rows_b, 0)),
        ]
        shift_out.append(pl.BlockSpec((rpb, HEAD_DIM), lambda i: (i, 0)))
        shift_shapes.append(jax.ShapeDtypeStruct(c.shape, c.dtype))
        batch_rows.append(rows_b)
    shift_args = [a for c, f in zip(caches, fresh) for a in (c, c, f)]
    dec_in, dec_out, dec_shapes, dec_scratch = [], [], [], []
    if caches:
        assert len(caches) == N_GROUPS and q_dec.shape[0] % n_cache_batch == 0
        T = q_dec.shape[0] // n_cache_batch
        part = jax.ShapeDtypeStruct((steps * T, D_QKV), F32)
        dec_in = [pl.BlockSpec((T, D_QKV), lambda i: (i // spb, 0)), pl.BlockSpec(memory_space=pltpu.SMEM)]
        dec_out = [pl.BlockSpec((T, D_QKV), lambda i: (i, 0))] * 2
        dec_shapes = [part, part]
        shift_args += [q_dec, relb]
        assert T % 8 == 0
        widths = [_decode_keys(c.shape[0] // steps // KV_ROWS, d, T) + HEAD_DIM for c, d in zip(caches, DILATIONS)]
        for w in widths:
            dec_scratch += [pltpu.VMEM((w, D_GRP), BF16)] * 2
        dec_scratch += [pltpu.VMEM((spb, HEADS_PER_GROUP * T, w), F32) for w in widths]
    return pl.pallas_call(
        functools.partial(_lru_body, B=B, tt=tt, batch_rows=tuple(batch_rows), spb=spb),
        grid=(steps,),
        in_specs=[
            pl.BlockSpec((B, tt, D_MODEL), lambda i: (0, i, 0)),
            const((1, D_MODEL)), const((D_MODEL, 2 * D_LRU)), const((CONV_WIDTH, D_LRU)), const((1, D_LRU)),
            const((N_LRU_BLOCKS, LRU_BLOCK, LRU_BLOCK)), const((1, D_LRU)),
            const((N_LRU_BLOCKS, LRU_BLOCK, LRU_BLOCK)), const((1, D_LRU)), const((1, D_LRU)),
            const((D_LRU, D_MODEL)), const((hist, D_LRU)), const((B, D_LRU)),
        ] + shift_in + dec_in,
        out_specs=[
            pl.BlockSpec((B, tt, D_MODEL), lambda i: (0, i, 0)),
            pl.BlockSpec((hist, D_LRU), lambda i: (0, 0)),
            pl.BlockSpec((B, D_LRU), lambda i: (0, 0)),
        ] + shift_out + dec_out,
        out_shape=[
            jax.ShapeDtypeStruct((B, S, D_MODEL), F32),
            jax.ShapeDtypeStruct((hist, D_LRU), F32),
            jax.ShapeDtypeStruct((B, D_LRU), F32),
        ] + shift_shapes + dec_shapes,
        scratch_shapes=[
            pltpu.VMEM((hist + R, D_LRU), F32),
            pltpu.VMEM((R, D_LRU), F32),
            pltpu.VMEM((R, D_LRU), F32),
            pltpu.VMEM((R, D_LRU), F32),
            pltpu.VMEM((B, D_LRU), F32),
        ] + dec_scratch,
        compiler_params=pltpu.CompilerParams(dimension_semantics=("arbitrary",), vmem_limit_bytes=VMEM_LIMIT),
        name="lru_branch",
    )(x, gn, w_lru, cw, cb, wa, ba, wx, bx, lam, wp, conv0, h0, *shift_args)


def _attn_body(x_ref, ya_ref, gn_ref, wqkv_ref, wg_ref, gq_ref, gk_ref, bm_ref, relb_ref, wap_ref, wout_ref,
               y_ref, kv0_ref, kv1_ref, kv2_ref,
               zq_ref, zk_ref, zv_ref, qc0, qc1, qc2, hk0, hv0, hk1, hv1, hk2, hv2,
               o0, o1, o2, l0, l1, l2, bt0, bt1, bt2, s_ref, p_ref, *, S, tq):
    t = pl.program_id(1)
    nt = S // tq

    qcs, hks, hvs = (qc0, qc1, qc2), (hk0, hk1, hk2), (hv0, hv1, hv2)
    o_refs, l_refs, bts, kvs = (o0, o1, o2), (l0, l1, l2), (bt0, bt1, bt2), (kv0_ref, kv1_ref, kv2_ref)
    Qc = tuple(tq // d for d in DILATIONS)
    QB = tuple(min(q, N_BACK) for q in Qc)
    pad1 = KEY_WIN - QB[1]
    n2 = S // DILATIONS[2]

    @pl.when((pl.program_id(0) == 0) & (t == 0))
    def _():
        for ref in (hk0, hv0, hk1, hv1, hk2, hv2):
            ref[...] = jnp.zeros(ref.shape, ref.dtype)
        for g in (0, 1):
            a = lax.broadcasted_iota(jnp.int32, (QB[g], KEY_WIN), 0)
            c = lax.broadcasted_iota(jnp.int32, (QB[g], KEY_WIN), 1)
            j = a + (KEY_WIN - QB[g]) - c
            for h in range(HEADS_PER_GROUP):
                bts[g][h] = _bias_table(relb_ref, g * HEADS_PER_GROUP + h, j * DILATIONS[g], (j >= 0) & (j <= N_BACK))
        i = lax.broadcasted_iota(jnp.int32, (n2, n2), 0)
        c = lax.broadcasted_iota(jnp.int32, (n2, n2), 1)
        for h in range(HEADS_PER_GROUP):
            bt2[h] = _bias_table(relb_ref, 2 * HEADS_PER_GROUP + h, (i - c) * DILATIONS[2],
                                 (i - c >= 0) & (i - c <= N_BACK))

    x = x_ref[...]
    u = _rms(x, gn_ref[...]).astype(BF16)
    z_refs = (zq_ref, zk_ref, zv_ref)

    def store(part, head, value):
        z_refs[part][head] = value

    _qkv(u, wqkv_ref, gq_ref, gk_ref, store)

    for g in range(N_GROUPS):
        keep = min(WINDOWS[g], S)
        rb = min(tq, keep)

        @pl.when(t >= nt - keep // rb)
        def _(g=g, rb=rb):
            for part, z_ref in enumerate((zk_ref, zv_ref)):
                for h in range(HEADS_PER_GROUP):
                    dst = pl.ds(part * HEADS_PER_GROUP + h, rb, stride=KV_ROWS)
                    kvs[g][dst, :] = z_ref[g * HEADS_PER_GROUP + h, tq - rb:tq, :]

    for g in range(N_GROUPS):
        d = DILATIONS[g]
        for r in range(d):
            rows = pl.ds(r, Qc[g], stride=d) if d > 1 else slice(None)
            if g == 0:
                dst = pl.ds(tq, tq)
            elif g == 1:
                dst = pl.ds(pl.multiple_of(pad1 + t * Qc[1], Qc[1]), Qc[1])
            else:
                dst = pl.ds(pl.multiple_of(t * Qc[2], Qc[2]), Qc[2])
            for h in range(HEADS_PER_GROUP):
                hs_ = slice(h * HEAD_DIM, (h + 1) * HEAD_DIM)
                gh = g * HEADS_PER_GROUP + h
                qcs[g][r, :, hs_] = zq_ref[gh, rows, :].astype(BF16)
                hks[g][r, dst, hs_] = zk_ref[gh, rows, :].astype(BF16)
                hvs[g][r, dst, hs_] = zv_ref[gh, rows, :].astype(BF16)

    col = lax.broadcasted_iota(jnp.int32, (1, KEY_WIN), 1)
    rowi = lax.broadcasted_iota(jnp.int32, (tq, 1), 0)
    for g in range(N_GROUPS):
        d = DILATIONS[g]
        nq = Qc[g] // QB[g]
        kw = n2 if g == 2 else KEY_WIN
        units = [(r, qb) for r in range(d) for qb in range(nq)]

        def key_win(qb, g=g):
            if g == 0:
                return pl.ds(tq + qb * QB[0] + QB[0] - KEY_WIN, KEY_WIN)
            if g == 1:
                return pl.ds(pl.multiple_of(t * Qc[1], QB[1]), KEY_WIN)
            return slice(None)

        for r, qb in units:
            cm = slice(r * Qc[g] + qb * QB[g], r * Qc[g] + (qb + 1) * QB[g])
            for h in range(HEADS_PER_GROUP):
                hs_ = slice(h * HEAD_DIM, (h + 1) * HEAD_DIM)
                s_ref[h, cm, 0:kw] = _dot_nt(qcs[g][r, qb * QB[g]:(qb + 1) * QB[g], hs_], hks[g][r, key_win(qb), hs_])

        if g < 2:
            i0 = t * Qc[g] + (rowi & (Qc[g] - QB[g]))
            started = col >= KEY_WIN - QB[g] - i0
        for h in range(HEADS_PER_GROUP):
            bias = bt2[h, pl.ds(pl.multiple_of(t * QB[2], QB[2]), QB[2]), :] if g == 2 else bts[g][h]
            s = s_ref[h, :, 0:kw] * ATTN_SCALE
            s = (s.reshape(len(units), QB[g], kw) + bias[None]).reshape(tq, kw)
            if g < 2:
                s = jnp.where(started, s, NEG)
            m = jnp.max(s, axis=-1, keepdims=True)
            p = jnp.exp(s - m)
            den = jnp.sum(p, axis=-1, keepdims=True)
            p_ref[h, :, 0:kw] = (p * (1.0 / den)).astype(BF16)
            lse = jnp.broadcast_to(m + jnp.log(den), (tq, HEAD_DIM))
            for r in range(d):
                rows = pl.ds(r, Qc[g], stride=d) if d > 1 else slice(None)
                l_refs[g][h, rows, :] = lse[r * Qc[g]:(r + 1) * Qc[g]]

        for r, qb in units:
            cm = slice(r * Qc[g] + qb * QB[g], r * Qc[g] + (qb + 1) * QB[g])
            rows = pl.ds(r + d * qb * QB[g], QB[g], stride=d) if d > 1 else pl.ds(qb * QB[g], QB[g])
            for h in range(HEADS_PER_GROUP):
                hs_ = slice(h * HEAD_DIM, (h + 1) * HEAD_DIM)
                o_refs[g][h, rows, :] = _dot(p_ref[h, cm, 0:kw], hvs[g][r, key_win(qb), hs_])

    hk0[0, 0:tq, :] = hk0[0, tq:2 * tq, :]
    hv0[0, 0:tq, :] = hv0[0, tq:2 * tq, :]

    ob = _merge_groups(o_refs, l_refs)
    ob = jnp.concatenate([ob[h] for h in range(HEADS_PER_GROUP)], axis=1)
    gate = jnp.concatenate([_gate_chunk(u, wg_ref, bm_ref, c) for c in range(N_GATE_CHUNKS)], axis=1)
    y_ref[...] = _tail(x, ya_ref[...].astype(F32), ob, gate, wap_ref, wout_ref)


def _attn_call(x, ya, gn, wqkv, wg, gq, gk, bmerge, relb, wap, wout, *, tq):
    B, S, _ = x.shape
    nt = S // tq
    assert S % tq == 0 and tq % (DILATIONS[2] * 16) == 0 and tq >= N_BACK
    assert S // DILATIONS[2] == N_BACK and tq // DILATIONS[1] <= N_BACK
    Qc = tuple(tq // d for d in DILATIONS)
    QB = tuple(min(q, N_BACK) for q in Qc)
    n2 = S // DILATIONS[2]
    const = lambda shape: pl.BlockSpec(shape, lambda b, t: (0,) * len(shape), pipeline_mode=pl.Buffered(1))
    row = pl.BlockSpec((None, tq, D_MODEL), lambda b, t: (b, t, 0))

    def kv_spec(g):
        keep = min(WINDOWS[g], S)
        rb = min(tq, keep)
        first = nt - keep // rb
        return pl.BlockSpec((None, rb * KV_ROWS, HEAD_DIM), lambda b, t: (b, jnp.maximum(t - first, 0), 0))

    cls = lambda g, rows, dt: pltpu.VMEM((DILATIONS[g], rows, D_GRP), dt)
    scratch = [pltpu.VMEM((D_QKV // HEAD_DIM, tq, HEAD_DIM), F32)] * 3
    scratch += [cls(g, Qc[g], BF16) for g in range(N_GROUPS)]
    scratch += [cls(0, 2 * tq, BF16)] * 2 + [cls(1, KEY_WIN - QB[1] + S // DILATIONS[1], BF16)] * 2 + [cls(2, n2, BF16)] * 2
    scratch += [pltpu.VMEM((HEADS_PER_GROUP, tq, HEAD_DIM), F32)] * 6
    scratch += [pltpu.VMEM((HEADS_PER_GROUP, QB[0], KEY_WIN), F32), pltpu.VMEM((HEADS_PER_GROUP, QB[1], KEY_WIN), F32),
                pltpu.VMEM((HEADS_PER_GROUP, n2, n2), F32)]
    scratch += [pltpu.VMEM((HEADS_PER_GROUP, tq, KEY_WIN), F32), pltpu.VMEM((HEADS_PER_GROUP, tq, KEY_WIN), BF16)]
    return pl.pallas_call(
        functools.partial(_attn_body, S=S, tq=tq),
        grid=(B, nt),
        in_specs=[row, row, const((1, D_MODEL)), const((D_MODEL, 3 * D_QKV)), const((D_MODEL, D_GRP + 2 * D_MODEL)),
                  const((1, HEAD_DIM)), const((1, HEAD_DIM)), const((1, 2 * D_MODEL)),
                  pl.BlockSpec(memory_space=pltpu.SMEM), const((D_GRP, D_MODEL)), const((D_MODEL, D_MODEL))],
        out_specs=[row, kv_spec(0), kv_spec(1), kv_spec(2)],
        out_shape=[jax.ShapeDtypeStruct((B, S, D_MODEL), F32)]
        + [jax.ShapeDtypeStruct((B, min(WINDOWS[g], S) * KV_ROWS, HEAD_DIM), F32) for g in range(N_GROUPS)],
        scratch_shapes=scratch,
        compiler_params=pltpu.CompilerParams(dimension_semantics=("arbitrary", "arbitrary"), vmem_limit_bytes=VMEM_LIMIT),
        name="attn_branch",
    )(x, ya, gn, wqkv, wg, gq, gk, bmerge, relb, wap, wout)


def _qkv_body(x_ref, gn_ref, wqkv_ref, gq_ref, gk_ref, zq_ref, nr0_ref, nr1_ref, nr2_ref):
    M = x_ref.shape[0]
    u = _rms(x_ref[...], gn_ref[...]).astype(BF16)
    nr_refs = (nr0_ref, nr1_ref, nr2_ref)

    def store(part, head, value):
        if part == 0:
            zq_ref[:, head * HEAD_DIM:(head + 1) * HEAD_DIM] = value
        else:
            g, h = divmod(head, HEADS_PER_GROUP)
            nr_refs[g][pl.ds((part - 1) * HEADS_PER_GROUP + h, M, stride=KV_ROWS), :] = value

    _qkv(u, wqkv_ref, gq_ref, gk_ref, store)


def _qkv_call(x, gn, wqkv, gq, gk):
    M = x.shape[0]
    rows = jax.ShapeDtypeStruct((M * KV_ROWS, HEAD_DIM), F32)
    return pl.pallas_call(_qkv_body, out_shape=[jax.ShapeDtypeStruct((M, D_QKV), F32), rows, rows, rows],
                          compiler_params=pltpu.CompilerParams(vmem_limit_bytes=VMEM_LIMIT),
                          name="decode_qkv")(x, gn, wqkv, gq, gk)


def _tail_body(x_ref, ya_ref, gn_ref, wg_ref, bm_ref, wap_ref, wout_ref, o_ref, lse_ref, y_ref):
    x = x_ref[...]
    u = _rms(x, gn_ref[...]).astype(BF16)
    pieces = [(j, slice(g * D_GRP, (g + 1) * D_GRP)) for j in range(o_ref.shape[0]) for g in range(N_GROUPS)]
    ob = _merge_groups([o_ref.at[j, :, gs] for j, gs in pieces], [lse_ref.at[j, :, gs] for j, gs in pieces])
    gate = jnp.concatenate([_gate_chunk(u, wg_ref, bm_ref, c) for c in range(N_GATE_CHUNKS)], axis=1)
    y_ref[...] = _tail(x, ya_ref[...], ob, gate, wap_ref, wout_ref)


def _tail_call(x, ya, gn, wg, bmerge, wap, wout, o, lse):
    return pl.pallas_call(_tail_body, out_shape=jax.ShapeDtypeStruct(x.shape, F32),
                          compiler_params=pltpu.CompilerParams(vmem_limit_bytes=VMEM_LIMIT),
                          name="decode_tail")(x, ya, gn, wg, bmerge, wap, wout, o, lse)


def kernel(x_prompt, x_sample, cache_kv_w128, cache_kv_w512, cache_kv_w2048, state_conv, state_h, g_norm, w_in,
           b_merge, conv_w, conv_b, lru_w_a, lru_b_a, lru_w_x, lru_b_x, lru_lambda, g_q, g_k, rel_bias,
           w_lru_proj, w_attn_proj, w_out):
    assert w_in.shape[0] == 1, "single-layer step"
    B, S, _ = x_prompt.shape
    DB, T, _ = x_sample.shape
    o2 = 2 * D_LRU
    o5 = o2 + 3 * D_QKV
    w_lru = w_in[0, :, :o2].astype(BF16)
    wqkv = w_in[0, :, o2:o5].astype(BF16)
    wg = w_in[0, :, o5:].astype(BF16)
    wa, wx = lru_w_a[0].astype(BF16), lru_w_x[0].astype(BF16)
    wp, wap, wout = w_lru_proj[0].astype(BF16), w_attn_proj[0].astype(BF16), w_out[0].astype(BF16)
    lru_params = (g_norm, w_lru, conv_w[0], conv_b, wa, lru_b_a, wx, lru_b_x, lru_lambda, wp)
    hist = CONV_WIDTH - 1

    caches6 = (cache_kv_w128, cache_kv_w512, cache_kv_w2048)
    caches = [c.reshape(-1, HEAD_DIM) for c in caches6]

    xs2 = x_sample.reshape(DB * T, D_MODEL)
    zq, *newrows = _qkv_call(xs2, g_norm, wqkv, g_q, g_k)

    tt = 32
    ya_p, conv_p, h_p, *rest = _lru_call(x_prompt, jnp.zeros((hist * B, D_LRU), F32), jnp.zeros((B, D_LRU), F32),
                                         *lru_params, tt=tt, caches=caches, fresh=newrows, n_cache_batch=DB,
                                         q_dec=zq, relb=rel_bias)
    news, (o_parts, lse_parts) = rest[:N_GROUPS], rest[N_GROUPS:]
    y_p, kv0_p, kv1_p, kv2_p = _attn_call(x_prompt, ya_p, g_norm, wqkv, wg, g_q, g_k, b_merge, rel_bias, wap, wout,
                                          tq=256)

    conv0_s = jnp.swapaxes(state_conv[0], 0, 1).reshape(hist * DB, D_LRU)
    ya_s, conv_s, h_s = _lru_call(x_sample, conv0_s, state_h[0], *lru_params, tt=T)
    spb = S // tt // DB
    by_part = lambda a: jnp.swapaxes(a.reshape(DB, spb, T, D_QKV), 0, 1).reshape(spb, DB * T, D_QKV)
    y_s = _tail_call(xs2, ya_s.reshape(DB * T, D_MODEL), g_norm, wg, b_merge, wap, wout,
                     by_part(o_parts), by_part(lse_parts))

    kv_shape = lambda a: a.reshape(1, a.shape[0], a.shape[1] // KV_ROWS, 2, HEADS_PER_GROUP, HEAD_DIM)
    conv_out = lambda c, nb: jnp.swapaxes(c.reshape(hist, nb, D_LRU), 0, 1)[None]
    news = [n.reshape(c.shape) for n, c in zip(news, caches6)]
    return (y_p, y_s.reshape(DB, T, D_MODEL), kv_shape(kv0_p), kv_shape(kv1_p), kv_shape(kv2_p),
            conv_out(conv_p, B), h_p[None], news[0], news[1], news[2], conv_out(conv_s, DB), h_s[None])
```

```python
import functools
import math

import jax
import jax.numpy as jnp
from jax import lax
from jax.experimental import pallas as pl
from jax.experimental.pallas import tpu as pltpu

F32 = jnp.float32
BF16 = jnp.bfloat16

D_MODEL = 1024
D_LRU = 1024
N_LRU_BLOCKS = 8
LRU_BLOCK = D_LRU // N_LRU_BLOCKS
CONV_WIDTH = 4
LRU_C = 8.0
HEAD_DIM = 128
HEADS_PER_GROUP = 4
WINDOWS = (128, 512, 2048)
DILATIONS = (1, 4, 16)
N_GROUPS = 3
N_BACK = 128
D_QKV = N_GROUPS * HEADS_PER_GROUP * HEAD_DIM
D_GRP = HEADS_PER_GROUP * HEAD_DIM
ATTN_SCALE = HEAD_DIM ** -0.5
N_BUCKETS = 32
MAX_DISTANCE = 2048
NORM_EPS = 1e-6
NEG = -1e30
KEY_WIN = 256
KV_ROWS = 2 * HEADS_PER_GROUP
VMEM_LIMIT = 60000 * 1024
LRU_TIME_TILE = 64
ATTN_ROW_TILE = 256


def _rms(x, g):
    ms = jnp.mean(x * x, axis=-1, keepdims=True)
    return x * lax.rsqrt(ms + NORM_EPS) * g


def _sigmoid(x):
    return 0.5 * jnp.tanh(0.5 * x) + 0.5


def _dot(a, b):
    return jnp.dot(a, b, preferred_element_type=F32)


def _dot_nt(a, b):
    return lax.dot_general(a, b, (((1,), (1,)), ((), ())), preferred_element_type=F32)


def _bias_table(relb_ref, col, dist, valid):
    max_exact = N_BUCKETS // 2
    n_log = N_BUCKETS - max_exact
    df = jnp.maximum(dist, 1).astype(F32)
    val = jnp.log(df / max_exact) / math.log(MAX_DISTANCE / max_exact) * n_log
    out = jnp.zeros(dist.shape, F32)
    for b in range(max_exact):
        out = jnp.where(dist == b, relb_ref[b, col], out)
    for k in range(n_log):
        out = jnp.where((dist >= max_exact) & (val >= k), relb_ref[max_exact + k, col], out)
    return jnp.where(valid, out, NEG)


def _merge_groups(o_refs, lse_refs):
    lses = [r[...] for r in lse_refs]
    m = functools.reduce(jnp.maximum, lses)
    es = [jnp.exp(l - m) for l in lses]
    num = sum(e * r[...] for e, r in zip(es, o_refs))
    return num / sum(es)


N_GATE_CHUNKS = (D_GRP + 2 * D_MODEL) // D_GRP


def _gate_chunk(u, wg_ref, bm_ref, c):
    z = _dot(u, wg_ref[:, c * D_GRP:(c + 1) * D_GRP])
    if c == 0:
        return z * _sigmoid(z)
    return _sigmoid(z + bm_ref[:, (c - 1) * D_GRP:c * D_GRP])


def _tail(x, ya, ob, gate, wap_ref, wout_ref):
    yb = _dot((ob * gate[:, :D_GRP]).astype(BF16), wap_ref[...])
    merged = gate[:, D_GRP:D_GRP + D_MODEL] * ya + gate[:, D_GRP + D_MODEL:] * yb
    return x + _dot(merged.astype(BF16), wout_ref[...])


def _qkv(u, wqkv_ref, gq_ref, gk_ref, store):
    for part, g_ref in enumerate((gq_ref, gk_ref, None)):
        z = _dot(u, wqkv_ref[:, part * D_QKV:(part + 1) * D_QKV])
        zs = [z[:, h * HEAD_DIM:(h + 1) * HEAD_DIM] for h in range(D_QKV // HEAD_DIM)]
        if g_ref is not None:
            ms = [jnp.mean(zh * zh, axis=-1, keepdims=True) for zh in zs]
            zs = [zh * lax.rsqrt(m + NORM_EPS) * g_ref[...] for zh, m in zip(zs, ms)]
        for h, zh in enumerate(zs):
            store(part, h, zh)


N_LRU_IN = 13


def _shift_cache_block(blk_ref, next_ref, fresh_ref, out_hbm, tail_ref, body_sem, tail_sem, batch_rows):
    rpb = blk_ref.shape[0]
    hop = fresh_ref.shape[0]
    r0 = pl.multiple_of(pl.program_id(0) * rpb, rpb)
    ends_batch = ((pl.program_id(0) + 1) * rpb) % batch_rows == 0
    tail_ref[...] = jnp.where(ends_batch, fresh_ref[...], next_ref[...])
    return (pltpu.make_async_copy(blk_ref.at[pl.ds(hop, rpb - hop), :], out_hbm.at[pl.ds(r0, rpb - hop), :], body_sem),
            pltpu.make_async_copy(tail_ref, out_hbm.at[pl.ds(r0 + rpb - hop, hop), :], tail_sem))


def _decode_keys(positions, d, T):
    return positions // d * T if d >= 2 * T else positions


def _decode_attn_part(q_ref, relb_ref, blk_refs, fresh_refs, o_ref, lse_ref, kcs, vcs, bts, spb):
    T = q_ref.shape[0]
    assert T & (T - 1) == 0
    part = pl.program_id(0) % spb
    Lhs = tuple(blk.shape[0] // KV_ROWS for blk in blk_refs)
    Ks = tuple(_decode_keys(Lh, d, T) for Lh, d in zip(Lhs, DILATIONS))

    @pl.when(pl.program_id(0) == 0)
    def _():
        for g in range(N_GROUPS):
            Lh, K, d = Lhs[g], Ks[g], DILATIONS[g]
            L, W = Lh * spb, K + HEAD_DIM
            kcs[g][K:W, :] = jnp.zeros((W - K, D_GRP), BF16)
            vcs[g][K:W, :] = jnp.zeros((W - K, D_GRP), BF16)
            tq = lax.broadcasted_iota(jnp.int32, (T, W), 0)
            c = lax.broadcasted_iota(jnp.int32, (T, W), 1)
            within = c if K == Lh else (c >> (T.bit_length() - 1)) * d + (c & (T - 1))
            for j in range(spb):
                p = jnp.where(c < K, j * Lh + within, L + c - K)
                present = (c < K) | ((c < K + T) & (j == spb - 1))
                dist = L + tq - p
                valid = present & (dist >= 0) & ((dist & (d - 1)) == 0) & (dist <= N_BACK * d)
                for h in range(HEADS_PER_GROUP):
                    bts[g][j, h * T:(h + 1) * T, :] = _bias_table(relb_ref, g * HEADS_PER_GROUP + h, dist, valid)

    heads = [slice(h * HEAD_DIM, (h + 1) * HEAD_DIM) for h in range(HEADS_PER_GROUP)]

    def gather(g):
        Lh, K, d = Lhs[g], Ks[g], DILATIONS[g]
        if K == Lh:
            rows_of = lambda r: blk_refs[g][pl.ds(r, Lh, stride=KV_ROWS), :]
        else:
            periods = blk_refs[g].reshape(Lh // d, d * KV_ROWS, HEAD_DIM)
            rows_of = lambda r: periods[:, pl.ds(r, T, stride=KV_ROWS), :].reshape(K, HEAD_DIM)
        for h, hs_ in enumerate(heads):
            kcs[g][0:K, hs_] = rows_of(h).astype(BF16)
            vcs[g][0:K, hs_] = rows_of(HEADS_PER_GROUP + h).astype(BF16)
        k_new = jnp.concatenate([fresh_refs[g][pl.ds(h, T, stride=KV_ROWS), :] for h in range(HEADS_PER_GROUP)], axis=1)
        v_new = jnp.concatenate([fresh_refs[g][pl.ds(HEADS_PER_GROUP + h, T, stride=KV_ROWS), :]
                                 for h in range(HEADS_PER_GROUP)], axis=1)
        kcs[g][K:K + 2 * T, :] = jnp.concatenate([k_new, jnp.zeros_like(k_new)], axis=0).astype(BF16)
        vcs[g][K:K + 2 * T, :] = jnp.concatenate([v_new, jnp.zeros_like(v_new)], axis=0).astype(BF16)

    def attend(g):
        cols = [slice(g * D_GRP + h * HEAD_DIM, g * D_GRP + (h + 1) * HEAD_DIM) for h in range(HEADS_PER_GROUP)]
        s = jnp.concatenate([_dot_nt(q_ref[:, cs].astype(BF16), kcs[g][:, hs_]) for cs, hs_ in zip(cols, heads)], axis=0)
        s = s * ATTN_SCALE + bts[g][part]
        m = jnp.max(s, axis=-1, keepdims=True)
        p = jnp.exp(s - m)
        den = jnp.sum(p, axis=-1, keepdims=True)
        p = p * (1.0 / den)
        lse = m + jnp.log(den)
        for h, (cs, hs_) in enumerate(zip(cols, heads)):
            rows = slice(h * T, (h + 1) * T)
            o_ref[:, cs] = _dot(p[rows].astype(BF16), vcs[g][:, hs_])
            lse_ref[:, cs] = jnp.broadcast_to(lse[rows], (T, HEAD_DIM))

    return [functools.partial(f, g) for g in range(N_GROUPS) for f in (gather, attend)]


def _lru_body(*refs, B, tt, batch_rows, spb):
    (x_ref, gn_ref, w_ref, cw_ref, cb_ref, wa_ref, ba_ref, wx_ref, bx_ref, lam_ref, wp_ref,
     conv0_ref, h0_ref) = refs[:N_LRU_IN]
    n_shift = len(batch_rows)
    n_dec = 2 if n_shift else 0
    i0 = N_LRU_IN
    shift_in, i0 = refs[i0:i0 + 3 * n_shift], i0 + 3 * n_shift
    dec_in, i0 = refs[i0:i0 + n_dec], i0 + n_dec
    (ya_ref, convo_ref, ho_ref), i0 = refs[i0:i0 + 3], i0 + 3
    shift_out, i0 = refs[i0:i0 + n_shift], i0 + n_shift
    dec_out, i0 = refs[i0:i0 + n_dec], i0 + n_dec
    (xs_ref, ga_ref, a_ref, hs_ref, h_ref), i0 = refs[i0:i0 + 5], i0 + 5
    dec_scratch = refs[i0:]
    R = B * tt
    hist = (CONV_WIDTH - 1) * B

    @pl.when(pl.program_id(0) == 0)
    def _():
        xs_ref[0:hist, :] = conv0_ref[...]
        h_ref[...] = h0_ref[...]

    shift_copies, decode_stages = [], []
    if n_shift:
        kcs, vcs = dec_scratch[0:2 * n_shift:2], dec_scratch[1:2 * n_shift:2]
        bts, tails, sem = dec_scratch[2 * n_shift:3 * n_shift], dec_scratch[3 * n_shift:4 * n_shift], dec_scratch[-1]
        for k in range(n_shift):
            shift_copies += _shift_cache_block(*shift_in[3 * k:3 * k + 3], shift_out[k], tails[k], sem.at[2 * k],
                                               sem.at[2 * k + 1], batch_rows[k])
        decode_stages = _decode_attn_part(*dec_in, shift_in[0::3], shift_in[2::3], *dec_out, kcs, vcs, bts, spb)
    for cp in shift_copies:
        cp.start()
    assert len(decode_stages) <= N_LRU_BLOCKS

    xt = jnp.concatenate([x_ref[:, t, :] for t in range(tt)], axis=0)
    u = _rms(xt, gn_ref[...]).astype(BF16)
    xs_ref[hist:hist + R, :] = _dot(u, w_ref[:, :D_LRU])
    ga_ref[...] = _dot(u, w_ref[:, D_LRU:])
    sp = jax.nn.softplus(-lam_ref[...])
    for n in range(N_LRU_BLOCKS):
        if n < len(decode_stages):
            decode_stages[n]()
        cs = slice(n * LRU_BLOCK, (n + 1) * LRU_BLOCK)
        y = cb_ref[:, cs] + sum(xs_ref[j * B:j * B + R, cs] * cw_ref[j:j + 1, cs] for j in range(CONV_WIDTH))
        yb = y.astype(BF16)
        r = _sigmoid(_dot(yb, wa_ref[n]) + ba_ref[:, cs])
        i = _sigmoid(_dot(yb, wx_ref[n]) + bx_ref[:, cs])
        log_a = -LRU_C * r * sp[:, cs]
        a = jnp.exp(log_a)
        th = jnp.tanh(log_a)
        a_ref[:, cs] = a
        hs_ref[:, cs] = jnp.sqrt(-2.0 * th) * lax.rsqrt(1.0 - th) * (i * y)
    h = h_ref[...]
    for t in range(tt):
        rows = slice(t * B, (t + 1) * B)
        h = a_ref[rows, :] * h + hs_ref[rows, :]
        hs_ref[rows, :] = h
    h_ref[...] = h
    g = ga_ref[...]
    ya = _dot((hs_ref[...] * (g * _sigmoid(g))).astype(BF16), wp_ref[...])
    for t in range(tt):
        ya_ref[:, t, :] = ya[t * B:(t + 1) * B].astype(ya_ref.dtype)
    tail = xs_ref[R:R + hist, :]
    convo_ref[...] = tail
    xs_ref[0:hist, :] = tail
    ho_ref[...] = h_ref[...]
    for cp in shift_copies:
        cp.wait()


def _lru_call(x, conv0, h0, gn, w_lru, cw, cb, wa, ba, wx, bx, lam, wp, *, tt, caches=(), fresh=(), n_cache_batch=1,
              q_dec=None, relb=None):
    B, S, _ = x.shape
    assert S % tt == 0 and tt >= CONV_WIDTH - 1 and tt % 8 == 0 and B % 8 == 0
    R = B * tt
    hist = (CONV_WIDTH - 1) * B
    steps = S // tt
    spb = steps // n_cache_batch
    const = lambda shape: pl.BlockSpec(shape, lambda i: (0,) * len(shape), pipeline_mode=pl.Buffered(1))
    shift_in, shift_out, shift_shapes, batch_rows = [], [], [], []
    for c, f in zip(caches, fresh):
        total = c.shape[0]
        rpb = total // steps
        rows_b = total // n_cache_batch
        hop = f.shape[0] // n_cache_batch
        assert total % steps == 0 and rows_b == spb * rpb and rpb % hop == 0 and rpb > hop
        shift_in += [
            pl.BlockSpec((rpb, HEAD_DIM), lambda i: (i, 0)),
            pl.BlockSpec((hop, HEAD_DIM), lambda i, rpb=rpb, hop=hop, total=total:
                         (jnp.minimum((i + 1) * (rpb // hop), total // hop - 1), 0)),
            pl.BlockSpec((hop, HEAD_DIM), lambda i, rpb=rpb, rows_b=rows_b: ((i * rpb) // rows_b, 0)),
        ]
        shift_out.append(pl.BlockSpec(memory_space=pl.ANY))
        shift_shapes.append(jax.ShapeDtypeStruct(c.shape, c.dtype))
        batch_rows.append(rows_b)
    shift_args = [a for c, f in zip(caches, fresh) for a in (c, c, f)]
    dec_in, dec_out, dec_shapes, dec_scratch = [], [], [], []
    if caches:
        assert len(caches) == N_GROUPS and q_dec.shape[0] % n_cache_batch == 0
        T = q_dec.shape[0] // n_cache_batch
        part = jax.ShapeDtypeStruct((steps * T, D_QKV), F32)
        dec_in = [pl.BlockSpec((T, D_QKV), lambda i: (i // spb, 0)), pl.BlockSpec(memory_space=pltpu.SMEM)]
        dec_out = [pl.BlockSpec((T, D_QKV), lambda i: (i, 0))] * 2
        dec_shapes = [part, part]
        shift_args += [q_dec, relb]
        assert T % 8 == 0
        widths = [_decode_keys(c.shape[0] // steps // KV_ROWS, d, T) + HEAD_DIM for c, d in zip(caches, DILATIONS)]
        for w in widths:
            dec_scratch += [pltpu.VMEM((w, D_GRP), BF16)] * 2
        dec_scratch += [pltpu.VMEM((spb, HEADS_PER_GROUP * T, w), F32) for w in widths]
        dec_scratch += [pltpu.VMEM((f.shape[0] // n_cache_batch, HEAD_DIM), F32) for f in fresh]
        dec_scratch += [pltpu.SemaphoreType.DMA((2 * len(caches),))]
    return pl.pallas_call(
        functools.partial(_lru_body, B=B, tt=tt, batch_rows=tuple(batch_rows), spb=spb),
        grid=(steps,),
        in_specs=[
            pl.BlockSpec((B, tt, D_MODEL), lambda i: (0, i, 0)),
            const((1, D_MODEL)), const((D_MODEL, 2 * D_LRU)), const((CONV_WIDTH, D_LRU)), const((1, D_LRU)),
            const((N_LRU_BLOCKS, LRU_BLOCK, LRU_BLOCK)), const((1, D_LRU)),
            const((N_LRU_BLOCKS, LRU_BLOCK, LRU_BLOCK)), const((1, D_LRU)), const((1, D_LRU)),
            const((D_LRU, D_MODEL)), const((hist, D_LRU)), const((B, D_LRU)),
        ] + shift_in + dec_in,
        out_specs=[
            pl.BlockSpec((B, tt, D_MODEL), lambda i: (0, i, 0)),
            pl.BlockSpec((hist, D_LRU), lambda i: (0, 0)),
            pl.BlockSpec((B, D_LRU), lambda i: (0, 0)),
        ] + shift_out + dec_out,
        out_shape=[
            jax.ShapeDtypeStruct((B, S, D_MODEL), F32),
            jax.ShapeDtypeStruct((hist, D_LRU), F32),
            jax.ShapeDtypeStruct((B, D_LRU), F32),
        ] + shift_shapes + dec_shapes,
        scratch_shapes=[
            pltpu.VMEM((hist + R, D_LRU), F32),
            pltpu.VMEM((R, D_LRU), F32),
            pltpu.VMEM((R, D_LRU), F32),
            pltpu.VMEM((R, D_LRU), F32),
            pltpu.VMEM((B, D_LRU), F32),
        ] + dec_scratch,
        compiler_params=pltpu.CompilerParams(dimension_semantics=("arbitrary",), vmem_limit_bytes=VMEM_LIMIT),
        name="lru_branch",
    )(x, gn, w_lru, cw, cb, wa, ba, wx, bx, lam, wp, conv0, h0, *shift_args)


def _attn_body(x_ref, ya_ref, gn_ref, wqkv_ref, wg_ref, gq_ref, gk_ref, bm_ref, relb_ref, wap_ref, wout_ref,
               y_ref, kv0_ref, kv1_ref, kv2_ref,
               zq_ref, zk_ref, zv_ref, qc0, qc1, qc2, hk0, hv0, hk1, hv1, hk2, hv2,
               o0, o1, o2, l0, l1, l2, bt0, bt1, bt2, s_ref, p_ref, *, S, tq):
    t = pl.program_id(1)
    nt = S // tq

    qcs, hks, hvs = (qc0, qc1, qc2), (hk0, hk1, hk2), (hv0, hv1, hv2)
    o_refs, l_refs, bts, kvs = (o0, o1, o2), (l0, l1, l2), (bt0, bt1, bt2), (kv0_ref, kv1_ref, kv2_ref)
    Qc = tuple(tq // d for d in DILATIONS)
    QB = tuple(min(q, N_BACK) for q in Qc)
    pad1 = KEY_WIN - QB[1]
    n2 = S // DILATIONS[2]

    @pl.when((pl.program_id(0) == 0) & (t == 0))
    def _():
        for ref in (hk0, hv0, hk1, hv1, hk2, hv2):
            ref[...] = jnp.zeros(ref.shape, ref.dtype)
        for g in (0, 1):
            a = lax.broadcasted_iota(jnp.int32, (QB[g], KEY_WIN), 0)
            c = lax.broadcasted_iota(jnp.int32, (QB[g], KEY_WIN), 1)
            j = a + (KEY_WIN - QB[g]) - c
            for h in range(HEADS_PER_GROUP):
                bts[g][h] = _bias_table(relb_ref, g * HEADS_PER_GROUP + h, j * DILATIONS[g], (j >= 0) & (j <= N_BACK))
        i = lax.broadcasted_iota(jnp.int32, (n2, n2), 0)
        c = lax.broadcasted_iota(jnp.int32, (n2, n2), 1)
        for h in range(HEADS_PER_GROUP):
            bt2[h] = _bias_table(relb_ref, 2 * HEADS_PER_GROUP + h, (i - c) * DILATIONS[2],
                                 (i - c >= 0) & (i - c <= N_BACK))

    x = x_ref[...]
    u = _rms(x, gn_ref[...]).astype(BF16)
    z_refs = (zq_ref, zk_ref, zv_ref)

    def store(part, head, value):
        z_refs[part][head] = value

    _qkv(u, wqkv_ref, gq_ref, gk_ref, store)

    for g in range(N_GROUPS):
        keep = min(WINDOWS[g], S)
        rb = min(tq, keep)

        @pl.when(t >= nt - keep // rb)
        def _(g=g, rb=rb):
            for part, z_ref in enumerate((zk_ref, zv_ref)):
                for h in range(HEADS_PER_GROUP):
                    dst = pl.ds(part * HEADS_PER_GROUP + h, rb, stride=KV_ROWS)
                    kvs[g][dst, :] = z_ref[g * HEADS_PER_GROUP + h, tq - rb:tq, :]

    for g in range(N_GROUPS):
        d = DILATIONS[g]
        for r in range(d):
            rows = pl.ds(r, Qc[g], stride=d) if d > 1 else slice(None)
            if g == 0:
                dst = pl.ds(tq, tq)
            elif g == 1:
                dst = pl.ds(pl.multiple_of(pad1 + t * Qc[1], Qc[1]), Qc[1])
            else:
                dst = pl.ds(pl.multiple_of(t * Qc[2], Qc[2]), Qc[2])
            for h in range(HEADS_PER_GROUP):
                hs_ = slice(h * HEAD_DIM, (h + 1) * HEAD_DIM)
                gh = g * HEADS_PER_GROUP + h
                qcs[g][r, :, hs_] = zq_ref[gh, rows, :].astype(BF16)
                hks[g][r, dst, hs_] = zk_ref[gh, rows, :].astype(BF16)
                hvs[g][r, dst, hs_] = zv_ref[gh, rows, :].astype(BF16)

    col = lax.broadcasted_iota(jnp.int32, (1, KEY_WIN), 1)
    rowi = lax.broadcasted_iota(jnp.int32, (tq, 1), 0)
    for g in range(N_GROUPS):
        d = DILATIONS[g]
        nq = Qc[g] // QB[g]
        kw = n2 if g == 2 else KEY_WIN
        units = [(r, qb) for r in range(d) for qb in range(nq)]

        def key_win(qb, g=g):
            if g == 0:
                return pl.ds(tq + qb * QB[0] + QB[0] - KEY_WIN, KEY_WIN)
            if g == 1:
                return pl.ds(pl.multiple_of(t * Qc[1], QB[1]), KEY_WIN)
            return slice(None)

        for r, qb in units:
            cm = slice(r * Qc[g] + qb * QB[g], r * Qc[g] + (qb + 1) * QB[g])
            for h in range(HEADS_PER_GROUP):
                hs_ = slice(h * HEAD_DIM, (h + 1) * HEAD_DIM)
                s_ref[h, cm, 0:kw] = _dot_nt(qcs[g][r, qb * QB[g]:(qb + 1) * QB[g], hs_], hks[g][r, key_win(qb), hs_])

        if g < 2:
            i0 = t * Qc[g] + (rowi & (Qc[g] - QB[g]))
            started = col >= KEY_WIN - QB[g] - i0
        for h in range(HEADS_PER_GROUP):
            bias = bt2[h, pl.ds(pl.multiple_of(t * QB[2], QB[2]), QB[2]), :] if g == 2 else bts[g][h]
            s = s_ref[h, :, 0:kw] * ATTN_SCALE
            s = (s.reshape(len(units), QB[g], kw) + bias[None]).reshape(tq, kw)
            if g < 2:
                s = jnp.where(started, s, NEG)
            m = jnp.max(s, axis=-1, keepdims=True)
            p = jnp.exp(s - m)
            den = jnp.sum(p, axis=-1, keepdims=True)
            p_ref[h, :, 0:kw] = (p * (1.0 / den)).astype(BF16)
            lse = jnp.broadcast_to(m + jnp.log(den), (tq, HEAD_DIM))
            for r in range(d):
                rows = pl.ds(r, Qc[g], stride=d) if d > 1 else slice(None)
                l_refs[g][h, rows, :] = lse[r * Qc[g]:(r + 1) * Qc[g]]

        for r, qb in units:
            cm = slice(r * Qc[g] + qb * QB[g], r * Qc[g] + (qb + 1) * QB[g])
            rows = pl.ds(r + d * qb * QB[g], QB[g], stride=d) if d > 1 else pl.ds(qb * QB[g], QB[g])
            for h in range(HEADS_PER_GROUP):
                hs_ = slice(h * HEAD_DIM, (h + 1) * HEAD_DIM)
                o_refs[g][h, rows, :] = _dot(p_ref[h, cm, 0:kw], hvs[g][r, key_win(qb), hs_])

    hk0[0, 0:tq, :] = hk0[0, tq:2 * tq, :]
    hv0[0, 0:tq, :] = hv0[0, tq:2 * tq, :]

    ob = _merge_groups(o_refs, l_refs)
    ob = jnp.concatenate([ob[h] for h in range(HEADS_PER_GROUP)], axis=1)
    gate = jnp.concatenate([_gate_chunk(u, wg_ref, bm_ref, c) for c in range(N_GATE_CHUNKS)], axis=1)
    y_ref[...] = _tail(x, ya_ref[...].astype(F32), ob, gate, wap_ref, wout_ref)


def _attn_call(x, ya, gn, wqkv, wg, gq, gk, bmerge, relb, wap, wout, *, tq):
    B, S, _ = x.shape
    nt = S // tq
    assert S % tq == 0 and tq % (DILATIONS[2] * 16) == 0 and tq >= N_BACK
    assert S // DILATIONS[2] == N_BACK and tq // DILATIONS[1] <= N_BACK
    Qc = tuple(tq // d for d in DILATIONS)
    QB = tuple(min(q, N_BACK) for q in Qc)
    n2 = S // DILATIONS[2]
    const = lambda shape: pl.BlockSpec(shape, lambda b, t: (0,) * len(shape), pipeline_mode=pl.Buffered(1))
    row = pl.BlockSpec((None, tq, D_MODEL), lambda b, t: (b, t, 0))

    def kv_spec(g):
        keep = min(WINDOWS[g], S)
        rb = min(tq, keep)
        first = nt - keep // rb
        return pl.BlockSpec((None, rb * KV_ROWS, HEAD_DIM), lambda b, t: (b, jnp.maximum(t - first, 0), 0))

    cls = lambda g, rows, dt: pltpu.VMEM((DILATIONS[g], rows, D_GRP), dt)
    scratch = [pltpu.VMEM((D_QKV // HEAD_DIM, tq, HEAD_DIM), F32)] * 3
    scratch += [cls(g, Qc[g], BF16) for g in range(N_GROUPS)]
    scratch += [cls(0, 2 * tq, BF16)] * 2 + [cls(1, KEY_WIN - QB[1] + S // DILATIONS[1], BF16)] * 2 + [cls(2, n2, BF16)] * 2
    scratch += [pltpu.VMEM((HEADS_PER_GROUP, tq, HEAD_DIM), F32)] * 6
    scratch += [pltpu.VMEM((HEADS_PER_GROUP, QB[0], KEY_WIN), F32), pltpu.VMEM((HEADS_PER_GROUP, QB[1], KEY_WIN), F32),
                pltpu.VMEM((HEADS_PER_GROUP, n2, n2), F32)]
    scratch += [pltpu.VMEM((HEADS_PER_GROUP, tq, KEY_WIN), F32), pltpu.VMEM((HEADS_PER_GROUP, tq, KEY_WIN), BF16)]
    return pl.pallas_call(
        functools.partial(_attn_body, S=S, tq=tq),
        grid=(B, nt),
        in_specs=[row, row, const((1, D_MODEL)), const((D_MODEL, 3 * D_QKV)), const((D_MODEL, D_GRP + 2 * D_MODEL)),
                  const((1, HEAD_DIM)), const((1, HEAD_DIM)), const((1, 2 * D_MODEL)),
                  pl.BlockSpec(memory_space=pltpu.SMEM), const((D_GRP, D_MODEL)), const((D_MODEL, D_MODEL))],
        out_specs=[row, kv_spec(0), kv_spec(1), kv_spec(2)],
        out_shape=[jax.ShapeDtypeStruct((B, S, D_MODEL), F32)]
        + [jax.ShapeDtypeStruct((B, min(WINDOWS[g], S) * KV_ROWS, HEAD_DIM), F32) for g in range(N_GROUPS)],
        scratch_shapes=scratch,
        compiler_params=pltpu.CompilerParams(dimension_semantics=("arbitrary", "arbitrary"), vmem_limit_bytes=VMEM_LIMIT),
        name="attn_branch",
    )(x, ya, gn, wqkv, wg, gq, gk, bmerge, relb, wap, wout)


def _qkv_body(x_ref, gn_ref, wqkv_ref, gq_ref, gk_ref, zq_ref, nr0_ref, nr1_ref, nr2_ref):
    M = x_ref.shape[0]
    u = _rms(x_ref[...], gn_ref[...]).astype(BF16)
    nr_refs = (nr0_ref, nr1_ref, nr2_ref)

    def store(part, head, value):
        if part == 0:
            zq_ref[:, head * HEAD_DIM:(head + 1) * HEAD_DIM] = value
        else:
            g, h = divmod(head, HEADS_PER_GROUP)
            nr_refs[g][pl.ds((part - 1) * HEADS_PER_GROUP + h, M, stride=KV_ROWS), :] = value

    _qkv(u, wqkv_ref, gq_ref, gk_ref, store)


def _qkv_call(x, gn, wqkv, gq, gk):
    M = x.shape[0]
    rows = jax.ShapeDtypeStruct((M * KV_ROWS, HEAD_DIM), F32)
    return pl.pallas_call(_qkv_body, out_shape=[jax.ShapeDtypeStruct((M, D_QKV), F32), rows, rows, rows],
                          compiler_params=pltpu.CompilerParams(vmem_limit_bytes=VMEM_LIMIT),
                          name="decode_qkv")(x, gn, wqkv, gq, gk)


def _tail_body(x_ref, ya_ref, gn_ref, wg_ref, bm_ref, wap_ref, wout_ref, o_ref, lse_ref, y_ref):
    x = x_ref[...]
    u = _rms(x, gn_ref[...]).astype(BF16)
    pieces = [(j, slice(g * D_GRP, (g + 1) * D_GRP)) for j in range(o_ref.shape[0]) for g in range(N_GROUPS)]
    ob = _merge_groups([o_ref.at[j, :, gs] for j, gs in pieces], [lse_ref.at[j, :, gs] for j, gs in pieces])
    gate = jnp.concatenate([_gate_chunk(u, wg_ref, bm_ref, c) for c in range(N_GATE_CHUNKS)], axis=1)
    y_ref[...] = _tail(x, ya_ref[...], ob, gate, wap_ref, wout_ref)


def _tail_call(x, ya, gn, wg, bmerge, wap, wout, o, lse):
    return pl.pallas_call(_tail_body, out_shape=jax.ShapeDtypeStruct(x.shape, F32),
                          compiler_params=pltpu.CompilerParams(vmem_limit_bytes=VMEM_LIMIT),
                          name="decode_tail")(x, ya, gn, wg, bmerge, wap, wout, o, lse)


def kernel(x_prompt, x_sample, cache_kv_w128, cache_kv_w512, cache_kv_w2048, state_conv, state_h, g_norm, w_in,
           b_merge, conv_w, conv_b, lru_w_a, lru_b_a, lru_w_x, lru_b_x, lru_lambda, g_q, g_k, rel_bias,
           w_lru_proj, w_attn_proj, w_out):
    assert w_in.shape[0] == 1, "single-layer step"
    B, S, _ = x_prompt.shape
    DB, T, _ = x_sample.shape
    o2 = 2 * D_LRU
    o5 = o2 + 3 * D_QKV
    w_lru = w_in[0, :, :o2].astype(BF16)
    wqkv = w_in[0, :, o2:o5].astype(BF16)
    wg = w_in[0, :, o5:].astype(BF16)
    wa, wx = lru_w_a[0].astype(BF16), lru_w_x[0].astype(BF16)
    wp, wap, wout = w_lru_proj[0].astype(BF16), w_attn_proj[0].astype(BF16), w_out[0].astype(BF16)
    lru_params = (g_norm, w_lru, conv_w[0], conv_b, wa, lru_b_a, wx, lru_b_x, lru_lambda, wp)
    hist = CONV_WIDTH - 1

    caches6 = (cache_kv_w128, cache_kv_w512, cache_kv_w2048)
    caches = [c.reshape(-1, HEAD_DIM) for c in caches6]

    xs2 = x_sample.reshape(DB * T, D_MODEL)
    zq, *newrows = _qkv_call(xs2, g_norm, wqkv, g_q, g_k)

    ya_p, conv_p, h_p, *rest = _lru_call(x_prompt, jnp.zeros((hist * B, D_LRU), F32), jnp.zeros((B, D_LRU), F32),
                                         *lru_params, tt=LRU_TIME_TILE, caches=caches, fresh=newrows,
                                         n_cache_batch=DB, q_dec=zq, relb=rel_bias)
    news, (o_parts, lse_parts) = rest[:N_GROUPS], rest[N_GROUPS:]
    y_p, kv0_p, kv1_p, kv2_p = _attn_call(x_prompt, ya_p, g_norm, wqkv, wg, g_q, g_k, b_merge, rel_bias, wap, wout,
                                          tq=ATTN_ROW_TILE)

    conv0_s = jnp.swapaxes(state_conv[0], 0, 1).reshape(hist * DB, D_LRU)
    ya_s, conv_s, h_s = _lru_call(x_sample, conv0_s, state_h[0], *lru_params, tt=T)
    spb = S // LRU_TIME_TILE // DB
    by_part = lambda a: jnp.swapaxes(a.reshape(DB, spb, T, D_QKV), 0, 1).reshape(spb, DB * T, D_QKV)
    y_s = _tail_call(xs2, ya_s.reshape(DB * T, D_MODEL), g_norm, wg, b_merge, wap, wout,
                     by_part(o_parts), by_part(lse_parts))

    kv_shape = lambda a: a.reshape(1, a.shape[0], a.shape[1] // KV_ROWS, 2, HEADS_PER_GROUP, HEAD_DIM)
    conv_out = lambda c, nb: jnp.swapaxes(c.reshape(hist, nb, D_LRU), 0, 1)[None]
    news = [n.reshape(c.shape) for n, c in zip(news, caches6)]
    return (y_p, y_s.reshape(DB, T, D_MODEL), kv_shape(kv0_p), kv_shape(kv1_p), kv_shape(kv2_p),
            conv_out(conv_p, B), h_p[None], news[0], news[1], news[2], conv_out(conv_s, DB), h_s[None])
```

```python
import functools
import math

import jax
import jax.numpy as jnp
from jax import lax
from jax.experimental import pallas as pl
from jax.experimental.pallas import tpu as pltpu

F32 = jnp.float32
BF16 = jnp.bfloat16

D_MODEL = 1024
D_LRU = 1024
N_LRU_BLOCKS = 8
LRU_BLOCK = D_LRU // N_LRU_BLOCKS
CONV_WIDTH = 4
LRU_C = 8.0
HEAD_DIM = 128
HEADS_PER_GROUP = 4
WINDOWS = (128, 512, 2048)
DILATIONS = (1, 4, 16)
N_GROUPS = 3
N_BACK = 128
D_QKV = N_GROUPS * HEADS_PER_GROUP * HEAD_DIM
D_GRP = HEADS_PER_GROUP * HEAD_DIM
ATTN_SCALE = HEAD_DIM ** -0.5
N_BUCKETS = 32
MAX_DISTANCE = 2048
NORM_EPS = 1e-6
NEG = -1e30
KEY_WIN = 256
KV_ROWS = 2 * HEADS_PER_GROUP
VMEM_LIMIT = 60000 * 1024
LRU_TIME_TILE = 64
ATTN_ROW_TILE = 256


def _rms(x, g):
    ms = jnp.mean(x * x, axis=-1, keepdims=True)
    return x * lax.rsqrt(ms + NORM_EPS) * g


def _sigmoid(x):
    return 0.5 * jnp.tanh(0.5 * x) + 0.5


def _dot(a, b):
    return jnp.dot(a, b, preferred_element_type=F32)


def _dot_nt(a, b):
    return lax.dot_general(a, b, (((1,), (1,)), ((), ())), preferred_element_type=F32)


def _bias_table(relb_ref, col, dist, valid):
    max_exact = N_BUCKETS // 2
    n_log = N_BUCKETS - max_exact
    df = jnp.maximum(dist, 1).astype(F32)
    val = jnp.log(df / max_exact) / math.log(MAX_DISTANCE / max_exact) * n_log
    out = jnp.zeros(dist.shape, F32)
    for b in range(max_exact):
        out = jnp.where(dist == b, relb_ref[b, col], out)
    for k in range(n_log):
        out = jnp.where((dist >= max_exact) & (val >= k), relb_ref[max_exact + k, col], out)
    return jnp.where(valid, out, NEG)


def _merge_groups(o_refs, lse_refs):
    lses = [r[...] for r in lse_refs]
    m = functools.reduce(jnp.maximum, lses)
    es = [jnp.exp(l - m) for l in lses]
    num = sum(e * r[...] for e, r in zip(es, o_refs))
    return num / sum(es)


N_GATE_CHUNKS = (D_GRP + 2 * D_MODEL) // D_GRP


def _gate_chunk(u, wg_ref, bm_ref, c):
    z = _dot(u, wg_ref[:, c * D_GRP:(c + 1) * D_GRP])
    if c == 0:
        return z * _sigmoid(z)
    return _sigmoid(z + bm_ref[:, (c - 1) * D_GRP:c * D_GRP])


def _tail(x, ya, ob, gate, wap_ref, wout_ref):
    yb = _dot((ob * gate[:, :D_GRP]).astype(BF16), wap_ref[...])
    merged = gate[:, D_GRP:D_GRP + D_MODEL] * ya + gate[:, D_GRP + D_MODEL:] * yb
    return x + _dot(merged.astype(BF16), wout_ref[...])


def _qkv(u, wqkv_ref, gq_ref, gk_ref, store):
    for part, g_ref in enumerate((gq_ref, gk_ref, None)):
        z = _dot(u, wqkv_ref[:, part * D_QKV:(part + 1) * D_QKV])
        zs = [z[:, h * HEAD_DIM:(h + 1) * HEAD_DIM] for h in range(D_QKV // HEAD_DIM)]
        if g_ref is not None:
            ms = [jnp.mean(zh * zh, axis=-1, keepdims=True) for zh in zs]
            zs = [zh * lax.rsqrt(m + NORM_EPS) * g_ref[...] for zh, m in zip(zs, ms)]
        for h, zh in enumerate(zs):
            store(part, h, zh)


N_LRU_IN = 13


def _shift_cache_block(blk_ref, next_ref, fresh_ref, out_hbm, tail_ref, body_sem, tail_sem, batch_rows):
    rpb = blk_ref.shape[0]
    hop = fresh_ref.shape[0]
    r0 = pl.multiple_of(pl.program_id(0) * rpb, rpb)
    ends_batch = ((pl.program_id(0) + 1) * rpb) % batch_rows == 0
    tail_ref[...] = jnp.where(ends_batch, fresh_ref[...], next_ref[...])
    return (pltpu.make_async_copy(blk_ref.at[pl.ds(hop, rpb - hop), :], out_hbm.at[pl.ds(r0, rpb - hop), :], body_sem),
            pltpu.make_async_copy(tail_ref, out_hbm.at[pl.ds(r0 + rpb - hop, hop), :], tail_sem))


def _decode_keys(positions, d, T):
    return positions // d * T if d >= 2 * T else positions


def _decode_attn_part(q_ref, relb_ref, blk_refs, fresh_refs, o_ref, lse_ref, kcs, vcs, bts, spb):
    T = q_ref.shape[0]
    assert T & (T - 1) == 0
    part = pl.program_id(0) % spb
    Lhs = tuple(blk.shape[0] // KV_ROWS for blk in blk_refs)
    Ks = tuple(_decode_keys(Lh, d, T) for Lh, d in zip(Lhs, DILATIONS))

    @pl.when(pl.program_id(0) == 0)
    def _():
        for g in range(N_GROUPS):
            Lh, K, d = Lhs[g], Ks[g], DILATIONS[g]
            L, W = Lh * spb, K + HEAD_DIM
            kcs[g][K:W, :] = jnp.zeros((W - K, D_GRP), BF16)
            vcs[g][K:W, :] = jnp.zeros((W - K, D_GRP), BF16)
            tq = lax.broadcasted_iota(jnp.int32, (T, W), 0)
            c = lax.broadcasted_iota(jnp.int32, (T, W), 1)
            within = c if K == Lh else (c >> (T.bit_length() - 1)) * d + (c & (T - 1))
            for j in range(spb):
                p = jnp.where(c < K, j * Lh + within, L + c - K)
                present = (c < K) | ((c < K + T) & (j == spb - 1))
                dist = L + tq - p
                valid = present & (dist >= 0) & ((dist & (d - 1)) == 0) & (dist <= N_BACK * d)
                for h in range(HEADS_PER_GROUP):
                    bts[g][j, h * T:(h + 1) * T, :] = _bias_table(relb_ref, g * HEADS_PER_GROUP + h, dist, valid)

    heads = [slice(h * HEAD_DIM, (h + 1) * HEAD_DIM) for h in range(HEADS_PER_GROUP)]

    def gather(g):
        Lh, K, d = Lhs[g], Ks[g], DILATIONS[g]
        if K == Lh:
            rows_of = lambda r: blk_refs[g][pl.ds(r, Lh, stride=KV_ROWS), :]
        else:
            periods = blk_refs[g].reshape(Lh // d, d * KV_ROWS, HEAD_DIM)
            rows_of = lambda r: periods[:, pl.ds(r, T, stride=KV_ROWS), :].reshape(K, HEAD_DIM)
        for h, hs_ in enumerate(heads):
            kcs[g][0:K, hs_] = rows_of(h).astype(BF16)
            vcs[g][0:K, hs_] = rows_of(HEADS_PER_GROUP + h).astype(BF16)
        k_new = jnp.concatenate([fresh_refs[g][pl.ds(h, T, stride=KV_ROWS), :] for h in range(HEADS_PER_GROUP)], axis=1)
        v_new = jnp.concatenate([fresh_refs[g][pl.ds(HEADS_PER_GROUP + h, T, stride=KV_ROWS), :]
                                 for h in range(HEADS_PER_GROUP)], axis=1)
        kcs[g][K:K + 2 * T, :] = jnp.concatenate([k_new, jnp.zeros_like(k_new)], axis=0).astype(BF16)
        vcs[g][K:K + 2 * T, :] = jnp.concatenate([v_new, jnp.zeros_like(v_new)], axis=0).astype(BF16)

    def attend(g):
        cols = [slice(g * D_GRP + h * HEAD_DIM, g * D_GRP + (h + 1) * HEAD_DIM) for h in range(HEADS_PER_GROUP)]
        s = jnp.concatenate([_dot_nt(q_ref[:, cs].astype(BF16), kcs[g][:, hs_]) for cs, hs_ in zip(cols, heads)], axis=0)
        s = s * ATTN_SCALE + bts[g][part]
        m = jnp.max(s, axis=-1, keepdims=True)
        p = jnp.exp(s - m)
        den = jnp.sum(p, axis=-1, keepdims=True)
        p = p * (1.0 / den)
        lse = m + jnp.log(den)
        for h, (cs, hs_) in enumerate(zip(cols, heads)):
            rows = slice(h * T, (h + 1) * T)
            o_ref[:, cs] = _dot(p[rows].astype(BF16), vcs[g][:, hs_])
            lse_ref[:, cs] = jnp.broadcast_to(lse[rows], (T, HEAD_DIM))

    return [functools.partial(f, g) for g in range(N_GROUPS) for f in (gather, attend)]


def _lru_body(*refs, B, tt, batch_rows, spb):
    (x_ref, gn_ref, w_ref, cw_ref, cb_ref, wa_ref, ba_ref, wx_ref, bx_ref, lam_ref, wp_ref,
     conv0_ref, h0_ref) = refs[:N_LRU_IN]
    n_shift = len(batch_rows)
    n_dec = 2 if n_shift else 0
    i0 = N_LRU_IN
    shift_in, i0 = refs[i0:i0 + 3 * n_shift], i0 + 3 * n_shift
    dec_in, i0 = refs[i0:i0 + n_dec], i0 + n_dec
    (ya_ref, convo_ref, ho_ref), i0 = refs[i0:i0 + 3], i0 + 3
    shift_out, i0 = refs[i0:i0 + n_shift], i0 + n_shift
    dec_out, i0 = refs[i0:i0 + n_dec], i0 + n_dec
    (xs_ref, ga_ref, a_ref, hs_ref, h_ref, ya_stage), i0 = refs[i0:i0 + 6], i0 + 6
    dec_scratch = refs[i0:]
    R = B * tt
    hist = (CONV_WIDTH - 1) * B

    @pl.when(pl.program_id(0) == 0)
    def _():
        xs_ref[0:hist, :] = conv0_ref[...]
        h_ref[...] = h0_ref[...]

    shift_copies, decode_stages = [], []
    if n_shift:
        kcs, vcs = dec_scratch[0:2 * n_shift:2], dec_scratch[1:2 * n_shift:2]
        bts, tails, sem = dec_scratch[2 * n_shift:3 * n_shift], dec_scratch[3 * n_shift:4 * n_shift], dec_scratch[-1]
        for k in range(n_shift):
            shift_copies += _shift_cache_block(*shift_in[3 * k:3 * k + 3], shift_out[k], tails[k], sem.at[2 * k],
                                               sem.at[2 * k + 1], batch_rows[k])
        decode_stages = _decode_attn_part(*dec_in, shift_in[0::3], shift_in[2::3], *dec_out, kcs, vcs, bts, spb)
    for cp in shift_copies:
        cp.start()
    assert len(decode_stages) <= N_LRU_BLOCKS

    xt = jnp.concatenate([x_ref[:, t, :] for t in range(tt)], axis=0)
    u = _rms(xt, gn_ref[...]).astype(BF16)
    xs_ref[hist:hist + R, :] = _dot(u, w_ref[:, :D_LRU])
    ga_ref[...] = _dot(u, w_ref[:, D_LRU:])
    sp = jax.nn.softplus(-lam_ref[...])
    for n in range(N_LRU_BLOCKS):
        if n < len(decode_stages):
            decode_stages[n]()
        cs = slice(n * LRU_BLOCK, (n + 1) * LRU_BLOCK)
        y = cb_ref[:, cs] + sum(xs_ref[j * B:j * B + R, cs] * cw_ref[j:j + 1, cs] for j in range(CONV_WIDTH))
        yb = y.astype(BF16)
        r = _sigmoid(_dot(yb, wa_ref[n]) + ba_ref[:, cs])
        i = _sigmoid(_dot(yb, wx_ref[n]) + bx_ref[:, cs])
        log_a = -LRU_C * r * sp[:, cs]
        a = jnp.exp(log_a)
        th = jnp.tanh(log_a)
        a_ref[:, cs] = a
        hs_ref[:, cs] = jnp.sqrt(-2.0 * th) * lax.rsqrt(1.0 - th) * (i * y)
    h = h_ref[...]
    for t in range(tt):
        rows = slice(t * B, (t + 1) * B)
        h = a_ref[rows, :] * h + hs_ref[rows, :]
        hs_ref[rows, :] = h
    h_ref[...] = h
    g = ga_ref[...]
    ya = _dot((hs_ref[...] * (g * _sigmoid(g))).astype(BF16), wp_ref[...])
    for t in range(tt):
        ya_stage[:, t, :] = ya[t * B:(t + 1) * B]
    ya_ref[...] = ya_stage[...].astype(ya_ref.dtype)
    tail = xs_ref[R:R + hist, :]
    convo_ref[...] = tail
    xs_ref[0:hist, :] = tail
    ho_ref[...] = h_ref[...]
    for cp in shift_copies:
        cp.wait()


def _lru_call(x, conv0, h0, gn, w_lru, cw, cb, wa, ba, wx, bx, lam, wp, *, tt, caches=(), fresh=(), n_cache_batch=1,
              q_dec=None, relb=None):
    B, S, _ = x.shape
    assert S % tt == 0 and tt >= CONV_WIDTH - 1 and tt % 8 == 0 and B % 8 == 0
    R = B * tt
    hist = (CONV_WIDTH - 1) * B
    steps = S // tt
    spb = steps // n_cache_batch
    const = lambda shape: pl.BlockSpec(shape, lambda i: (0,) * len(shape), pipeline_mode=pl.Buffered(1))
    shift_in, shift_out, shift_shapes, batch_rows = [], [], [], []
    for c, f in zip(caches, fresh):
        total = c.shape[0]
        rpb = total // steps
        rows_b = total // n_cache_batch
        hop = f.shape[0] // n_cache_batch
        assert total % steps == 0 and rows_b == spb * rpb and rpb % hop == 0 and rpb > hop
        shift_in += [
            pl.BlockSpec((rpb, HEAD_DIM), lambda i: (i, 0)),
            pl.BlockSpec((hop, HEAD_DIM), lambda i, rpb=rpb, hop=hop, total=total:
                         (jnp.minimum((i + 1) * (rpb // hop), total // hop - 1), 0)),
            pl.BlockSpec((hop, HEAD_DIM), lambda i, rpb=rpb, rows_b=rows_b: ((i * rpb) // rows_b, 0)),
        ]
        shift_out.append(pl.BlockSpec(memory_space=pl.ANY))
        shift_shapes.append(jax.ShapeDtypeStruct(c.shape, c.dtype))
        batch_rows.append(rows_b)
    shift_args = [a for c, f in zip(caches, fresh) for a in (c, c, f)]
    dec_in, dec_out, dec_shapes, dec_scratch = [], [], [], []
    if caches:
        assert len(caches) == N_GROUPS and q_dec.shape[0] % n_cache_batch == 0
        T = q_dec.shape[0] // n_cache_batch
        part = jax.ShapeDtypeStruct((steps * T, D_QKV), F32)
        dec_in = [pl.BlockSpec((T, D_QKV), lambda i: (i // spb, 0)), pl.BlockSpec(memory_space=pltpu.SMEM)]
        dec_out = [pl.BlockSpec((T, D_QKV), lambda i: (i, 0))] * 2
        dec_shapes = [part, part]
        shift_args += [q_dec, relb]
        assert T % 8 == 0
        widths = [_decode_keys(c.shape[0] // steps // KV_ROWS, d, T) + HEAD_DIM for c, d in zip(caches, DILATIONS)]
        for w in widths:
            dec_scratch += [pltpu.VMEM((w, D_GRP), BF16)] * 2
        dec_scratch += [pltpu.VMEM((spb, HEADS_PER_GROUP * T, w), F32) for w in widths]
        dec_scratch += [pltpu.VMEM((f.shape[0] // n_cache_batch, HEAD_DIM), F32) for f in fresh]
        dec_scratch += [pltpu.SemaphoreType.DMA((2 * len(caches),))]
    return pl.pallas_call(
        functools.partial(_lru_body, B=B, tt=tt, batch_rows=tuple(batch_rows), spb=spb),
        grid=(steps,),
        in_specs=[
            pl.BlockSpec((B, tt, D_MODEL), lambda i: (0, i, 0)),
            const((1, D_MODEL)), const((D_MODEL, 2 * D_LRU)), const((CONV_WIDTH, D_LRU)), const((1, D_LRU)),
            const((N_LRU_BLOCKS, LRU_BLOCK, LRU_BLOCK)), const((1, D_LRU)),
            const((N_LRU_BLOCKS, LRU_BLOCK, LRU_BLOCK)), const((1, D_LRU)), const((1, D_LRU)),
            const((D_LRU, D_MODEL)), const((hist, D_LRU)), const((B, D_LRU)),
        ] + shift_in + dec_in,
        out_specs=[
            pl.BlockSpec((B, tt, D_MODEL), lambda i: (0, i, 0)),
            pl.BlockSpec((hist, D_LRU), lambda i: (0, 0)),
            pl.BlockSpec((B, D_LRU), lambda i: (0, 0)),
        ] + shift_out + dec_out,
        out_shape=[
            jax.ShapeDtypeStruct((B, S, D_MODEL), BF16),
            jax.ShapeDtypeStruct((hist, D_LRU), F32),
            jax.ShapeDtypeStruct((B, D_LRU), F32),
        ] + shift_shapes + dec_shapes,
        scratch_shapes=[
            pltpu.VMEM((hist + R, D_LRU), F32),
            pltpu.VMEM((R, D_LRU), F32),
            pltpu.VMEM((R, D_LRU), F32),
            pltpu.VMEM((R, D_LRU), F32),
            pltpu.VMEM((B, D_LRU), F32),
            pltpu.VMEM((B, tt, D_MODEL), F32),
        ] + dec_scratch,
        compiler_params=pltpu.CompilerParams(dimension_semantics=("arbitrary",), vmem_limit_bytes=VMEM_LIMIT),
        name="lru_branch",
    )(x, gn, w_lru, cw, cb, wa, ba, wx, bx, lam, wp, conv0, h0, *shift_args)


def _attn_body(x_ref, ya_ref, gn_ref, wqkv_ref, wg_ref, gq_ref, gk_ref, bm_ref, relb_ref, wap_ref, wout_ref,
               y_ref, kv0_ref, kv1_ref, kv2_ref,
               zq_ref, zk_ref, zv_ref, qc0, qc1, qc2, hk0, hv0, hk1, hv1, hk2, hv2,
               o0, o1, o2, l0, l1, l2, bt0, bt1, bt2, s_ref, p_ref, *, S, tq):
    t = pl.program_id(1)
    nt = S // tq

    qcs, hks, hvs = (qc0, qc1, qc2), (hk0, hk1, hk2), (hv0, hv1, hv2)
    o_refs, l_refs, bts, kvs = (o0, o1, o2), (l0, l1, l2), (bt0, bt1, bt2), (kv0_ref, kv1_ref, kv2_ref)
    Qc = tuple(tq // d for d in DILATIONS)
    QB = tuple(min(q, N_BACK) for q in Qc)
    pad1 = KEY_WIN - QB[1]
    n2 = S // DILATIONS[2]

    @pl.when((pl.program_id(0) == 0) & (t == 0))
    def _():
        for ref in (hk0, hv0, hk1, hv1, hk2, hv2):
            ref[...] = jnp.zeros(ref.shape, ref.dtype)
        for g in (0, 1):
            a = lax.broadcasted_iota(jnp.int32, (QB[g], KEY_WIN), 0)
            c = lax.broadcasted_iota(jnp.int32, (QB[g], KEY_WIN), 1)
            j = a + (KEY_WIN - QB[g]) - c
            for h in range(HEADS_PER_GROUP):
                bts[g][h] = _bias_table(relb_ref, g * HEADS_PER_GROUP + h, j * DILATIONS[g], (j >= 0) & (j <= N_BACK))
        i = lax.broadcasted_iota(jnp.int32, (n2, n2), 0)
        c = lax.broadcasted_iota(jnp.int32, (n2, n2), 1)
        for h in range(HEADS_PER_GROUP):
            bt2[h] = _bias_table(relb_ref, 2 * HEADS_PER_GROUP + h, (i - c) * DILATIONS[2],
                                 (i - c >= 0) & (i - c <= N_BACK))

    x = x_ref[...]
    u = _rms(x, gn_ref[...]).astype(BF16)
    z_refs = (zq_ref, zk_ref, zv_ref)

    def store(part, head, value):
        z_refs[part][head] = value

    _qkv(u, wqkv_ref, gq_ref, gk_ref, store)

    for g in range(N_GROUPS):
        keep = min(WINDOWS[g], S)
        rb = min(tq, keep)

        @pl.when(t >= nt - keep // rb)
        def _(g=g, rb=rb):
            for part, z_ref in enumerate((zk_ref, zv_ref)):
                for h in range(HEADS_PER_GROUP):
                    dst = pl.ds(part * HEADS_PER_GROUP + h, rb, stride=KV_ROWS)
                    kvs[g][dst, :] = z_ref[g * HEADS_PER_GROUP + h, tq - rb:tq, :]

    for g in range(N_GROUPS):
        d = DILATIONS[g]
        for r in range(d):
            rows = pl.ds(r, Qc[g], stride=d) if d > 1 else slice(None)
            if g == 0:
                dst = pl.ds(tq, tq)
            elif g == 1:
                dst = pl.ds(pl.multiple_of(pad1 + t * Qc[1], Qc[1]), Qc[1])
            else:
                dst = pl.ds(pl.multiple_of(t * Qc[2], Qc[2]), Qc[2])
            for h in range(HEADS_PER_GROUP):
                hs_ = slice(h * HEAD_DIM, (h + 1) * HEAD_DIM)
                gh = g * HEADS_PER_GROUP + h
                qcs[g][r, :, hs_] = zq_ref[gh, rows, :].astype(BF16)
                hks[g][r, dst, hs_] = zk_ref[gh, rows, :].astype(BF16)
                hvs[g][r, dst, hs_] = zv_ref[gh, rows, :].astype(BF16)

    col = lax.broadcasted_iota(jnp.int32, (1, KEY_WIN), 1)
    rowi = lax.broadcasted_iota(jnp.int32, (tq, 1), 0)
    for g in range(N_GROUPS):
        d = DILATIONS[g]
        nq = Qc[g] // QB[g]
        kw = n2 if g == 2 else KEY_WIN
        units = [(r, qb) for r in range(d) for qb in range(nq)]

        def key_win(qb, g=g):
            if g == 0:
                return pl.ds(tq + qb * QB[0] + QB[0] - KEY_WIN, KEY_WIN)
            if g == 1:
                return pl.ds(pl.multiple_of(t * Qc[1], QB[1]), KEY_WIN)
            return slice(None)

        for r, qb in units:
            cm = slice(r * Qc[g] + qb * QB[g], r * Qc[g] + (qb + 1) * QB[g])
            for h in range(HEADS_PER_GROUP):
                hs_ = slice(h * HEAD_DIM, (h + 1) * HEAD_DIM)
                s_ref[h, cm, 0:kw] = _dot_nt(qcs[g][r, qb * QB[g]:(qb + 1) * QB[g], hs_], hks[g][r, key_win(qb), hs_])

        if g < 2:
            i0 = t * Qc[g] + (rowi & (Qc[g] - QB[g]))
            started = col >= KEY_WIN - QB[g] - i0
        for h in range(HEADS_PER_GROUP):
            bias = bt2[h, pl.ds(pl.multiple_of(t * QB[2], QB[2]), QB[2]), :] if g == 2 else bts[g][h]
            s = s_ref[h, :, 0:kw] * ATTN_SCALE
            s = (s.reshape(len(units), QB[g], kw) + bias[None]).reshape(tq, kw)
            if g < 2:
                s = jnp.where(started, s, NEG)
            m = jnp.max(s, axis=-1, keepdims=True)
            p = jnp.exp(s - m)
            den = jnp.sum(p, axis=-1, keepdims=True)
            p_ref[h, :, 0:kw] = (p * (1.0 / den)).astype(BF16)
            lse = jnp.broadcast_to(m + jnp.log(den), (tq, HEAD_DIM))
            for r in range(d):
                rows = pl.ds(r, Qc[g], stride=d) if d > 1 else slice(None)
                l_refs[g][h, rows, :] = lse[r * Qc[g]:(r + 1) * Qc[g]]

        for r, qb in units:
            cm = slice(r * Qc[g] + qb * QB[g], r * Qc[g] + (qb + 1) * QB[g])
            rows = pl.ds(r + d * qb * QB[g], QB[g], stride=d) if d > 1 else pl.ds(qb * QB[g], QB[g])
            for h in range(HEADS_PER_GROUP):
                hs_ = slice(h * HEAD_DIM, (h + 1) * HEAD_DIM)
                o_refs[g][h, rows, :] = _dot(p_ref[h, cm, 0:kw], hvs[g][r, key_win(qb), hs_])

    hk0[0, 0:tq, :] = hk0[0, tq:2 * tq, :]
    hv0[0, 0:tq, :] = hv0[0, tq:2 * tq, :]

    ob = _merge_groups(o_refs, l_refs)
    ob = jnp.concatenate([ob[h] for h in range(HEADS_PER_GROUP)], axis=1)
    gate = jnp.concatenate([_gate_chunk(u, wg_ref, bm_ref, c) for c in range(N_GATE_CHUNKS)], axis=1)
    y_ref[...] = _tail(x, ya_ref[...].astype(F32), ob, gate, wap_ref, wout_ref)


def _attn_call(x, ya, gn, wqkv, wg, gq, gk, bmerge, relb, wap, wout, *, tq):
    B, S, _ = x.shape
    nt = S // tq
    assert S % tq == 0 and tq % (DILATIONS[2] * 16) == 0 and tq >= N_BACK
    assert S // DILATIONS[2] == N_BACK and tq // DILATIONS[1] <= N_BACK
    Qc = tuple(tq // d for d in DILATIONS)
    QB = tuple(min(q, N_BACK) for q in Qc)
    n2 = S // DILATIONS[2]
    const = lambda shape: pl.BlockSpec(shape, lambda b, t: (0,) * len(shape), pipeline_mode=pl.Buffered(1))
    row = pl.BlockSpec((None, tq, D_MODEL), lambda b, t: (b, t, 0))

    def kv_spec(g):
        keep = min(WINDOWS[g], S)
        rb = min(tq, keep)
        first = nt - keep // rb
        return pl.BlockSpec((None, rb * KV_ROWS, HEAD_DIM), lambda b, t: (b, jnp.maximum(t - first, 0), 0))

    cls = lambda g, rows, dt: pltpu.VMEM((DILATIONS[g], rows, D_GRP), dt)
    scratch = [pltpu.VMEM((D_QKV // HEAD_DIM, tq, HEAD_DIM), F32)] * 3
    scratch += [cls(g, Qc[g], BF16) for g in range(N_GROUPS)]
    scratch += [cls(0, 2 * tq, BF16)] * 2 + [cls(1, KEY_WIN - QB[1] + S // DILATIONS[1], BF16)] * 2 + [cls(2, n2, BF16)] * 2
    scratch += [pltpu.VMEM((HEADS_PER_GROUP, tq, HEAD_DIM), F32)] * 6
    scratch += [pltpu.VMEM((HEADS_PER_GROUP, QB[0], KEY_WIN), F32), pltpu.VMEM((HEADS_PER_GROUP, QB[1], KEY_WIN), F32),
                pltpu.VMEM((HEADS_PER_GROUP, n2, n2), F32)]
    scratch += [pltpu.VMEM((HEADS_PER_GROUP, tq, KEY_WIN), F32), pltpu.VMEM((HEADS_PER_GROUP, tq, KEY_WIN), BF16)]
    return pl.pallas_call(
        functools.partial(_attn_body, S=S, tq=tq),
        grid=(B, nt),
        in_specs=[row, row, const((1, D_MODEL)), const((D_MODEL, 3 * D_QKV)), const((D_MODEL, D_GRP + 2 * D_MODEL)),
                  const((1, HEAD_DIM)), const((1, HEAD_DIM)), const((1, 2 * D_MODEL)),
                  pl.BlockSpec(memory_space=pltpu.SMEM), const((D_GRP, D_MODEL)), const((D_MODEL, D_MODEL))],
        out_specs=[row, kv_spec(0), kv_spec(1), kv_spec(2)],
        out_shape=[jax.ShapeDtypeStruct((B, S, D_MODEL), F32)]
        + [jax.ShapeDtypeStruct((B, min(WINDOWS[g], S) * KV_ROWS, HEAD_DIM), F32) for g in range(N_GROUPS)],
        scratch_shapes=scratch,
        compiler_params=pltpu.CompilerParams(dimension_semantics=("arbitrary", "arbitrary"), vmem_limit_bytes=VMEM_LIMIT),
        name="attn_branch",
    )(x, ya, gn, wqkv, wg, gq, gk, bmerge, relb, wap, wout)


def _qkv_body(x_ref, gn_ref, wqkv_ref, gq_ref, gk_ref, zq_ref, nr0_ref, nr1_ref, nr2_ref):
    M = x_ref.shape[0]
    u = _rms(x_ref[...], gn_ref[...]).astype(BF16)
    nr_refs = (nr0_ref, nr1_ref, nr2_ref)

    def store(part, head, value):
        if part == 0:
            zq_ref[:, head * HEAD_DIM:(head + 1) * HEAD_DIM] = value
        else:
            g, h = divmod(head, HEADS_PER_GROUP)
            nr_refs[g][pl.ds((part - 1) * HEADS_PER_GROUP + h, M, stride=KV_ROWS), :] = value

    _qkv(u, wqkv_ref, gq_ref, gk_ref, store)


def _qkv_call(x, gn, wqkv, gq, gk):
    M = x.shape[0]
    rows = jax.ShapeDtypeStruct((M * KV_ROWS, HEAD_DIM), F32)
    return pl.pallas_call(_qkv_body, out_shape=[jax.ShapeDtypeStruct((M, D_QKV), F32), rows, rows, rows],
                          compiler_params=pltpu.CompilerParams(vmem_limit_bytes=VMEM_LIMIT),
                          name="decode_qkv")(x, gn, wqkv, gq, gk)


def _tail_body(x_ref, ya_ref, gn_ref, wg_ref, bm_ref, wap_ref, wout_ref, o_ref, lse_ref, y_ref):
    x = x_ref[...]
    u = _rms(x, gn_ref[...]).astype(BF16)
    pieces = [(j, slice(g * D_GRP, (g + 1) * D_GRP)) for j in range(o_ref.shape[0]) for g in range(N_GROUPS)]
    ob = _merge_groups([o_ref.at[j, :, gs] for j, gs in pieces], [lse_ref.at[j, :, gs] for j, gs in pieces])
    gate = jnp.concatenate([_gate_chunk(u, wg_ref, bm_ref, c) for c in range(N_GATE_CHUNKS)], axis=1)
    y_ref[...] = _tail(x, ya_ref[...], ob, gate, wap_ref, wout_ref)


def _tail_call(x, ya, gn, wg, bmerge, wap, wout, o, lse):
    return pl.pallas_call(_tail_body, out_shape=jax.ShapeDtypeStruct(x.shape, F32),
                          compiler_params=pltpu.CompilerParams(vmem_limit_bytes=VMEM_LIMIT),
                          name="decode_tail")(x, ya, gn, wg, bmerge, wap, wout, o, lse)


def kernel(x_prompt, x_sample, cache_kv_w128, cache_kv_w512, cache_kv_w2048, state_conv, state_h, g_norm, w_in,
           b_merge, conv_w, conv_b, lru_w_a, lru_b_a, lru_w_x, lru_b_x, lru_lambda, g_q, g_k, rel_bias,
           w_lru_proj, w_attn_proj, w_out):
    assert w_in.shape[0] == 1, "single-layer step"
    B, S, _ = x_prompt.shape
    DB, T, _ = x_sample.shape
    o2 = 2 * D_LRU
    o5 = o2 + 3 * D_QKV
    w_lru = w_in[0, :, :o2].astype(BF16)
    wqkv = w_in[0, :, o2:o5].astype(BF16)
    wg = w_in[0, :, o5:].astype(BF16)
    wa, wx = lru_w_a[0].astype(BF16), lru_w_x[0].astype(BF16)
    wp, wap, wout = w_lru_proj[0].astype(BF16), w_attn_proj[0].astype(BF16), w_out[0].astype(BF16)
    lru_params = (g_norm, w_lru, conv_w[0], conv_b, wa, lru_b_a, wx, lru_b_x, lru_lambda, wp)
    hist = CONV_WIDTH - 1

    caches6 = (cache_kv_w128, cache_kv_w512, cache_kv_w2048)
    caches = [c.reshape(-1, HEAD_DIM) for c in caches6]

    xs2 = x_sample.reshape(DB * T, D_MODEL)
    zq, *newrows = _qkv_call(xs2, g_norm, wqkv, g_q, g_k)

    ya_p, conv_p, h_p, *rest = _lru_call(x_prompt, jnp.zeros((hist * B, D_LRU), F32), jnp.zeros((B, D_LRU), F32),
                                         *lru_params, tt=LRU_TIME_TILE, caches=caches, fresh=newrows,
                                         n_cache_batch=DB, q_dec=zq, relb=rel_bias)
    news, (o_parts, lse_parts) = rest[:N_GROUPS], rest[N_GROUPS:]
    y_p, kv0_p, kv1_p, kv2_p = _attn_call(x_prompt, ya_p, g_norm, wqkv, wg, g_q, g_k, b_merge, rel_bias, wap, wout,
                                          tq=ATTN_ROW_TILE)

    conv0_s = jnp.swapaxes(state_conv[0], 0, 1).reshape(hist * DB, D_LRU)
    ya_s, conv_s, h_s = _lru_call(x_sample, conv0_s, state_h[0], *lru_params, tt=T)
    spb = S // LRU_TIME_TILE // DB
    by_part = lambda a: jnp.swapaxes(a.reshape(DB, spb, T, D_QKV), 0, 1).reshape(spb, DB * T, D_QKV)
    y_s = _tail_call(xs2, ya_s.reshape(DB * T, D_MODEL), g_norm, wg, b_merge, wap, wout,
                     by_part(o_parts), by_part(lse_parts))

    kv_shape = lambda a: a.reshape(1, a.shape[0], a.shape[1] // KV_ROWS, 2, HEADS_PER_GROUP, HEAD_DIM)
    conv_out = lambda c, nb: jnp.swapaxes(c.reshape(hist, nb, D_LRU), 0, 1)[None]
    news = [n.reshape(c.shape) for n, c in zip(news, caches6)]
    return (y_p, y_s.reshape(DB, T, D_MODEL), kv_shape(kv0_p), kv_shape(kv1_p), kv_shape(kv2_p),
            conv_out(conv_p, B), h_p[None], news[0], news[1], news[2], conv_out(conv_s, DB), h_s[None])
```

```python
import functools
import math

import jax
import jax.numpy as jnp
from jax import lax
from jax.experimental import pallas as pl
from jax.experimental.pallas import tpu as pltpu

F32 = jnp.float32
BF16 = jnp.bfloat16

D_MODEL = 1024
D_LRU = 1024
N_LRU_BLOCKS = 8
LRU_BLOCK = D_LRU // N_LRU_BLOCKS
CONV_WIDTH = 4
LRU_C = 8.0
HEAD_DIM = 128
HEADS_PER_GROUP = 4
WINDOWS = (128, 512, 2048)
DILATIONS = (1, 4, 16)
N_GROUPS = 3
N_BACK = 128
D_QKV = N_GROUPS * HEADS_PER_GROUP * HEAD_DIM
D_GRP = HEADS_PER_GROUP * HEAD_DIM
ATTN_SCALE = HEAD_DIM ** -0.5
N_BUCKETS = 32
MAX_DISTANCE = 2048
NORM_EPS = 1e-6
NEG = -1e30
KEY_WIN = 256
KV_ROWS = 2 * HEADS_PER_GROUP
VMEM_LIMIT = 60000 * 1024
LRU_TIME_TILE = 64
ATTN_ROW_TILE = 256


def _rms(x, g):
    ms = jnp.mean(x * x, axis=-1, keepdims=True)
    return x * lax.rsqrt(ms + NORM_EPS) * g


def _sigmoid(x):
    return 0.5 * jnp.tanh(0.5 * x) + 0.5


def _dot(a, b):
    return jnp.dot(a, b, preferred_element_type=F32)


def _dot_nt(a, b):
    return lax.dot_general(a, b, (((1,), (1,)), ((), ())), preferred_element_type=F32)


def _bias_table(relb_ref, col, dist, valid):
    max_exact = N_BUCKETS // 2
    n_log = N_BUCKETS - max_exact
    df = jnp.maximum(dist, 1).astype(F32)
    val = jnp.log(df / max_exact) / math.log(MAX_DISTANCE / max_exact) * n_log
    out = jnp.zeros(dist.shape, F32)
    for b in range(max_exact):
        out = jnp.where(dist == b, relb_ref[b, col], out)
    for k in range(n_log):
        out = jnp.where((dist >= max_exact) & (val >= k), relb_ref[max_exact + k, col], out)
    return jnp.where(valid, out, NEG)


def _merge_groups(o_refs, lse_refs):
    lses = [r[...] for r in lse_refs]
    m = functools.reduce(jnp.maximum, lses)
    es = [jnp.exp(l - m) for l in lses]
    num = sum(e * r[...] for e, r in zip(es, o_refs))
    return num / sum(es)


N_GATE_CHUNKS = (D_GRP + 2 * D_MODEL) // D_GRP


def _gate_chunk(u, wg_ref, bm_ref, c):
    z = _dot(u, wg_ref[:, c * D_GRP:(c + 1) * D_GRP])
    if c == 0:
        return z * _sigmoid(z)
    return _sigmoid(z + bm_ref[:, (c - 1) * D_GRP:c * D_GRP])


def _tail(x, ya, ob, gate, wap_ref, wout_ref):
    yb = _dot((ob * gate[:, :D_GRP]).astype(BF16), wap_ref[...])
    merged = gate[:, D_GRP:D_GRP + D_MODEL] * ya + gate[:, D_GRP + D_MODEL:] * yb
    return x + _dot(merged.astype(BF16), wout_ref[...])


def _qkv(u, wqkv_ref, gq_ref, gk_ref, store):
    for part, g_ref in enumerate((gq_ref, gk_ref, None)):
        z = _dot(u, wqkv_ref[:, part * D_QKV:(part + 1) * D_QKV])
        zs = [z[:, h * HEAD_DIM:(h + 1) * HEAD_DIM] for h in range(D_QKV // HEAD_DIM)]
        if g_ref is not None:
            ms = [jnp.mean(zh * zh, axis=-1, keepdims=True) for zh in zs]
            zs = [zh * lax.rsqrt(m + NORM_EPS) * g_ref[...] for zh, m in zip(zs, ms)]
        for h, zh in enumerate(zs):
            store(part, h, zh)


N_LRU_IN = 13


def _shift_cache_block(step, blk_ref, next_ref, fresh_ref, out_hbm, tail_ref, body_sem, tail_sem, batch_rows):
    rpb = blk_ref.shape[0]
    hop = fresh_ref.shape[0]
    r0 = pl.multiple_of(step * rpb, rpb)
    ends_batch = ((step + 1) * rpb) % batch_rows == 0
    tail_ref[...] = jnp.where(ends_batch, fresh_ref[...], next_ref[...])
    return (pltpu.make_async_copy(blk_ref.at[pl.ds(hop, rpb - hop), :], out_hbm.at[pl.ds(r0, rpb - hop), :], body_sem),
            pltpu.make_async_copy(tail_ref, out_hbm.at[pl.ds(r0 + rpb - hop, hop), :], tail_sem))


def _shift_specs(cache, fresh, steps, n_batch, step_of):
    total = cache.shape[0]
    rpb, rows_b, hop = total // steps, total // n_batch, fresh.shape[0] // n_batch
    assert total % steps == 0 and rows_b % rpb == 0 and rpb % hop == 0 and rpb > hop and hop % 8 == 0
    return [
        pl.BlockSpec((rpb, HEAD_DIM), lambda *g: (step_of(*g), 0)),
        pl.BlockSpec((hop, HEAD_DIM), lambda *g: (jnp.minimum((step_of(*g) + 1) * (rpb // hop), total // hop - 1), 0)),
        pl.BlockSpec((hop, HEAD_DIM), lambda *g: ((step_of(*g) * rpb) // rows_b, 0)),
    ], rows_b


def _decode_keys(positions, d, T):
    return positions // d * T if d >= 2 * T else positions


def _decode_attn_part(q_ref, relb_ref, blk_refs, fresh_refs, o_ref, lse_ref, kcs, vcs, bts, spb):
    T = q_ref.shape[0]
    assert T & (T - 1) == 0
    part = pl.program_id(0) % spb
    Lhs = tuple(blk.shape[0] // KV_ROWS for blk in blk_refs)
    Ks = tuple(_decode_keys(Lh, d, T) for Lh, d in zip(Lhs, DILATIONS))

    @pl.when(pl.program_id(0) == 0)
    def _():
        for g in range(N_GROUPS):
            Lh, K, d = Lhs[g], Ks[g], DILATIONS[g]
            L, W = Lh * spb, K + HEAD_DIM
            kcs[g][K:W, :] = jnp.zeros((W - K, D_GRP), BF16)
            vcs[g][K:W, :] = jnp.zeros((W - K, D_GRP), BF16)
            tq = lax.broadcasted_iota(jnp.int32, (T, W), 0)
            c = lax.broadcasted_iota(jnp.int32, (T, W), 1)
            within = c if K == Lh else (c >> (T.bit_length() - 1)) * d + (c & (T - 1))
            for j in range(spb):
                p = jnp.where(c < K, j * Lh + within, L + c - K)
                present = (c < K) | ((c < K + T) & (j == spb - 1))
                dist = L + tq - p
                valid = present & (dist >= 0) & ((dist & (d - 1)) == 0) & (dist <= N_BACK * d)
                for h in range(HEADS_PER_GROUP):
                    bts[g][j, h * T:(h + 1) * T, :] = _bias_table(relb_ref, g * HEADS_PER_GROUP + h, dist, valid)

    heads = [slice(h * HEAD_DIM, (h + 1) * HEAD_DIM) for h in range(HEADS_PER_GROUP)]

    def gather(g):
        Lh, K, d = Lhs[g], Ks[g], DILATIONS[g]
        if K == Lh:
            rows_of = lambda r: blk_refs[g][pl.ds(r, Lh, stride=KV_ROWS), :]
        else:
            periods = blk_refs[g].reshape(Lh // d, d * KV_ROWS, HEAD_DIM)
            rows_of = lambda r: periods[:, pl.ds(r, T, stride=KV_ROWS), :].reshape(K, HEAD_DIM)
        for h, hs_ in enumerate(heads):
            kcs[g][0:K, hs_] = rows_of(h).astype(BF16)
            vcs[g][0:K, hs_] = rows_of(HEADS_PER_GROUP + h).astype(BF16)
        k_new = jnp.concatenate([fresh_refs[g][pl.ds(h, T, stride=KV_ROWS), :] for h in range(HEADS_PER_GROUP)], axis=1)
        v_new = jnp.concatenate([fresh_refs[g][pl.ds(HEADS_PER_GROUP + h, T, stride=KV_ROWS), :]
                                 for h in range(HEADS_PER_GROUP)], axis=1)
        kcs[g][K:K + 2 * T, :] = jnp.concatenate([k_new, jnp.zeros_like(k_new)], axis=0).astype(BF16)
        vcs[g][K:K + 2 * T, :] = jnp.concatenate([v_new, jnp.zeros_like(v_new)], axis=0).astype(BF16)

    def attend(g):
        cols = [slice(g * D_GRP + h * HEAD_DIM, g * D_GRP + (h + 1) * HEAD_DIM) for h in range(HEADS_PER_GROUP)]
        s = jnp.concatenate([_dot_nt(q_ref[:, cs].astype(BF16), kcs[g][:, hs_]) for cs, hs_ in zip(cols, heads)], axis=0)
        s = s * ATTN_SCALE + bts[g][part]
        m = jnp.max(s, axis=-1, keepdims=True)
        p = jnp.exp(s - m)
        den = jnp.sum(p, axis=-1, keepdims=True)
        p = p * (1.0 / den)
        lse = m + jnp.log(den)
        for h, (cs, hs_) in enumerate(zip(cols, heads)):
            rows = slice(h * T, (h + 1) * T)
            o_ref[:, cs] = _dot(p[rows].astype(BF16), vcs[g][:, hs_])
            lse_ref[:, cs] = jnp.broadcast_to(lse[rows], (T, HEAD_DIM))

    return [functools.partial(f, g) for g in range(N_GROUPS) for f in (gather, attend)]


def _lru_body(*refs, B, tt, batch_rows, shifted, spb):
    (x_ref, gn_ref, w_ref, cw_ref, cb_ref, wa_ref, ba_ref, wx_ref, bx_ref, lam_ref, wp_ref,
     conv0_ref, h0_ref) = refs[:N_LRU_IN]
    n_shift = len(batch_rows)
    n_dec = 2 if n_shift else 0
    i0 = N_LRU_IN
    shift_in, i0 = refs[i0:i0 + 3 * n_shift], i0 + 3 * n_shift
    dec_in, i0 = refs[i0:i0 + n_dec], i0 + n_dec
    (ya_ref, convo_ref, ho_ref), i0 = refs[i0:i0 + 3], i0 + 3
    shift_out, i0 = refs[i0:i0 + len(shifted)], i0 + len(shifted)
    dec_out, i0 = refs[i0:i0 + n_dec], i0 + n_dec
    (xs_ref, ga_ref, a_ref, hs_ref, h_ref), i0 = refs[i0:i0 + 5], i0 + 5
    dec_scratch = refs[i0:]
    R = B * tt
    hist = (CONV_WIDTH - 1) * B

    @pl.when(pl.program_id(0) == 0)
    def _():
        xs_ref[0:hist, :] = conv0_ref[...]
        h_ref[...] = h0_ref[...]

    shift_copies, decode_stages = [], []
    if n_shift:
        kcs, vcs = dec_scratch[0:2 * n_shift:2], dec_scratch[1:2 * n_shift:2]
        bts, tails, sem = dec_scratch[2 * n_shift:3 * n_shift], dec_scratch[3 * n_shift:-1], dec_scratch[-1]
        for j, k in enumerate(shifted):
            shift_copies += _shift_cache_block(pl.program_id(0), *shift_in[3 * k:3 * k + 3], shift_out[j], tails[j],
                                               sem.at[2 * j], sem.at[2 * j + 1], batch_rows[k])
        decode_stages = _decode_attn_part(*dec_in, shift_in[0::3], shift_in[2::3], *dec_out, kcs, vcs, bts, spb)
    for cp in shift_copies:
        cp.start()
    assert len(decode_stages) <= N_LRU_BLOCKS

    xt = jnp.concatenate([x_ref[:, t, :] for t in range(tt)], axis=0)
    u = _rms(xt, gn_ref[...]).astype(BF16)
    xs_ref[hist:hist + R, :] = _dot(u, w_ref[:, :D_LRU])
    ga_ref[...] = _dot(u, w_ref[:, D_LRU:])
    sp = jax.nn.softplus(-lam_ref[...])
    for n in range(N_LRU_BLOCKS):
        if n < len(decode_stages):
            decode_stages[n]()
        cs = slice(n * LRU_BLOCK, (n + 1) * LRU_BLOCK)
        y = cb_ref[:, cs] + sum(xs_ref[j * B:j * B + R, cs] * cw_ref[j:j + 1, cs] for j in range(CONV_WIDTH))
        yb = y.astype(BF16)
        r = _sigmoid(_dot(yb, wa_ref[n]) + ba_ref[:, cs])
        i = _sigmoid(_dot(yb, wx_ref[n]) + bx_ref[:, cs])
        log_a = -LRU_C * r * sp[:, cs]
        a = jnp.exp(log_a)
        th = jnp.tanh(log_a)
        a_ref[:, cs] = a
        hs_ref[:, cs] = jnp.sqrt(-2.0 * th) * lax.rsqrt(1.0 - th) * (i * y)
    h = h_ref[...]
    for t in range(tt):
        rows = slice(t * B, (t + 1) * B)
        h = a_ref[rows, :] * h + hs_ref[rows, :]
        hs_ref[rows, :] = h
    h_ref[...] = h
    g = ga_ref[...]
    ya = _dot((hs_ref[...] * (g * _sigmoid(g))).astype(BF16), wp_ref[...])
    for t in range(tt):
        ya_ref[:, t, :] = ya[t * B:(t + 1) * B].astype(ya_ref.dtype)
    tail = xs_ref[R:R + hist, :]
    convo_ref[...] = tail
    xs_ref[0:hist, :] = tail
    ho_ref[...] = h_ref[...]
    for cp in shift_copies:
        cp.wait()


def _lru_call(x, conv0, h0, gn, w_lru, cw, cb, wa, ba, wx, bx, lam, wp, *, tt, caches=(), fresh=(), n_cache_batch=1,
              shifted=(), q_dec=None, relb=None):
    B, S, _ = x.shape
    assert S % tt == 0 and tt >= CONV_WIDTH - 1 and tt % 8 == 0 and B % 8 == 0
    R = B * tt
    hist = (CONV_WIDTH - 1) * B
    steps = S // tt
    spb = steps // n_cache_batch
    const = lambda shape: pl.BlockSpec(shape, lambda i: (0,) * len(shape), pipeline_mode=pl.Buffered(1))
    shift_in, batch_rows = [], []
    for c, f in zip(caches, fresh):
        specs, rows_b = _shift_specs(c, f, steps, n_cache_batch, lambda i: i)
        assert rows_b == spb * (c.shape[0] // steps)
        shift_in += specs
        batch_rows.append(rows_b)
    shift_out = [pl.BlockSpec(memory_space=pl.ANY)] * len(shifted)
    shift_shapes = [jax.ShapeDtypeStruct(caches[k].shape, caches[k].dtype) for k in shifted]
    shift_args = [a for c, f in zip(caches, fresh) for a in (c, c, f)]
    dec_in, dec_out, dec_shapes, dec_scratch = [], [], [], []
    if caches:
        assert len(caches) == N_GROUPS and q_dec.shape[0] % n_cache_batch == 0
        T = q_dec.shape[0] // n_cache_batch
        part = jax.ShapeDtypeStruct((steps * T, D_QKV), F32)
        dec_in = [pl.BlockSpec((T, D_QKV), lambda i: (i // spb, 0)), pl.BlockSpec(memory_space=pltpu.SMEM)]
        dec_out = [pl.BlockSpec((T, D_QKV), lambda i: (i, 0))] * 2
        dec_shapes = [part, part]
        shift_args += [q_dec, relb]
        assert T % 8 == 0
        widths = [_decode_keys(c.shape[0] // steps // KV_ROWS, d, T) + HEAD_DIM for c, d in zip(caches, DILATIONS)]
        for w in widths:
            dec_scratch += [pltpu.VMEM((w, D_GRP), BF16)] * 2
        dec_scratch += [pltpu.VMEM((spb, HEADS_PER_GROUP * T, w), F32) for w in widths]
        dec_scratch += [pltpu.VMEM((fresh[k].shape[0] // n_cache_batch, HEAD_DIM), F32) for k in shifted]
        dec_scratch += [pltpu.SemaphoreType.DMA((max(2 * len(shifted), 1),))]
    return pl.pallas_call(
        functools.partial(_lru_body, B=B, tt=tt, batch_rows=tuple(batch_rows), shifted=tuple(shifted), spb=spb),
        grid=(steps,),
        in_specs=[
            pl.BlockSpec((B, tt, D_MODEL), lambda i: (0, i, 0)),
            const((1, D_MODEL)), const((D_MODEL, 2 * D_LRU)), const((CONV_WIDTH, D_LRU)), const((1, D_LRU)),
            const((N_LRU_BLOCKS, LRU_BLOCK, LRU_BLOCK)), const((1, D_LRU)),
            const((N_LRU_BLOCKS, LRU_BLOCK, LRU_BLOCK)), const((1, D_LRU)), const((1, D_LRU)),
            const((D_LRU, D_MODEL)), const((hist, D_LRU)), const((B, D_LRU)),
        ] + shift_in + dec_in,
        out_specs=[
            pl.BlockSpec((B, tt, D_MODEL), lambda i: (0, i, 0)),
            pl.BlockSpec((hist, D_LRU), lambda i: (0, 0)),
            pl.BlockSpec((B, D_LRU), lambda i: (0, 0)),
        ] + shift_out + dec_out,
        out_shape=[
            jax.ShapeDtypeStruct((B, S, D_MODEL), F32),
            jax.ShapeDtypeStruct((hist, D_LRU), F32),
            jax.ShapeDtypeStruct((B, D_LRU), F32),
        ] + shift_shapes + dec_shapes,
        scratch_shapes=[
            pltpu.VMEM((hist + R, D_LRU), F32),
            pltpu.VMEM((R, D_LRU), F32),
            pltpu.VMEM((R, D_LRU), F32),
            pltpu.VMEM((R, D_LRU), F32),
            pltpu.VMEM((B, D_LRU), F32),
        ] + dec_scratch,
        compiler_params=pltpu.CompilerParams(dimension_semantics=("arbitrary",), vmem_limit_bytes=VMEM_LIMIT),
        name="lru_branch",
    )(x, gn, w_lru, cw, cb, wa, ba, wx, bx, lam, wp, conv0, h0, *shift_args)


N_ATTN_IN, N_ATTN_OUT, N_ATTN_SCRATCH = 11, 4, 23


def _attn_body(*refs, S, tq, shift_batch_rows):
    (x_ref, ya_ref, gn_ref, wqkv_ref, wg_ref, gq_ref, gk_ref, bm_ref, relb_ref, wap_ref, wout_ref) = refs[:N_ATTN_IN]
    ns = len(shift_batch_rows)
    i0 = N_ATTN_IN
    shift_in, i0 = refs[i0:i0 + 3 * ns], i0 + 3 * ns
    (y_ref, kv0_ref, kv1_ref, kv2_ref), i0 = refs[i0:i0 + N_ATTN_OUT], i0 + N_ATTN_OUT
    shift_out, i0 = refs[i0:i0 + ns], i0 + ns
    (zq_ref, zk_ref, zv_ref, qc0, qc1, qc2, hk0, hv0, hk1, hv1, hk2, hv2,
     o0, o1, o2, l0, l1, l2, bt0, bt1, bt2, s_ref, p_ref), i0 = refs[i0:i0 + N_ATTN_SCRATCH], i0 + N_ATTN_SCRATCH
    shift_scratch = refs[i0:]
    t = pl.program_id(1)
    nt = S // tq

    shift_copies = []
    for k in range(ns):
        shift_copies += _shift_cache_block(pl.program_id(0) * nt + t, *shift_in[3 * k:3 * k + 3], shift_out[k],
                                           shift_scratch[k], shift_scratch[-1].at[2 * k],
                                           shift_scratch[-1].at[2 * k + 1], shift_batch_rows[k])
    for cp in shift_copies:
        cp.start()

    qcs, hks, hvs = (qc0, qc1, qc2), (hk0, hk1, hk2), (hv0, hv1, hv2)
    o_refs, l_refs, bts, kvs = (o0, o1, o2), (l0, l1, l2), (bt0, bt1, bt2), (kv0_ref, kv1_ref, kv2_ref)
    Qc = tuple(tq // d for d in DILATIONS)
    QB = tuple(min(q, N_BACK) for q in Qc)
    pad1 = KEY_WIN - QB[1]
    n2 = S // DILATIONS[2]

    @pl.when((pl.program_id(0) == 0) & (t == 0))
    def _():
        for ref in (hk0, hv0, hk1, hv1, hk2, hv2):
            ref[...] = jnp.zeros(ref.shape, ref.dtype)
        for g in (0, 1):
            a = lax.broadcasted_iota(jnp.int32, (QB[g], KEY_WIN), 0)
            c = lax.broadcasted_iota(jnp.int32, (QB[g], KEY_WIN), 1)
            j = a + (KEY_WIN - QB[g]) - c
            for h in range(HEADS_PER_GROUP):
                bts[g][h] = _bias_table(relb_ref, g * HEADS_PER_GROUP + h, j * DILATIONS[g], (j >= 0) & (j <= N_BACK))
        i = lax.broadcasted_iota(jnp.int32, (n2, n2), 0)
        c = lax.broadcasted_iota(jnp.int32, (n2, n2), 1)
        for h in range(HEADS_PER_GROUP):
            bt2[h] = _bias_table(relb_ref, 2 * HEADS_PER_GROUP + h, (i - c) * DILATIONS[2],
                                 (i - c >= 0) & (i - c <= N_BACK))

    x = x_ref[...]
    u = _rms(x, gn_ref[...]).astype(BF16)
    z_refs = (zq_ref, zk_ref, zv_ref)

    def store(part, head, value):
        z_refs[part][head] = value

    _qkv(u, wqkv_ref, gq_ref, gk_ref, store)

    for g in range(N_GROUPS):
        keep = min(WINDOWS[g], S)
        rb = min(tq, keep)

        @pl.when(t >= nt - keep // rb)
        def _(g=g, rb=rb):
            for part, z_ref in enumerate((zk_ref, zv_ref)):
                for h in range(HEADS_PER_GROUP):
                    dst = pl.ds(part * HEADS_PER_GROUP + h, rb, stride=KV_ROWS)
                    kvs[g][dst, :] = z_ref[g * HEADS_PER_GROUP + h, tq - rb:tq, :]

    for g in range(N_GROUPS):
        d = DILATIONS[g]
        for r in range(d):
            rows = pl.ds(r, Qc[g], stride=d) if d > 1 else slice(None)
            if g == 0:
                dst = pl.ds(tq, tq)
            elif g == 1:
                dst = pl.ds(pl.multiple_of(pad1 + t * Qc[1], Qc[1]), Qc[1])
            else:
                dst = pl.ds(pl.multiple_of(t * Qc[2], Qc[2]), Qc[2])
            for h in range(HEADS_PER_GROUP):
                hs_ = slice(h * HEAD_DIM, (h + 1) * HEAD_DIM)
                gh = g * HEADS_PER_GROUP + h
                qcs[g][r, :, hs_] = zq_ref[gh, rows, :].astype(BF16)
                hks[g][r, dst, hs_] = zk_ref[gh, rows, :].astype(BF16)
                hvs[g][r, dst, hs_] = zv_ref[gh, rows, :].astype(BF16)

    col = lax.broadcasted_iota(jnp.int32, (1, KEY_WIN), 1)
    rowi = lax.broadcasted_iota(jnp.int32, (tq, 1), 0)
    for g in range(N_GROUPS):
        d = DILATIONS[g]
        nq = Qc[g] // QB[g]
        kw = n2 if g == 2 else KEY_WIN
        units = [(r, qb) for r in range(d) for qb in range(nq)]

        def key_win(qb, g=g):
            if g == 0:
                return pl.ds(tq + qb * QB[0] + QB[0] - KEY_WIN, KEY_WIN)
            if g == 1:
                return pl.ds(pl.multiple_of(t * Qc[1], QB[1]), KEY_WIN)
            return slice(None)

        for r, qb in units:
            cm = slice(r * Qc[g] + qb * QB[g], r * Qc[g] + (qb + 1) * QB[g])
            for h in range(HEADS_PER_GROUP):
                hs_ = slice(h * HEAD_DIM, (h + 1) * HEAD_DIM)
                s_ref[h, cm, 0:kw] = _dot_nt(qcs[g][r, qb * QB[g]:(qb + 1) * QB[g], hs_], hks[g][r, key_win(qb), hs_])

        if g < 2:
            i0 = t * Qc[g] + (rowi & (Qc[g] - QB[g]))
            started = col >= KEY_WIN - QB[g] - i0
        for h in range(HEADS_PER_GROUP):
            bias = bt2[h, pl.ds(pl.multiple_of(t * QB[2], QB[2]), QB[2]), :] if g == 2 else bts[g][h]
            s = s_ref[h, :, 0:kw] * ATTN_SCALE
            s = (s.reshape(len(units), QB[g], kw) + bias[None]).reshape(tq, kw)
            if g < 2:
                s = jnp.where(started, s, NEG)
            m = jnp.max(s, axis=-1, keepdims=True)
            p = jnp.exp(s - m)
            den = jnp.sum(p, axis=-1, keepdims=True)
            p_ref[h, :, 0:kw] = (p * (1.0 / den)).astype(BF16)
            lse = jnp.broadcast_to(m + jnp.log(den), (tq, HEAD_DIM))
            for r in range(d):
                rows = pl.ds(r, Qc[g], stride=d) if d > 1 else slice(None)
                l_refs[g][h, rows, :] = lse[r * Qc[g]:(r + 1) * Qc[g]]

        for r, qb in units:
            cm = slice(r * Qc[g] + qb * QB[g], r * Qc[g] + (qb + 1) * QB[g])
            rows = pl.ds(r + d * qb * QB[g], QB[g], stride=d) if d > 1 else pl.ds(qb * QB[g], QB[g])
            for h in range(HEADS_PER_GROUP):
                hs_ = slice(h * HEAD_DIM, (h + 1) * HEAD_DIM)
                o_refs[g][h, rows, :] = _dot(p_ref[h, cm, 0:kw], hvs[g][r, key_win(qb), hs_])

    hk0[0, 0:tq, :] = hk0[0, tq:2 * tq, :]
    hv0[0, 0:tq, :] = hv0[0, tq:2 * tq, :]

    ob = _merge_groups(o_refs, l_refs)
    ob = jnp.concatenate([ob[h] for h in range(HEADS_PER_GROUP)], axis=1)
    gate = jnp.concatenate([_gate_chunk(u, wg_ref, bm_ref, c) for c in range(N_GATE_CHUNKS)], axis=1)
    y_ref[...] = _tail(x, ya_ref[...].astype(F32), ob, gate, wap_ref, wout_ref)
    for cp in shift_copies:
        cp.wait()


def _attn_call(x, ya, gn, wqkv, wg, gq, gk, bmerge, relb, wap, wout, *, tq, caches=(), fresh=(), n_cache_batch=1):
    B, S, _ = x.shape
    nt = S // tq
    assert S % tq == 0 and tq % (DILATIONS[2] * 16) == 0 and tq >= N_BACK
    assert S // DILATIONS[2] == N_BACK and tq // DILATIONS[1] <= N_BACK
    Qc = tuple(tq // d for d in DILATIONS)
    QB = tuple(min(q, N_BACK) for q in Qc)
    n2 = S // DILATIONS[2]
    const = lambda shape: pl.BlockSpec(shape, lambda b, t: (0,) * len(shape), pipeline_mode=pl.Buffered(1))
    row = pl.BlockSpec((None, tq, D_MODEL), lambda b, t: (b, t, 0))

    def kv_spec(g):
        keep = min(WINDOWS[g], S)
        rb = min(tq, keep)
        first = nt - keep // rb
        return pl.BlockSpec((None, rb * KV_ROWS, HEAD_DIM), lambda b, t: (b, jnp.maximum(t - first, 0), 0))

    cls = lambda g, rows, dt: pltpu.VMEM((DILATIONS[g], rows, D_GRP), dt)
    scratch = [pltpu.VMEM((D_QKV // HEAD_DIM, tq, HEAD_DIM), F32)] * 3
    scratch += [cls(g, Qc[g], BF16) for g in range(N_GROUPS)]
    scratch += [cls(0, 2 * tq, BF16)] * 2 + [cls(1, KEY_WIN - QB[1] + S // DILATIONS[1], BF16)] * 2 + [cls(2, n2, BF16)] * 2
    scratch += [pltpu.VMEM((HEADS_PER_GROUP, tq, HEAD_DIM), F32)] * 6
    scratch += [pltpu.VMEM((HEADS_PER_GROUP, QB[0], KEY_WIN), F32), pltpu.VMEM((HEADS_PER_GROUP, QB[1], KEY_WIN), F32),
                pltpu.VMEM((HEADS_PER_GROUP, n2, n2), F32)]
    scratch += [pltpu.VMEM((HEADS_PER_GROUP, tq, KEY_WIN), F32), pltpu.VMEM((HEADS_PER_GROUP, tq, KEY_WIN), BF16)]
    assert len(scratch) == N_ATTN_SCRATCH
    shift_in, shift_batch_rows = [], []
    for c, f in zip(caches, fresh):
        specs, rows_b = _shift_specs(c, f, B * nt, n_cache_batch, lambda b, t: b * nt + t)
        shift_in += specs
        shift_batch_rows.append(rows_b)
    if caches:
        scratch += [pltpu.VMEM((f.shape[0] // n_cache_batch, HEAD_DIM), F32) for f in fresh]
        scratch += [pltpu.SemaphoreType.DMA((2 * len(caches),))]
    return pl.pallas_call(
        functools.partial(_attn_body, S=S, tq=tq, shift_batch_rows=tuple(shift_batch_rows)),
        grid=(B, nt),
        in_specs=[row, row, const((1, D_MODEL)), const((D_MODEL, 3 * D_QKV)), const((D_MODEL, D_GRP + 2 * D_MODEL)),
                  const((1, HEAD_DIM)), const((1, HEAD_DIM)), const((1, 2 * D_MODEL)),
                  pl.BlockSpec(memory_space=pltpu.SMEM), const((D_GRP, D_MODEL)), const((D_MODEL, D_MODEL))] + shift_in,
        out_specs=[row, kv_spec(0), kv_spec(1), kv_spec(2)] + [pl.BlockSpec(memory_space=pl.ANY)] * len(caches),
        out_shape=[jax.ShapeDtypeStruct((B, S, D_MODEL), F32)]
        + [jax.ShapeDtypeStruct((B, min(WINDOWS[g], S) * KV_ROWS, HEAD_DIM), F32) for g in range(N_GROUPS)]
        + [jax.ShapeDtypeStruct(c.shape, c.dtype) for c in caches],
        scratch_shapes=scratch,
        compiler_params=pltpu.CompilerParams(dimension_semantics=("arbitrary", "arbitrary"), vmem_limit_bytes=VMEM_LIMIT),
        name="attn_branch",
    )(x, ya, gn, wqkv, wg, gq, gk, bmerge, relb, wap, wout, *[a for c, f in zip(caches, fresh) for a in (c, c, f)])


def _qkv_body(x_ref, gn_ref, wqkv_ref, gq_ref, gk_ref, zq_ref, nr0_ref, nr1_ref, nr2_ref):
    M = x_ref.shape[0]
    u = _rms(x_ref[...], gn_ref[...]).astype(BF16)
    nr_refs = (nr0_ref, nr1_ref, nr2_ref)

    def store(part, head, value):
        if part == 0:
            zq_ref[:, head * HEAD_DIM:(head + 1) * HEAD_DIM] = value
        else:
            g, h = divmod(head, HEADS_PER_GROUP)
            nr_refs[g][pl.ds((part - 1) * HEADS_PER_GROUP + h, M, stride=KV_ROWS), :] = value

    _qkv(u, wqkv_ref, gq_ref, gk_ref, store)


def _qkv_call(x, gn, wqkv, gq, gk):
    M = x.shape[0]
    rows = jax.ShapeDtypeStruct((M * KV_ROWS, HEAD_DIM), F32)
    return pl.pallas_call(_qkv_body, out_shape=[jax.ShapeDtypeStruct((M, D_QKV), F32), rows, rows, rows],
                          compiler_params=pltpu.CompilerParams(vmem_limit_bytes=VMEM_LIMIT),
                          name="decode_qkv")(x, gn, wqkv, gq, gk)


def _tail_body(x_ref, ya_ref, gn_ref, wg_ref, bm_ref, wap_ref, wout_ref, o_ref, lse_ref, y_ref):
    x = x_ref[...]
    u = _rms(x, gn_ref[...]).astype(BF16)
    pieces = [(j, slice(g * D_GRP, (g + 1) * D_GRP)) for j in range(o_ref.shape[0]) for g in range(N_GROUPS)]
    ob = _merge_groups([o_ref.at[j, :, gs] for j, gs in pieces], [lse_ref.at[j, :, gs] for j, gs in pieces])
    gate = jnp.concatenate([_gate_chunk(u, wg_ref, bm_ref, c) for c in range(N_GATE_CHUNKS)], axis=1)
    y_ref[...] = _tail(x, ya_ref[...], ob, gate, wap_ref, wout_ref)


def _tail_call(x, ya, gn, wg, bmerge, wap, wout, o, lse):
    return pl.pallas_call(_tail_body, out_shape=jax.ShapeDtypeStruct(x.shape, F32),
                          compiler_params=pltpu.CompilerParams(vmem_limit_bytes=VMEM_LIMIT),
                          name="decode_tail")(x, ya, gn, wg, bmerge, wap, wout, o, lse)


def kernel(x_prompt, x_sample, cache_kv_w128, cache_kv_w512, cache_kv_w2048, state_conv, state_h, g_norm, w_in,
           b_merge, conv_w, conv_b, lru_w_a, lru_b_a, lru_w_x, lru_b_x, lru_lambda, g_q, g_k, rel_bias,
           w_lru_proj, w_attn_proj, w_out):
    assert w_in.shape[0] == 1, "single-layer step"
    B, S, _ = x_prompt.shape
    DB, T, _ = x_sample.shape
    o2 = 2 * D_LRU
    o5 = o2 + 3 * D_QKV
    w_lru = w_in[0, :, :o2].astype(BF16)
    wqkv = w_in[0, :, o2:o5].astype(BF16)
    wg = w_in[0, :, o5:].astype(BF16)
    wa, wx = lru_w_a[0].astype(BF16), lru_w_x[0].astype(BF16)
    wp, wap, wout = w_lru_proj[0].astype(BF16), w_attn_proj[0].astype(BF16), w_out[0].astype(BF16)
    lru_params = (g_norm, w_lru, conv_w[0], conv_b, wa, lru_b_a, wx, lru_b_x, lru_lambda, wp)
    hist = CONV_WIDTH - 1

    caches6 = (cache_kv_w128, cache_kv_w512, cache_kv_w2048)
    caches = [c.reshape(-1, HEAD_DIM) for c in caches6]

    xs2 = x_sample.reshape(DB * T, D_MODEL)
    zq, *newrows = _qkv_call(xs2, g_norm, wqkv, g_q, g_k)

    in_lru = (N_GROUPS - 1,)
    in_attn = tuple(g for g in range(N_GROUPS) if g not in in_lru)
    ya_p, conv_p, h_p, *rest = _lru_call(x_prompt, jnp.zeros((hist * B, D_LRU), F32), jnp.zeros((B, D_LRU), F32),
                                         *lru_params, tt=LRU_TIME_TILE, caches=caches, fresh=newrows,
                                         n_cache_batch=DB, shifted=in_lru, q_dec=zq, relb=rel_bias)
    news_lru, (o_parts, lse_parts) = rest[:len(in_lru)], rest[len(in_lru):]
    y_p, kv0_p, kv1_p, kv2_p, *news_attn = _attn_call(
        x_prompt, ya_p, g_norm, wqkv, wg, g_q, g_k, b_merge, rel_bias, wap, wout, tq=ATTN_ROW_TILE,
        caches=[caches[g] for g in in_attn], fresh=[newrows[g] for g in in_attn], n_cache_batch=DB)
    news = dict(zip(in_lru + in_attn, news_lru + news_attn))
    news = [news[g] for g in range(N_GROUPS)]

    conv0_s = jnp.swapaxes(state_conv[0], 0, 1).reshape(hist * DB, D_LRU)
    ya_s, conv_s, h_s = _lru_call(x_sample, conv0_s, state_h[0], *lru_params, tt=T)
    spb = S // LRU_TIME_TILE // DB
    by_part = lambda a: jnp.swapaxes(a.reshape(DB, spb, T, D_QKV), 0, 1).reshape(spb, DB * T, D_QKV)
    y_s = _tail_call(xs2, ya_s.reshape(DB * T, D_MODEL), g_norm, wg, b_merge, wap, wout,
                     by_part(o_parts), by_part(lse_parts))

    kv_shape = lambda a: a.reshape(1, a.shape[0], a.shape[1] // KV_ROWS, 2, HEADS_PER_GROUP, HEAD_DIM)
    conv_out = lambda c, nb: jnp.swapaxes(c.reshape(hist, nb, D_LRU), 0, 1)[None]
    news = [n.reshape(c.shape) for n, c in zip(news, caches6)]
    return (y_p, y_s.reshape(DB, T, D_MODEL), kv_shape(kv0_p), kv_shape(kv1_p), kv_shape(kv2_p),
            conv_out(conv_p, B), h_p[None], news[0], news[1], news[2], conv_out(conv_s, DB), h_s[None])
```

```python
import functools
import math

import jax
import jax.numpy as jnp
from jax import lax
from jax.experimental import pallas as pl
from jax.experimental.pallas import tpu as pltpu

F32 = jnp.float32
BF16 = jnp.bfloat16

D_MODEL = 1024
D_LRU = 1024
N_LRU_BLOCKS = 8
LRU_BLOCK = D_LRU // N_LRU_BLOCKS
CONV_WIDTH = 4
LRU_C = 8.0
HEAD_DIM = 128
HEADS_PER_GROUP = 4
WINDOWS = (128, 512, 2048)
DILATIONS = (1, 4, 16)
N_GROUPS = 3
N_BACK = 128
D_QKV = N_GROUPS * HEADS_PER_GROUP * HEAD_DIM
D_GRP = HEADS_PER_GROUP * HEAD_DIM
ATTN_SCALE = HEAD_DIM ** -0.5
N_BUCKETS = 32
MAX_DISTANCE = 2048
NORM_EPS = 1e-6
NEG = -1e30
KEY_WIN = 256
KV_ROWS = 2 * HEADS_PER_GROUP
VMEM_LIMIT = 60000 * 1024
LRU_TIME_TILE = 64
ATTN_ROW_TILE = 256


def _rms(x, g):
    ms = jnp.mean(x * x, axis=-1, keepdims=True)
    return x * lax.rsqrt(ms + NORM_EPS) * g


def _sigmoid(x):
    return 0.5 * jnp.tanh(0.5 * x) + 0.5


def _dot(a, b):
    return jnp.dot(a, b, preferred_element_type=F32)


def _dot_nt(a, b):
    return lax.dot_general(a, b, (((1,), (1,)), ((), ())), preferred_element_type=F32)


def _bias_table(relb_ref, col, dist, valid):
    max_exact = N_BUCKETS // 2
    n_log = N_BUCKETS - max_exact
    df = jnp.maximum(dist, 1).astype(F32)
    val = jnp.log(df / max_exact) / math.log(MAX_DISTANCE / max_exact) * n_log
    out = jnp.zeros(dist.shape, F32)
    for b in range(max_exact):
        out = jnp.where(dist == b, relb_ref[b, col], out)
    for k in range(n_log):
        out = jnp.where((dist >= max_exact) & (val >= k), relb_ref[max_exact + k, col], out)
    return jnp.where(valid, out, NEG)


def _merge_groups(o_refs, lse_refs):
    lses = [r[...] for r in lse_refs]
    m = functools.reduce(jnp.maximum, lses)
    es = [jnp.exp(l - m) for l in lses]
    num = sum(e * r[...] for e, r in zip(es, o_refs))
    return num / sum(es)


N_GATE_CHUNKS = (D_GRP + 2 * D_MODEL) // D_GRP


def _gate_chunk(u, wg_ref, bm_ref, c):
    z = _dot(u, wg_ref[:, c * D_GRP:(c + 1) * D_GRP])
    if c == 0:
        return z * _sigmoid(z)
    return _sigmoid(z + bm_ref[:, (c - 1) * D_GRP:c * D_GRP])


def _tail(x, ya, ob, gate, wap_ref, wout_ref):
    yb = _dot((ob * gate[:, :D_GRP]).astype(BF16), wap_ref[...])
    merged = gate[:, D_GRP:D_GRP + D_MODEL] * ya + gate[:, D_GRP + D_MODEL:] * yb
    return x + _dot(merged.astype(BF16), wout_ref[...])


def _qkv(u, wqkv_ref, gq_ref, gk_ref, store):
    for part, g_ref in enumerate((gq_ref, gk_ref, None)):
        z = _dot(u, wqkv_ref[:, part * D_QKV:(part + 1) * D_QKV])
        zs = [z[:, h * HEAD_DIM:(h + 1) * HEAD_DIM] for h in range(D_QKV // HEAD_DIM)]
        if g_ref is not None:
            ms = [jnp.mean(zh * zh, axis=-1, keepdims=True) for zh in zs]
            zs = [zh * lax.rsqrt(m + NORM_EPS) * g_ref[...] for zh, m in zip(zs, ms)]
        for h, zh in enumerate(zs):
            store(part, h, zh)


N_LRU_IN = 13


def _shift_cache_block(blk_ref, next_ref, fresh_ref, out_hbm, tail_ref, body_sem, tail_sem, batch_rows):
    rpb = blk_ref.shape[0]
    hop = fresh_ref.shape[0]
    r0 = pl.multiple_of(pl.program_id(0) * rpb, rpb)
    ends_batch = ((pl.program_id(0) + 1) * rpb) % batch_rows == 0
    tail_ref[...] = jnp.where(ends_batch, fresh_ref[...], next_ref[...])
    return (pltpu.make_async_copy(blk_ref.at[pl.ds(hop, rpb - hop), :], out_hbm.at[pl.ds(r0, rpb - hop), :], body_sem),
            pltpu.make_async_copy(tail_ref, out_hbm.at[pl.ds(r0 + rpb - hop, hop), :], tail_sem))


def _decode_keys(positions, d, T):
    return positions // d * T if d >= 2 * T else positions


def _decode_attn_part(q_ref, relb_ref, blk_refs, fresh_refs, o_ref, lse_ref, kcs, vcs, bts, spb):
    T = q_ref.shape[0]
    assert T & (T - 1) == 0
    part = pl.program_id(0) % spb
    Lhs = tuple(blk.shape[0] // KV_ROWS for blk in blk_refs)
    Ks = tuple(_decode_keys(Lh, d, T) for Lh, d in zip(Lhs, DILATIONS))

    @pl.when(pl.program_id(0) == 0)
    def _():
        for g in range(N_GROUPS):
            Lh, K, d = Lhs[g], Ks[g], DILATIONS[g]
            L, W = Lh * spb, K + HEAD_DIM
            kcs[g][K:W, :] = jnp.zeros((W - K, D_GRP), BF16)
            vcs[g][K:W, :] = jnp.zeros((W - K, D_GRP), BF16)
            tq = lax.broadcasted_iota(jnp.int32, (T, W), 0)
            c = lax.broadcasted_iota(jnp.int32, (T, W), 1)
            within = c if K == Lh else (c >> (T.bit_length() - 1)) * d + (c & (T - 1))
            for j in range(spb):
                p = jnp.where(c < K, j * Lh + within, L + c - K)
                present = (c < K) | ((c < K + T) & (j == spb - 1))
                dist = L + tq - p
                valid = present & (dist >= 0) & ((dist & (d - 1)) == 0) & (dist <= N_BACK * d)
                for h in range(HEADS_PER_GROUP):
                    bts[g][j, h * T:(h + 1) * T, :] = _bias_table(relb_ref, g * HEADS_PER_GROUP + h, dist, valid)

    heads = [slice(h * HEAD_DIM, (h + 1) * HEAD_DIM) for h in range(HEADS_PER_GROUP)]

    def gather(g):
        Lh, K, d = Lhs[g], Ks[g], DILATIONS[g]
        if K == Lh:
            rows_of = lambda r: blk_refs[g][pl.ds(r, Lh, stride=KV_ROWS), :]
        else:
            periods = blk_refs[g].reshape(Lh // d, d * KV_ROWS, HEAD_DIM)
            rows_of = lambda r: periods[:, pl.ds(r, T, stride=KV_ROWS), :].reshape(K, HEAD_DIM)
        for h, hs_ in enumerate(heads):
            kcs[g][0:K, hs_] = rows_of(h).astype(BF16)
            vcs[g][0:K, hs_] = rows_of(HEADS_PER_GROUP + h).astype(BF16)
        k_new = jnp.concatenate([fresh_refs[g][pl.ds(h, T, stride=KV_ROWS), :] for h in range(HEADS_PER_GROUP)], axis=1)
        v_new = jnp.concatenate([fresh_refs[g][pl.ds(HEADS_PER_GROUP + h, T, stride=KV_ROWS), :]
                                 for h in range(HEADS_PER_GROUP)], axis=1)
        kcs[g][K:K + 2 * T, :] = jnp.concatenate([k_new, jnp.zeros_like(k_new)], axis=0).astype(BF16)
        vcs[g][K:K + 2 * T, :] = jnp.concatenate([v_new, jnp.zeros_like(v_new)], axis=0).astype(BF16)

    def attend(g):
        cols = [slice(g * D_GRP + h * HEAD_DIM, g * D_GRP + (h + 1) * HEAD_DIM) for h in range(HEADS_PER_GROUP)]
        s = jnp.concatenate([_dot_nt(q_ref[:, cs].astype(BF16), kcs[g][:, hs_]) for cs, hs_ in zip(cols, heads)], axis=0)
        s = s * ATTN_SCALE + bts[g][part]
        m = jnp.max(s, axis=-1, keepdims=True)
        p = jnp.exp(s - m)
        den = jnp.sum(p, axis=-1, keepdims=True)
        p = p * (1.0 / den)
        lse = m + jnp.log(den)
        for h, (cs, hs_) in enumerate(zip(cols, heads)):
            rows = slice(h * T, (h + 1) * T)
            o_ref[:, cs] = _dot(p[rows].astype(BF16), vcs[g][:, hs_])
            lse_ref[:, cs] = jnp.broadcast_to(lse[rows], (T, HEAD_DIM))

    return [functools.partial(f, g) for g in range(N_GROUPS) for f in (gather, attend)]


def _lru_body(*refs, B, tt, batch_rows, spb):
    (x_ref, gn_ref, w_ref, cw_ref, cb_ref, wa_ref, ba_ref, wx_ref, bx_ref, lam_ref, wp_ref,
     conv0_ref, h0_ref) = refs[:N_LRU_IN]
    n_shift = len(batch_rows)
    n_dec = 2 if n_shift else 0
    i0 = N_LRU_IN
    shift_in, i0 = refs[i0:i0 + 3 * n_shift], i0 + 3 * n_shift
    dec_in, i0 = refs[i0:i0 + n_dec], i0 + n_dec
    (ya_ref, convo_ref, ho_ref), i0 = refs[i0:i0 + 3], i0 + 3
    shift_out, i0 = refs[i0:i0 + n_shift], i0 + n_shift
    dec_out, i0 = refs[i0:i0 + n_dec], i0 + n_dec
    (xs_ref, ga_ref, a_ref, hs_ref, h_ref), i0 = refs[i0:i0 + 5], i0 + 5
    dec_scratch = refs[i0:]
    R = B * tt
    hist = (CONV_WIDTH - 1) * B

    @pl.when(pl.program_id(0) == 0)
    def _():
        xs_ref[0:hist, :] = conv0_ref[...]
        h_ref[...] = h0_ref[...]

    shift_copies, decode_stages = [], []
    if n_shift:
        kcs, vcs = dec_scratch[0:2 * n_shift:2], dec_scratch[1:2 * n_shift:2]
        bts, tails, sem = dec_scratch[2 * n_shift:3 * n_shift], dec_scratch[3 * n_shift:4 * n_shift], dec_scratch[-1]
        for k in range(n_shift):
            shift_copies += _shift_cache_block(*shift_in[3 * k:3 * k + 3], shift_out[k], tails[k], sem.at[2 * k],
                                               sem.at[2 * k + 1], batch_rows[k])
        decode_stages = _decode_attn_part(*dec_in, shift_in[0::3], shift_in[2::3], *dec_out, kcs, vcs, bts, spb)
    for cp in shift_copies:
        cp.start()
    assert len(decode_stages) <= N_LRU_BLOCKS

    xt = jnp.concatenate([x_ref[:, t, :] for t in range(tt)], axis=0)
    u = _rms(xt, gn_ref[...]).astype(BF16)
    xs_ref[hist:hist + R, :] = _dot(u, w_ref[:, :D_LRU])
    ga_ref[...] = _dot(u, w_ref[:, D_LRU:])
    sp = jax.nn.softplus(-lam_ref[...])
    for n in range(N_LRU_BLOCKS):
        if n < len(decode_stages):
            decode_stages[n]()
        cs = slice(n * LRU_BLOCK, (n + 1) * LRU_BLOCK)
        y = cb_ref[:, cs] + sum(xs_ref[j * B:j * B + R, cs] * cw_ref[j:j + 1, cs] for j in range(CONV_WIDTH))
        yb = y.astype(BF16)
        r = _sigmoid(_dot(yb, wa_ref[n]) + ba_ref[:, cs])
        i = _sigmoid(_dot(yb, wx_ref[n]) + bx_ref[:, cs])
        log_a = -LRU_C * r * sp[:, cs]
        a = jnp.exp(log_a)
        th = jnp.tanh(log_a)
        a_ref[:, cs] = a
        hs_ref[:, cs] = jnp.sqrt(-2.0 * th) * lax.rsqrt(1.0 - th) * (i * y)
    h = h_ref[...]
    for t in range(tt):
        rows = slice(t * B, (t + 1) * B)
        h = a_ref[rows, :] * h + hs_ref[rows, :]
        hs_ref[rows, :] = h
    h_ref[...] = h
    g = ga_ref[...]
    ya = _dot((hs_ref[...] * (g * _sigmoid(g))).astype(BF16), wp_ref[...])
    for t in range(tt):
        ya_ref[:, t, :] = ya[t * B:(t + 1) * B].astype(ya_ref.dtype)
    tail = xs_ref[R:R + hist, :]
    convo_ref[...] = tail
    xs_ref[0:hist, :] = tail
    ho_ref[...] = h_ref[...]
    for cp in shift_copies:
        cp.wait()


def _lru_call(x, conv0, h0, gn, w_lru, cw, cb, wa, ba, wx, bx, lam, wp, *, tt, caches=(), fresh=(), n_cache_batch=1,
              q_dec=None, relb=None):
    B, S, _ = x.shape
    assert S % tt == 0 and tt >= CONV_WIDTH - 1 and tt % 8 == 0 and B % 8 == 0
    R = B * tt
    hist = (CONV_WIDTH - 1) * B
    steps = S // tt
    spb = steps // n_cache_batch
    const = lambda shape: pl.BlockSpec(shape, lambda i: (0,) * len(shape), pipeline_mode=pl.Buffered(1))
    shift_in, shift_out, shift_shapes, batch_rows = [], [], [], []
    for c, f in zip(caches, fresh):
        total = c.shape[0]
        rpb = total // steps
        rows_b = total // n_cache_batch
        hop = f.shape[0] // n_cache_batch
        assert total % steps == 0 and rows_b == spb * rpb and rpb % hop == 0 and rpb > hop
        shift_in += [
            pl.BlockSpec((rpb, HEAD_DIM), lambda i: (i, 0)),
            pl.BlockSpec((hop, HEAD_DIM), lambda i, rpb=rpb, hop=hop, total=total:
                         (jnp.minimum((i + 1) * (rpb // hop), total // hop - 1), 0)),
            pl.BlockSpec((hop, HEAD_DIM), lambda i, rpb=rpb, rows_b=rows_b: ((i * rpb) // rows_b, 0)),
        ]
        shift_out.append(pl.BlockSpec(memory_space=pl.ANY))
        shift_shapes.append(jax.ShapeDtypeStruct(c.shape, c.dtype))
        batch_rows.append(rows_b)
    shift_args = [a for c, f in zip(caches, fresh) for a in (c, c, f)]
    dec_in, dec_out, dec_shapes, dec_scratch = [], [], [], []
    if caches:
        assert len(caches) == N_GROUPS and q_dec.shape[0] % n_cache_batch == 0
        T = q_dec.shape[0] // n_cache_batch
        part = jax.ShapeDtypeStruct((steps * T, D_QKV), F32)
        dec_in = [pl.BlockSpec((T, D_QKV), lambda i: (i // spb, 0)), pl.BlockSpec(memory_space=pltpu.SMEM)]
        dec_out = [pl.BlockSpec((T, D_QKV), lambda i: (i, 0))] * 2
        dec_shapes = [part, part]
        shift_args += [q_dec, relb]
        assert T % 8 == 0
        widths = [_decode_keys(c.shape[0] // steps // KV_ROWS, d, T) + HEAD_DIM for c, d in zip(caches, DILATIONS)]
        for w in widths:
            dec_scratch += [pltpu.VMEM((w, D_GRP), BF16)] * 2
        dec_scratch += [pltpu.VMEM((spb, HEADS_PER_GROUP * T, w), F32) for w in widths]
        dec_scratch += [pltpu.VMEM((f.shape[0] // n_cache_batch, HEAD_DIM), F32) for f in fresh]
        dec_scratch += [pltpu.SemaphoreType.DMA((2 * len(caches),))]
    return pl.pallas_call(
        functools.partial(_lru_body, B=B, tt=tt, batch_rows=tuple(batch_rows), spb=spb),
        grid=(steps,),
        in_specs=[
            pl.BlockSpec((B, tt, D_MODEL), lambda i: (0, i, 0)),
            const((1, D_MODEL)), const((D_MODEL, 2 * D_LRU)), const((CONV_WIDTH, D_LRU)), const((1, D_LRU)),
            const((N_LRU_BLOCKS, LRU_BLOCK, LRU_BLOCK)), const((1, D_LRU)),
            const((N_LRU_BLOCKS, LRU_BLOCK, LRU_BLOCK)), const((1, D_LRU)), const((1, D_LRU)),
            const((D_LRU, D_MODEL)), const((hist, D_LRU)), const((B, D_LRU)),
        ] + shift_in + dec_in,
        out_specs=[
            pl.BlockSpec((B, tt, D_MODEL), lambda i: (0, i, 0)),
            pl.BlockSpec((hist, D_LRU), lambda i: (0, 0)),
            pl.BlockSpec((B, D_LRU), lambda i: (0, 0)),
        ] + shift_out + dec_out,
        out_shape=[
            jax.ShapeDtypeStruct((B, S, D_MODEL), F32),
            jax.ShapeDtypeStruct((hist, D_LRU), F32),
            jax.ShapeDtypeStruct((B, D_LRU), F32),
        ] + shift_shapes + dec_shapes,
        scratch_shapes=[
            pltpu.VMEM((hist + R, D_LRU), F32),
            pltpu.VMEM((R, D_LRU), F32),
            pltpu.VMEM((R, D_LRU), F32),
            pltpu.VMEM((R, D_LRU), F32),
            pltpu.VMEM((B, D_LRU), F32),
        ] + dec_scratch,
        compiler_params=pltpu.CompilerParams(dimension_semantics=("arbitrary",), vmem_limit_bytes=VMEM_LIMIT),
        name="lru_branch",
    )(x, gn, w_lru, cw, cb, wa, ba, wx, bx, lam, wp, conv0, h0, *shift_args)


def _attn_body(x_ref, ya_ref, gn_ref, wqkv_ref, wg_ref, gq_ref, gk_ref, bm_ref, relb_ref, wap_ref, wout_ref,
               y_ref, kv0_ref, kv1_ref, kv2_ref,
               zq_ref, zk_ref, zv_ref, qc0, qc1, qc2, hk0, hv0, hk1, hv1, hk2, hv2,
               o0, o1, o2, l0, l1, l2, bt0, bt1, bt2, s_ref, p_ref, *, S, tq):
    t = pl.program_id(1)
    nt = S // tq

    qcs, hks, hvs = (qc0, qc1, qc2), (hk0, hk1, hk2), (hv0, hv1, hv2)
    o_refs, l_refs, bts, kvs = (o0, o1, o2), (l0, l1, l2), (bt0, bt1, bt2), (kv0_ref, kv1_ref, kv2_ref)
    Qc = tuple(tq // d for d in DILATIONS)
    QB = tuple(min(q, N_BACK) for q in Qc)
    pad1 = KEY_WIN - QB[1]
    n2 = S // DILATIONS[2]

    @pl.when((pl.program_id(0) == 0) & (t == 0))
    def _():
        for ref in (hk0, hv0, hk1, hv1, hk2, hv2):
            ref[...] = jnp.zeros(ref.shape, ref.dtype)
        for g in (0, 1):
            a = lax.broadcasted_iota(jnp.int32, (QB[g], KEY_WIN), 0)
            c = lax.broadcasted_iota(jnp.int32, (QB[g], KEY_WIN), 1)
            j = a + (KEY_WIN - QB[g]) - c
            for h in range(HEADS_PER_GROUP):
                bts[g][h] = _bias_table(relb_ref, g * HEADS_PER_GROUP + h, j * DILATIONS[g], (j >= 0) & (j <= N_BACK))
        i = lax.broadcasted_iota(jnp.int32, (n2, n2), 0)
        c = lax.broadcasted_iota(jnp.int32, (n2, n2), 1)
        for h in range(HEADS_PER_GROUP):
            bt2[h] = _bias_table(relb_ref, 2 * HEADS_PER_GROUP + h, (i - c) * DILATIONS[2],
                                 (i - c >= 0) & (i - c <= N_BACK))

    x = x_ref[...]
    u = _rms(x, gn_ref[...]).astype(BF16)
    z_refs = (zq_ref, zk_ref, zv_ref)

    def store(part, head, value):
        z_refs[part][head] = value

    _qkv(u, wqkv_ref, gq_ref, gk_ref, store)

    for g in range(N_GROUPS):
        keep = min(WINDOWS[g], S)
        rb = min(tq, keep)

        @pl.when(t >= nt - keep // rb)
        def _(g=g, rb=rb):
            for part, z_ref in enumerate((zk_ref, zv_ref)):
                for h in range(HEADS_PER_GROUP):
                    dst = pl.ds(part * HEADS_PER_GROUP + h, rb, stride=KV_ROWS)
                    kvs[g][dst, :] = z_ref[g * HEADS_PER_GROUP + h, tq - rb:tq, :]

    for g in range(N_GROUPS):
        d = DILATIONS[g]
        for r in range(d):
            rows = pl.ds(r, Qc[g], stride=d) if d > 1 else slice(None)
            if g == 0:
                dst = pl.ds(tq, tq)
            elif g == 1:
                dst = pl.ds(pl.multiple_of(pad1 + t * Qc[1], Qc[1]), Qc[1])
            else:
                dst = pl.ds(pl.multiple_of(t * Qc[2], Qc[2]), Qc[2])
            for h in range(HEADS_PER_GROUP):
                hs_ = slice(h * HEAD_DIM, (h + 1) * HEAD_DIM)
                gh = g * HEADS_PER_GROUP + h
                qcs[g][r, :, hs_] = zq_ref[gh, rows, :].astype(BF16)
                hks[g][r, dst, hs_] = zk_ref[gh, rows, :].astype(BF16)
                hvs[g][r, dst, hs_] = zv_ref[gh, rows, :].astype(BF16)

    col = lax.broadcasted_iota(jnp.int32, (1, KEY_WIN), 1)
    rowi = lax.broadcasted_iota(jnp.int32, (tq, 1), 0)
    for g in range(N_GROUPS):
        d = DILATIONS[g]
        nq = Qc[g] // QB[g]
        kw = n2 if g == 2 else KEY_WIN
        units = [(r, qb) for r in range(d) for qb in range(nq)]

        def key_win(qb, g=g):
            if g == 0:
                return pl.ds(tq + qb * QB[0] + QB[0] - KEY_WIN, KEY_WIN)
            if g == 1:
                return pl.ds(pl.multiple_of(t * Qc[1], QB[1]), KEY_WIN)
            return slice(None)

        for r, qb in units:
            cm = slice(r * Qc[g] + qb * QB[g], r * Qc[g] + (qb + 1) * QB[g])
            for h in range(HEADS_PER_GROUP):
                hs_ = slice(h * HEAD_DIM, (h + 1) * HEAD_DIM)
                s_ref[h, cm, 0:kw] = _dot_nt(qcs[g][r, qb * QB[g]:(qb + 1) * QB[g], hs_], hks[g][r, key_win(qb), hs_])

        if g < 2:
            i0 = t * Qc[g] + (rowi & (Qc[g] - QB[g]))
            started = col >= KEY_WIN - QB[g] - i0
        for h in range(HEADS_PER_GROUP):
            bias = bt2[h, pl.ds(pl.multiple_of(t * QB[2], QB[2]), QB[2]), :] if g == 2 else bts[g][h]
            s = s_ref[h, :, 0:kw] * ATTN_SCALE
            s = (s.reshape(len(units), QB[g], kw) + bias[None]).reshape(tq, kw)
            if g < 2:
                s = jnp.where(started, s, NEG)
            m = jnp.max(s, axis=-1, keepdims=True)
            p = jnp.exp(s - m)
            den = jnp.sum(p, axis=-1, keepdims=True)
            p_ref[h, :, 0:kw] = (p * (1.0 / den)).astype(BF16)
            lse = jnp.broadcast_to(m + jnp.log(den), (tq, HEAD_DIM))
            for r in range(d):
                rows = pl.ds(r, Qc[g], stride=d) if d > 1 else slice(None)
                l_refs[g][h, rows, :] = lse[r * Qc[g]:(r + 1) * Qc[g]]

        for r, qb in units:
            cm = slice(r * Qc[g] + qb * QB[g], r * Qc[g] + (qb + 1) * QB[g])
            rows = pl.ds(r + d * qb * QB[g], QB[g], stride=d) if d > 1 else pl.ds(qb * QB[g], QB[g])
            for h in range(HEADS_PER_GROUP):
                hs_ = slice(h * HEAD_DIM, (h + 1) * HEAD_DIM)
                o_refs[g][h, rows, :] = _dot(p_ref[h, cm, 0:kw], hvs[g][r, key_win(qb), hs_])

    hk0[0, 0:tq, :] = hk0[0, tq:2 * tq, :]
    hv0[0, 0:tq, :] = hv0[0, tq:2 * tq, :]

    ob = _merge_groups(o_refs, l_refs)
    ob = jnp.concatenate([ob[h] for h in range(HEADS_PER_GROUP)], axis=1)
    gate = jnp.concatenate([_gate_chunk(u, wg_ref, bm_ref, c) for c in range(N_GATE_CHUNKS)], axis=1)
    y_ref[...] = _tail(x, ya_ref[...].astype(F32), ob, gate, wap_ref, wout_ref)


def _attn_call(x, ya, gn, wqkv, wg, gq, gk, bmerge, relb, wap, wout, *, tq):
    B, S, _ = x.shape
    nt = S // tq
    assert S % tq == 0 and tq % (DILATIONS[2] * 16) == 0 and tq >= N_BACK
    assert S // DILATIONS[2] == N_BACK and tq // DILATIONS[1] <= N_BACK
    Qc = tuple(tq // d for d in DILATIONS)
    QB = tuple(min(q, N_BACK) for q in Qc)
    n2 = S // DILATIONS[2]
    const = lambda shape: pl.BlockSpec(shape, lambda b, t: (0,) * len(shape), pipeline_mode=pl.Buffered(1))
    row = pl.BlockSpec((None, tq, D_MODEL), lambda b, t: (b, t, 0))

    def kv_spec(g):
        keep = min(WINDOWS[g], S)
        rb = min(tq, keep)
        first = nt - keep // rb
        return pl.BlockSpec((None, rb * KV_ROWS, HEAD_DIM), lambda b, t: (b, jnp.maximum(t - first, 0), 0))

    cls = lambda g, rows, dt: pltpu.VMEM((DILATIONS[g], rows, D_GRP), dt)
    scratch = [pltpu.VMEM((D_QKV // HEAD_DIM, tq, HEAD_DIM), F32)] * 3
    scratch += [cls(g, Qc[g], BF16) for g in range(N_GROUPS)]
    scratch += [cls(0, 2 * tq, BF16)] * 2 + [cls(1, KEY_WIN - QB[1] + S // DILATIONS[1], BF16)] * 2 + [cls(2, n2, BF16)] * 2
    scratch += [pltpu.VMEM((HEADS_PER_GROUP, tq, HEAD_DIM), F32)] * 6
    scratch += [pltpu.VMEM((HEADS_PER_GROUP, QB[0], KEY_WIN), F32), pltpu.VMEM((HEADS_PER_GROUP, QB[1], KEY_WIN), F32),
                pltpu.VMEM((HEADS_PER_GROUP, n2, n2), F32)]
    scratch += [pltpu.VMEM((HEADS_PER_GROUP, tq, KEY_WIN), F32), pltpu.VMEM((HEADS_PER_GROUP, tq, KEY_WIN), BF16)]
    return pl.pallas_call(
        functools.partial(_attn_body, S=S, tq=tq),
        grid=(B, nt),
        in_specs=[row, row, const((1, D_MODEL)), const((D_MODEL, 3 * D_QKV)), const((D_MODEL, D_GRP + 2 * D_MODEL)),
                  const((1, HEAD_DIM)), const((1, HEAD_DIM)), const((1, 2 * D_MODEL)),
                  pl.BlockSpec(memory_space=pltpu.SMEM), const((D_GRP, D_MODEL)), const((D_MODEL, D_MODEL))],
        out_specs=[row, kv_spec(0), kv_spec(1), kv_spec(2)],
        out_shape=[jax.ShapeDtypeStruct((B, S, D_MODEL), F32)]
        + [jax.ShapeDtypeStruct((B, min(WINDOWS[g], S) * KV_ROWS, HEAD_DIM), F32) for g in range(N_GROUPS)],
        scratch_shapes=scratch,
        compiler_params=pltpu.CompilerParams(dimension_semantics=("arbitrary", "arbitrary"), vmem_limit_bytes=VMEM_LIMIT),
        name="attn_branch",
    )(x, ya, gn, wqkv, wg, gq, gk, bmerge, relb, wap, wout)


def _qkv_body(x_ref, gn_ref, wqkv_ref, gq_ref, gk_ref, zq_ref, nr0_ref, nr1_ref, nr2_ref):
    M = x_ref.shape[0]
    u = _rms(x_ref[...], gn_ref[...]).astype(BF16)
    nr_refs = (nr0_ref, nr1_ref, nr2_ref)

    def store(part, head, value):
        if part == 0:
            zq_ref[:, head * HEAD_DIM:(head + 1) * HEAD_DIM] = value
        else:
            g, h = divmod(head, HEADS_PER_GROUP)
            nr_refs[g][pl.ds((part - 1) * HEADS_PER_GROUP + h, M, stride=KV_ROWS), :] = value

    _qkv(u, wqkv_ref, gq_ref, gk_ref, store)


def _qkv_call(x, gn, wqkv, gq, gk):
    M = x.shape[0]
    rows = jax.ShapeDtypeStruct((M * KV_ROWS, HEAD_DIM), F32)
    return pl.pallas_call(_qkv_body, out_shape=[jax.ShapeDtypeStruct((M, D_QKV), F32), rows, rows, rows],
                          compiler_params=pltpu.CompilerParams(vmem_limit_bytes=VMEM_LIMIT),
                          name="decode_qkv")(x, gn, wqkv, gq, gk)


def _tail_body(x_ref, ya_ref, gn_ref, wg_ref, bm_ref, wap_ref, wout_ref, o_ref, lse_ref, y_ref):
    x = x_ref[...]
    u = _rms(x, gn_ref[...]).astype(BF16)
    pieces = [(j, slice(g * D_GRP, (g + 1) * D_GRP)) for j in range(o_ref.shape[0]) for g in range(N_GROUPS)]
    ob = _merge_groups([o_ref.at[j, :, gs] for j, gs in pieces], [lse_ref.at[j, :, gs] for j, gs in pieces])
    gate = jnp.concatenate([_gate_chunk(u, wg_ref, bm_ref, c) for c in range(N_GATE_CHUNKS)], axis=1)
    y_ref[...] = _tail(x, ya_ref[...], ob, gate, wap_ref, wout_ref)


def _tail_call(x, ya, gn, wg, bmerge, wap, wout, o, lse):
    return pl.pallas_call(_tail_body, out_shape=jax.ShapeDtypeStruct(x.shape, F32),
                          compiler_params=pltpu.CompilerParams(vmem_limit_bytes=VMEM_LIMIT),
                          name="decode_tail")(x, ya, gn, wg, bmerge, wap, wout, o, lse)


def kernel(x_prompt, x_sample, cache_kv_w128, cache_kv_w512, cache_kv_w2048, state_conv, state_h, g_norm, w_in,
           b_merge, conv_w, conv_b, lru_w_a, lru_b_a, lru_w_x, lru_b_x, lru_lambda, g_q, g_k, rel_bias,
           w_lru_proj, w_attn_proj, w_out):
    assert w_in.shape[0] == 1, "single-layer step"
    B, S, _ = x_prompt.shape
    DB, T, _ = x_sample.shape
    o2 = 2 * D_LRU
    o5 = o2 + 3 * D_QKV
    w_all = w_in[0].astype(BF16)
    wqkv = w_all[:, o2:o5]
    wg = w_all[:, o5:]
    wa, wx = lru_w_a[0].astype(BF16), lru_w_x[0].astype(BF16)
    wp, wap, wout = w_lru_proj[0].astype(BF16), w_attn_proj[0].astype(BF16), w_out[0].astype(BF16)
    lru_params = (g_norm, w_all, conv_w[0], conv_b, wa, lru_b_a, wx, lru_b_x, lru_lambda, wp)
    hist = CONV_WIDTH - 1

    caches6 = (cache_kv_w128, cache_kv_w512, cache_kv_w2048)
    caches = [c.reshape(-1, HEAD_DIM) for c in caches6]

    xs2 = x_sample.reshape(DB * T, D_MODEL)
    zq, *newrows = _qkv_call(xs2, g_norm, wqkv, g_q, g_k)

    ya_p, conv_p, h_p, *rest = _lru_call(x_prompt, jnp.zeros((hist * B, D_LRU), F32), jnp.zeros((B, D_LRU), F32),
                                         *lru_params, tt=LRU_TIME_TILE, caches=caches, fresh=newrows,
                                         n_cache_batch=DB, q_dec=zq, relb=rel_bias)
    news, (o_parts, lse_parts) = rest[:N_GROUPS], rest[N_GROUPS:]
    y_p, kv0_p, kv1_p, kv2_p = _attn_call(x_prompt, ya_p, g_norm, wqkv, wg, g_q, g_k, b_merge, rel_bias, wap, wout,
                                          tq=ATTN_ROW_TILE)

    conv0_s = jnp.swapaxes(state_conv[0], 0, 1).reshape(hist * DB, D_LRU)
    ya_s, conv_s, h_s = _lru_call(x_sample, conv0_s, state_h[0], *lru_params, tt=T)
    spb = S // LRU_TIME_TILE // DB
    by_part = lambda a: jnp.swapaxes(a.reshape(DB, spb, T, D_QKV), 0, 1).reshape(spb, DB * T, D_QKV)
    y_s = _tail_call(xs2, ya_s.reshape(DB * T, D_MODEL), g_norm, wg, b_merge, wap, wout,
                     by_part(o_parts), by_part(lse_parts))

    kv_shape = lambda a: a.reshape(1, a.shape[0], a.shape[1] // KV_ROWS, 2, HEADS_PER_GROUP, HEAD_DIM)
    conv_out = lambda c, nb: jnp.swapaxes(c.reshape(hist, nb, D_LRU), 0, 1)[None]
    news = [n.reshape(c.shape) for n, c in zip(news, caches6)]
    return (y_p, y_s.reshape(DB, T, D_MODEL), kv_shape(kv0_p), kv_shape(kv1_p), kv_shape(kv2_p),
            conv_out(conv_p, B), h_p[None], news[0], news[1], news[2], conv_out(conv_s, DB), h_s[None])
```

```python
import functools
import math

import jax
import jax.numpy as jnp
from jax import lax
from jax.experimental import pallas as pl
from jax.experimental.pallas import tpu as pltpu

F32 = jnp.float32
BF16 = jnp.bfloat16

D_MODEL = 1024
D_LRU = 1024
N_LRU_BLOCKS = 8
LRU_BLOCK = D_LRU // N_LRU_BLOCKS
CONV_WIDTH = 4
LRU_C = 8.0
HEAD_DIM = 128
HEADS_PER_GROUP = 4
WINDOWS = (128, 512, 2048)
DILATIONS = (1, 4, 16)
N_GROUPS = 3
N_BACK = 128
D_QKV = N_GROUPS * HEADS_PER_GROUP * HEAD_DIM
D_GRP = HEADS_PER_GROUP * HEAD_DIM
ATTN_SCALE = HEAD_DIM ** -0.5
N_BUCKETS = 32
MAX_DISTANCE = 2048
NORM_EPS = 1e-6
NEG = -1e30
KEY_WIN = 256
KV_ROWS = 2 * HEADS_PER_GROUP
VMEM_LIMIT = 60000 * 1024
LRU_TIME_TILE = 64
ATTN_ROW_TILE = 256


def _rms(x, g):
    ms = jnp.mean(x * x, axis=-1, keepdims=True)
    return x * lax.rsqrt(ms + NORM_EPS) * g


def _sigmoid(x):
    return 0.5 * jnp.tanh(0.5 * x) + 0.5


def _dot(a, b):
    return jnp.dot(a, b, preferred_element_type=F32)


def _dot_nt(a, b):
    return lax.dot_general(a, b, (((1,), (1,)), ((), ())), preferred_element_type=F32)


def _bias_table(relb_ref, col, dist, valid):
    max_exact = N_BUCKETS // 2
    n_log = N_BUCKETS - max_exact
    df = jnp.maximum(dist, 1).astype(F32)
    val = jnp.log(df / max_exact) / math.log(MAX_DISTANCE / max_exact) * n_log
    out = jnp.zeros(dist.shape, F32)
    for b in range(max_exact):
        out = jnp.where(dist == b, relb_ref[b, col], out)
    for k in range(n_log):
        out = jnp.where((dist >= max_exact) & (val >= k), relb_ref[max_exact + k, col], out)
    return jnp.where(valid, out, NEG)


def _merge_groups(o_refs, lse_refs):
    lses = [r[...] for r in lse_refs]
    m = functools.reduce(jnp.maximum, lses)
    es = [jnp.exp(l - m) for l in lses]
    num = sum(e * r[...] for e, r in zip(es, o_refs))
    return num / sum(es)


N_GATE_CHUNKS = (D_GRP + 2 * D_MODEL) // D_GRP


def _gate_chunk(u, wg_ref, bm_ref, c):
    z = _dot(u, wg_ref[:, c * D_GRP:(c + 1) * D_GRP])
    if c == 0:
        return z * _sigmoid(z)
    return _sigmoid(z + bm_ref[:, (c - 1) * D_GRP:c * D_GRP])


def _tail(x, ya, ob, gate, wap_ref, wout_ref):
    yb = _dot((ob * gate[:, :D_GRP]).astype(BF16), wap_ref[...])
    merged = gate[:, D_GRP:D_GRP + D_MODEL] * ya + gate[:, D_GRP + D_MODEL:] * yb
    return x + _dot(merged.astype(BF16), wout_ref[...])


def _qkv(u, wqkv_ref, gq_ref, gk_ref, store):
    for part, g_ref in enumerate((gq_ref, gk_ref, None)):
        z = _dot(u, wqkv_ref[:, part * D_QKV:(part + 1) * D_QKV])
        zs = [z[:, h * HEAD_DIM:(h + 1) * HEAD_DIM] for h in range(D_QKV // HEAD_DIM)]
        if g_ref is not None:
            ms = [jnp.mean(zh * zh, axis=-1, keepdims=True) for zh in zs]
            zs = [zh * lax.rsqrt(m + NORM_EPS) * g_ref[...] for zh, m in zip(zs, ms)]
        for h, zh in enumerate(zs):
            store(part, h, zh)


N_LRU_IN = 13


def _shift_cache_block(blk_ref, next_ref, fresh_ref, out_hbm, tail_ref, body_sem, tail_sem, batch_rows):
    rpb = blk_ref.shape[0]
    hop = fresh_ref.shape[0]
    r0 = pl.multiple_of(pl.program_id(0) * rpb, rpb)
    ends_batch = ((pl.program_id(0) + 1) * rpb) % batch_rows == 0
    tail_ref[...] = jnp.where(ends_batch, fresh_ref[...], next_ref[...])
    return (pltpu.make_async_copy(blk_ref.at[pl.ds(hop, rpb - hop), :], out_hbm.at[pl.ds(r0, rpb - hop), :], body_sem),
            pltpu.make_async_copy(tail_ref, out_hbm.at[pl.ds(r0 + rpb - hop, hop), :], tail_sem))


def _decode_keys(positions, d, T):
    return positions // d * T if d >= 2 * T else positions


def _decode_attn_part(q_ref, relb_ref, blk_refs, fresh_refs, o_ref, lse_ref, kcs, vcs, bts, spb):
    T = q_ref.shape[0]
    assert T & (T - 1) == 0
    part = pl.program_id(0) % spb
    Lhs = tuple(blk.shape[0] // KV_ROWS for blk in blk_refs)
    Ks = tuple(_decode_keys(Lh, d, T) for Lh, d in zip(Lhs, DILATIONS))

    @pl.when(pl.program_id(0) == 0)
    def _():
        for g in range(N_GROUPS):
            Lh, K, d = Lhs[g], Ks[g], DILATIONS[g]
            L, W = Lh * spb, K + HEAD_DIM
            kcs[g][K:W, :] = jnp.zeros((W - K, D_GRP), BF16)
            vcs[g][K:W, :] = jnp.zeros((W - K, D_GRP), BF16)
            tq = lax.broadcasted_iota(jnp.int32, (T, W), 0)
            c = lax.broadcasted_iota(jnp.int32, (T, W), 1)
            within = c if K == Lh else (c >> (T.bit_length() - 1)) * d + (c & (T - 1))
            for j in range(spb):
                p = jnp.where(c < K, j * Lh + within, L + c - K)
                present = (c < K) | ((c < K + T) & (j == spb - 1))
                dist = L + tq - p
                valid = present & (dist >= 0) & ((dist & (d - 1)) == 0) & (dist <= N_BACK * d)
                for h in range(HEADS_PER_GROUP):
                    bts[g][j, h * T:(h + 1) * T, :] = _bias_table(relb_ref, g * HEADS_PER_GROUP + h, dist, valid)

    heads = [slice(h * HEAD_DIM, (h + 1) * HEAD_DIM) for h in range(HEADS_PER_GROUP)]

    def gather(g):
        Lh, K, d = Lhs[g], Ks[g], DILATIONS[g]
        if K == Lh:
            rows_of = lambda r: blk_refs[g][pl.ds(r, Lh, stride=KV_ROWS), :]
        else:
            periods = blk_refs[g].reshape(Lh // d, d * KV_ROWS, HEAD_DIM)
            rows_of = lambda r: periods[:, pl.ds(r, T, stride=KV_ROWS), :].reshape(K, HEAD_DIM)
        for h, hs_ in enumerate(heads):
            kcs[g][0:K, hs_] = rows_of(h).astype(BF16)
            vcs[g][0:K, hs_] = rows_of(HEADS_PER_GROUP + h).astype(BF16)
        k_new = jnp.concatenate([fresh_refs[g][pl.ds(h, T, stride=KV_ROWS), :] for h in range(HEADS_PER_GROUP)], axis=1)
        v_new = jnp.concatenate([fresh_refs[g][pl.ds(HEADS_PER_GROUP + h, T, stride=KV_ROWS), :]
                                 for h in range(HEADS_PER_GROUP)], axis=1)
        kcs[g][K:K + 2 * T, :] = jnp.concatenate([k_new, jnp.zeros_like(k_new)], axis=0).astype(BF16)
        vcs[g][K:K + 2 * T, :] = jnp.concatenate([v_new, jnp.zeros_like(v_new)], axis=0).astype(BF16)

    def attend(g):
        cols = [slice(g * D_GRP + h * HEAD_DIM, g * D_GRP + (h + 1) * HEAD_DIM) for h in range(HEADS_PER_GROUP)]
        s = jnp.concatenate([_dot_nt(q_ref[:, cs].astype(BF16), kcs[g][:, hs_]) for cs, hs_ in zip(cols, heads)], axis=0)
        s = s * ATTN_SCALE + bts[g][part]
        m = jnp.max(s, axis=-1, keepdims=True)
        p = jnp.exp(s - m)
        den = jnp.sum(p, axis=-1, keepdims=True)
        p = p * (1.0 / den)
        lse = m + jnp.log(den)
        for h, (cs, hs_) in enumerate(zip(cols, heads)):
            rows = slice(h * T, (h + 1) * T)
            o_ref[:, cs] = _dot(p[rows].astype(BF16), vcs[g][:, hs_])
            lse_ref[:, cs] = jnp.broadcast_to(lse[rows], (T, HEAD_DIM))

    return [functools.partial(f, g) for g in range(N_GROUPS) for f in (gather, attend)]


def _lru_body(*refs, B, tt, batch_rows, spb):
    (x_ref, gn_ref, w_ref, cw_ref, cb_ref, wa_ref, ba_ref, wx_ref, bx_ref, lam_ref, wp_ref,
     conv0_ref, h0_ref) = refs[:N_LRU_IN]
    n_shift = len(batch_rows)
    n_dec = 2 if n_shift else 0
    i0 = N_LRU_IN
    shift_in, i0 = refs[i0:i0 + 3 * n_shift], i0 + 3 * n_shift
    dec_in, i0 = refs[i0:i0 + n_dec], i0 + n_dec
    (ya_ref, convo_ref, ho_ref), i0 = refs[i0:i0 + 3], i0 + 3
    shift_out, i0 = refs[i0:i0 + n_shift], i0 + n_shift
    dec_out, i0 = refs[i0:i0 + n_dec], i0 + n_dec
    (xs_ref, ga_ref, a_ref, hs_ref, h_ref), i0 = refs[i0:i0 + 5], i0 + 5
    dec_scratch = refs[i0:]
    R = B * tt
    hist = (CONV_WIDTH - 1) * B

    @pl.when(pl.program_id(0) == 0)
    def _():
        xs_ref[0:hist, :] = conv0_ref[...]
        h_ref[...] = h0_ref[...]

    shift_copies, decode_stages = [], []
    if n_shift:
        kcs, vcs = dec_scratch[0:2 * n_shift:2], dec_scratch[1:2 * n_shift:2]
        bts, tails, sem = dec_scratch[2 * n_shift:3 * n_shift], dec_scratch[3 * n_shift:4 * n_shift], dec_scratch[-1]
        for k in range(n_shift):
            shift_copies += _shift_cache_block(*shift_in[3 * k:3 * k + 3], shift_out[k], tails[k], sem.at[2 * k],
                                               sem.at[2 * k + 1], batch_rows[k])
        decode_stages = _decode_attn_part(*dec_in, shift_in[0::3], shift_in[2::3], *dec_out, kcs, vcs, bts, spb)
    for cp in shift_copies:
        cp.start()
    assert len(decode_stages) <= N_LRU_BLOCKS

    xt = jnp.concatenate([x_ref[:, t, :] for t in range(tt)], axis=0)
    u = _rms(xt, gn_ref[...]).astype(BF16)
    xs_ref[hist:hist + R, :] = _dot(u, w_ref[:, :D_LRU])
    ga_ref[...] = _dot(u, w_ref[:, D_LRU:])
    sp = jax.nn.softplus(-lam_ref[...])
    for n in range(N_LRU_BLOCKS):
        if n < len(decode_stages):
            decode_stages[n]()
        cs = slice(n * LRU_BLOCK, (n + 1) * LRU_BLOCK)
        y = cb_ref[:, cs] + sum(xs_ref[j * B:j * B + R, cs] * cw_ref[j:j + 1, cs] for j in range(CONV_WIDTH))
        yb = y.astype(BF16)
        r = _sigmoid(_dot(yb, wa_ref[n]) + ba_ref[:, cs])
        i = _sigmoid(_dot(yb, wx_ref[n]) + bx_ref[:, cs])
        log_a = -LRU_C * r * sp[:, cs]
        a = jnp.exp(log_a)
        th = jnp.tanh(log_a)
        a_ref[:, cs] = a
        hs_ref[:, cs] = jnp.sqrt(-2.0 * th) * lax.rsqrt(1.0 - th) * (i * y)
    h = h_ref[...]
    for t in range(tt):
        rows = slice(t * B, (t + 1) * B)
        h = a_ref[rows, :] * h + hs_ref[rows, :]
        hs_ref[rows, :] = h
    h_ref[...] = h
    g = ga_ref[...]
    ya = _dot((hs_ref[...] * (g * _sigmoid(g))).astype(BF16), wp_ref[...])
    for t in range(tt):
        ya_ref[:, t, :] = ya[t * B:(t + 1) * B].astype(ya_ref.dtype)
    tail = xs_ref[R:R + hist, :]
    convo_ref[...] = tail
    xs_ref[0:hist, :] = tail
    ho_ref[...] = h_ref[...]
    for cp in shift_copies:
        cp.wait()


def _lru_call(x, conv0, h0, gn, w_lru, cw, cb, wa, ba, wx, bx, lam, wp, *, tt, caches=(), fresh=(), n_cache_batch=1,
              q_dec=None, relb=None):
    B, S, _ = x.shape
    assert S % tt == 0 and tt >= CONV_WIDTH - 1 and tt % 8 == 0 and B % 8 == 0
    R = B * tt
    hist = (CONV_WIDTH - 1) * B
    steps = S // tt
    spb = steps // n_cache_batch
    const = lambda shape: pl.BlockSpec(shape, lambda i: (0,) * len(shape), pipeline_mode=pl.Buffered(1))
    shift_in, shift_out, shift_shapes, batch_rows = [], [], [], []
    for c, f in zip(caches, fresh):
        total = c.shape[0]
        rpb = total // steps
        rows_b = total // n_cache_batch
        hop = f.shape[0] // n_cache_batch
        assert total % steps == 0 and rows_b == spb * rpb and rpb % hop == 0 and rpb > hop
        shift_in += [
            pl.BlockSpec((rpb, HEAD_DIM), lambda i: (i, 0)),
            pl.BlockSpec((hop, HEAD_DIM), lambda i, rpb=rpb, hop=hop, total=total:
                         (jnp.minimum((i + 1) * (rpb // hop), total // hop - 1), 0)),
            pl.BlockSpec((hop, HEAD_DIM), lambda i, rpb=rpb, rows_b=rows_b: ((i * rpb) // rows_b, 0)),
        ]
        shift_out.append(pl.BlockSpec(memory_space=pl.ANY))
        shift_shapes.append(jax.ShapeDtypeStruct(c.shape, c.dtype))
        batch_rows.append(rows_b)
    shift_args = [a for c, f in zip(caches, fresh) for a in (c, c, f)]
    dec_in, dec_out, dec_shapes, dec_scratch = [], [], [], []
    if caches:
        assert len(caches) == N_GROUPS and q_dec.shape[0] % n_cache_batch == 0
        T = q_dec.shape[0] // n_cache_batch
        part = jax.ShapeDtypeStruct((steps * T, D_QKV), F32)
        dec_in = [pl.BlockSpec((T, D_QKV), lambda i: (i // spb, 0)), pl.BlockSpec(memory_space=pltpu.SMEM)]
        dec_out = [pl.BlockSpec((T, D_QKV), lambda i: (i, 0))] * 2
        dec_shapes = [part, part]
        shift_args += [q_dec, relb]
        assert T % 8 == 0
        widths = [_decode_keys(c.shape[0] // steps // KV_ROWS, d, T) + HEAD_DIM for c, d in zip(caches, DILATIONS)]
        for w in widths:
            dec_scratch += [pltpu.VMEM((w, D_GRP), BF16)] * 2
        dec_scratch += [pltpu.VMEM((spb, HEADS_PER_GROUP * T, w), F32) for w in widths]
        dec_scratch += [pltpu.VMEM((f.shape[0] // n_cache_batch, HEAD_DIM), F32) for f in fresh]
        dec_scratch += [pltpu.SemaphoreType.DMA((2 * len(caches),))]
    return pl.pallas_call(
        functools.partial(_lru_body, B=B, tt=tt, batch_rows=tuple(batch_rows), spb=spb),
        grid=(steps,),
        in_specs=[
            pl.BlockSpec((B, tt, D_MODEL), lambda i: (0, i, 0)),
            const((1, D_MODEL)), const((D_MODEL, 2 * D_LRU)), const((CONV_WIDTH, D_LRU)), const((1, D_LRU)),
            const((N_LRU_BLOCKS, LRU_BLOCK, LRU_BLOCK)), const((1, D_LRU)),
            const((N_LRU_BLOCKS, LRU_BLOCK, LRU_BLOCK)), const((1, D_LRU)), const((1, D_LRU)),
            const((D_LRU, D_MODEL)), const((hist, D_LRU)), const((B, D_LRU)),
        ] + shift_in + dec_in,
        out_specs=[
            pl.BlockSpec((B, tt, D_MODEL), lambda i: (0, i, 0)),
            pl.BlockSpec((hist, D_LRU), lambda i: (0, 0)),
            pl.BlockSpec((B, D_LRU), lambda i: (0, 0)),
        ] + shift_out + dec_out,
        out_shape=[
            jax.ShapeDtypeStruct((B, S, D_MODEL), F32),
            jax.ShapeDtypeStruct((hist, D_LRU), F32),
            jax.ShapeDtypeStruct((B, D_LRU), F32),
        ] + shift_shapes + dec_shapes,
        scratch_shapes=[
            pltpu.VMEM((hist + R, D_LRU), F32),
            pltpu.VMEM((R, D_LRU), F32),
            pltpu.VMEM((R, D_LRU), F32),
            pltpu.VMEM((R, D_LRU), F32),
            pltpu.VMEM((B, D_LRU), F32),
        ] + dec_scratch,
        compiler_params=pltpu.CompilerParams(
            dimension_semantics=("arbitrary",), vmem_limit_bytes=VMEM_LIMIT,
            allow_input_fusion=[i in (2, 5, 7, 10) for i in range(N_LRU_IN + len(shift_args))]),
        name="lru_branch",
    )(x, gn, w_lru, cw, cb, wa, ba, wx, bx, lam, wp, conv0, h0, *shift_args)


def _attn_body(x_ref, ya_ref, gn_ref, wqkv_ref, wg_ref, gq_ref, gk_ref, bm_ref, relb_ref, wap_ref, wout_ref,
               y_ref, kv0_ref, kv1_ref, kv2_ref,
               zq_ref, zk_ref, zv_ref, qc0, qc1, qc2, hk0, hv0, hk1, hv1, hk2, hv2,
               o0, o1, o2, l0, l1, l2, bt0, bt1, bt2, s_ref, p_ref, *, S, tq):
    t = pl.program_id(1)
    nt = S // tq

    qcs, hks, hvs = (qc0, qc1, qc2), (hk0, hk1, hk2), (hv0, hv1, hv2)
    o_refs, l_refs, bts, kvs = (o0, o1, o2), (l0, l1, l2), (bt0, bt1, bt2), (kv0_ref, kv1_ref, kv2_ref)
    Qc = tuple(tq // d for d in DILATIONS)
    QB = tuple(min(q, N_BACK) for q in Qc)
    pad1 = KEY_WIN - QB[1]
    n2 = S // DILATIONS[2]

    @pl.when((pl.program_id(0) == 0) & (t == 0))
    def _():
        for ref in (hk0, hv0, hk1, hv1, hk2, hv2):
            ref[...] = jnp.zeros(ref.shape, ref.dtype)
        for g in (0, 1):
            a = lax.broadcasted_iota(jnp.int32, (QB[g], KEY_WIN), 0)
            c = lax.broadcasted_iota(jnp.int32, (QB[g], KEY_WIN), 1)
            j = a + (KEY_WIN - QB[g]) - c
            for h in range(HEADS_PER_GROUP):
                bts[g][h] = _bias_table(relb_ref, g * HEADS_PER_GROUP + h, j * DILATIONS[g], (j >= 0) & (j <= N_BACK))
        i = lax.broadcasted_iota(jnp.int32, (n2, n2), 0)
        c = lax.broadcasted_iota(jnp.int32, (n2, n2), 1)
        for h in range(HEADS_PER_GROUP):
            bt2[h] = _bias_table(relb_ref, 2 * HEADS_PER_GROUP + h, (i - c) * DILATIONS[2],
                                 (i - c >= 0) & (i - c <= N_BACK))

    x = x_ref[...]
    u = _rms(x, gn_ref[...]).astype(BF16)
    z_refs = (zq_ref, zk_ref, zv_ref)

    def store(part, head, value):
        z_refs[part][head] = value

    _qkv(u, wqkv_ref, gq_ref, gk_ref, store)

    for g in range(N_GROUPS):
        keep = min(WINDOWS[g], S)
        rb = min(tq, keep)

        @pl.when(t >= nt - keep // rb)
        def _(g=g, rb=rb):
            for part, z_ref in enumerate((zk_ref, zv_ref)):
                for h in range(HEADS_PER_GROUP):
                    dst = pl.ds(part * HEADS_PER_GROUP + h, rb, stride=KV_ROWS)
                    kvs[g][dst, :] = z_ref[g * HEADS_PER_GROUP + h, tq - rb:tq, :]

    for g in range(N_GROUPS):
        d = DILATIONS[g]
        for r in range(d):
            rows = pl.ds(r, Qc[g], stride=d) if d > 1 else slice(None)
            if g == 0:
                dst = pl.ds(tq, tq)
            elif g == 1:
                dst = pl.ds(pl.multiple_of(pad1 + t * Qc[1], Qc[1]), Qc[1])
            else:
                dst = pl.ds(pl.multiple_of(t * Qc[2], Qc[2]), Qc[2])
            for h in range(HEADS_PER_GROUP):
                hs_ = slice(h * HEAD_DIM, (h + 1) * HEAD_DIM)
                gh = g * HEADS_PER_GROUP + h
                qcs[g][r, :, hs_] = zq_ref[gh, rows, :].astype(BF16)
                hks[g][r, dst, hs_] = zk_ref[gh, rows, :].astype(BF16)
                hvs[g][r, dst, hs_] = zv_ref[gh, rows, :].astype(BF16)

    col = lax.broadcasted_iota(jnp.int32, (1, KEY_WIN), 1)
    rowi = lax.broadcasted_iota(jnp.int32, (tq, 1), 0)
    for g in range(N_GROUPS):
        d = DILATIONS[g]
        nq = Qc[g] // QB[g]
        kw = n2 if g == 2 else KEY_WIN
        units = [(r, qb) for r in range(d) for qb in range(nq)]

        def key_win(qb, g=g):
            if g == 0:
                return pl.ds(tq + qb * QB[0] + QB[0] - KEY_WIN, KEY_WIN)
            if g == 1:
                return pl.ds(pl.multiple_of(t * Qc[1], QB[1]), KEY_WIN)
            return slice(None)

        for r, qb in units:
            cm = slice(r * Qc[g] + qb * QB[g], r * Qc[g] + (qb + 1) * QB[g])
            for h in range(HEADS_PER_GROUP):
                hs_ = slice(h * HEAD_DIM, (h + 1) * HEAD_DIM)
                s_ref[h, cm, 0:kw] = _dot_nt(qcs[g][r, qb * QB[g]:(qb + 1) * QB[g], hs_], hks[g][r, key_win(qb), hs_])

        if g < 2:
            i0 = t * Qc[g] + (rowi & (Qc[g] - QB[g]))
            started = col >= KEY_WIN - QB[g] - i0
        for h in range(HEADS_PER_GROUP):
            bias = bt2[h, pl.ds(pl.multiple_of(t * QB[2], QB[2]), QB[2]), :] if g == 2 else bts[g][h]
            s = s_ref[h, :, 0:kw] * ATTN_SCALE
            s = (s.reshape(len(units), QB[g], kw) + bias[None]).reshape(tq, kw)
            if g < 2:
                s = jnp.where(started, s, NEG)
            m = jnp.max(s, axis=-1, keepdims=True)
            p = jnp.exp(s - m)
            den = jnp.sum(p, axis=-1, keepdims=True)
            p_ref[h, :, 0:kw] = (p * (1.0 / den)).astype(BF16)
            lse = jnp.broadcast_to(m + jnp.log(den), (tq, HEAD_DIM))
            for r in range(d):
                rows = pl.ds(r, Qc[g], stride=d) if d > 1 else slice(None)
                l_refs[g][h, rows, :] = lse[r * Qc[g]:(r + 1) * Qc[g]]

        for r, qb in units:
            cm = slice(r * Qc[g] + qb * QB[g], r * Qc[g] + (qb + 1) * QB[g])
            rows = pl.ds(r + d * qb * QB[g], QB[g], stride=d) if d > 1 else pl.ds(qb * QB[g], QB[g])
            for h in range(HEADS_PER_GROUP):
                hs_ = slice(h * HEAD_DIM, (h + 1) * HEAD_DIM)
                o_refs[g][h, rows, :] = _dot(p_ref[h, cm, 0:kw], hvs[g][r, key_win(qb), hs_])

    hk0[0, 0:tq, :] = hk0[0, tq:2 * tq, :]
    hv0[0, 0:tq, :] = hv0[0, tq:2 * tq, :]

    ob = _merge_groups(o_refs, l_refs)
    ob = jnp.concatenate([ob[h] for h in range(HEADS_PER_GROUP)], axis=1)
    gate = jnp.concatenate([_gate_chunk(u, wg_ref, bm_ref, c) for c in range(N_GATE_CHUNKS)], axis=1)
    y_ref[...] = _tail(x, ya_ref[...].astype(F32), ob, gate, wap_ref, wout_ref)


def _attn_call(x, ya, gn, wqkv, wg, gq, gk, bmerge, relb, wap, wout, *, tq):
    B, S, _ = x.shape
    nt = S // tq
    assert S % tq == 0 and tq % (DILATIONS[2] * 16) == 0 and tq >= N_BACK
    assert S // DILATIONS[2] == N_BACK and tq // DILATIONS[1] <= N_BACK
    Qc = tuple(tq // d for d in DILATIONS)
    QB = tuple(min(q, N_BACK) for q in Qc)
    n2 = S // DILATIONS[2]
    const = lambda shape: pl.BlockSpec(shape, lambda b, t: (0,) * len(shape), pipeline_mode=pl.Buffered(1))
    row = pl.BlockSpec((None, tq, D_MODEL), lambda b, t: (b, t, 0))

    def kv_spec(g):
        keep = min(WINDOWS[g], S)
        rb = min(tq, keep)
        first = nt - keep // rb
        return pl.BlockSpec((None, rb * KV_ROWS, HEAD_DIM), lambda b, t: (b, jnp.maximum(t - first, 0), 0))

    cls = lambda g, rows, dt: pltpu.VMEM((DILATIONS[g], rows, D_GRP), dt)
    scratch = [pltpu.VMEM((D_QKV // HEAD_DIM, tq, HEAD_DIM), F32)] * 3
    scratch += [cls(g, Qc[g], BF16) for g in range(N_GROUPS)]
    scratch += [cls(0, 2 * tq, BF16)] * 2 + [cls(1, KEY_WIN - QB[1] + S // DILATIONS[1], BF16)] * 2 + [cls(2, n2, BF16)] * 2
    scratch += [pltpu.VMEM((HEADS_PER_GROUP, tq, HEAD_DIM), F32)] * 6
    scratch += [pltpu.VMEM((HEADS_PER_GROUP, QB[0], KEY_WIN), F32), pltpu.VMEM((HEADS_PER_GROUP, QB[1], KEY_WIN), F32),
                pltpu.VMEM((HEADS_PER_GROUP, n2, n2), F32)]
    scratch += [pltpu.VMEM((HEADS_PER_GROUP, tq, KEY_WIN), F32), pltpu.VMEM((HEADS_PER_GROUP, tq, KEY_WIN), BF16)]
    return pl.pallas_call(
        functools.partial(_attn_body, S=S, tq=tq),
        grid=(B, nt),
        in_specs=[row, row, const((1, D_MODEL)), const((D_MODEL, 3 * D_QKV)), const((D_MODEL, D_GRP + 2 * D_MODEL)),
                  const((1, HEAD_DIM)), const((1, HEAD_DIM)), const((1, 2 * D_MODEL)),
                  pl.BlockSpec(memory_space=pltpu.SMEM), const((D_GRP, D_MODEL)), const((D_MODEL, D_MODEL))],
        out_specs=[row, kv_spec(0), kv_spec(1), kv_spec(2)],
        out_shape=[jax.ShapeDtypeStruct((B, S, D_MODEL), F32)]
        + [jax.ShapeDtypeStruct((B, min(WINDOWS[g], S) * KV_ROWS, HEAD_DIM), F32) for g in range(N_GROUPS)],
        scratch_shapes=scratch,
        compiler_params=pltpu.CompilerParams(dimension_semantics=("arbitrary", "arbitrary"), vmem_limit_bytes=VMEM_LIMIT,
                                             allow_input_fusion=[i in (3, 4, 9, 10) for i in range(11)]),
        name="attn_branch",
    )(x, ya, gn, wqkv, wg, gq, gk, bmerge, relb, wap, wout)


def _qkv_body(x_ref, gn_ref, wqkv_ref, gq_ref, gk_ref, zq_ref, nr0_ref, nr1_ref, nr2_ref):
    M = x_ref.shape[0]
    u = _rms(x_ref[...], gn_ref[...]).astype(BF16)
    nr_refs = (nr0_ref, nr1_ref, nr2_ref)

    def store(part, head, value):
        if part == 0:
            zq_ref[:, head * HEAD_DIM:(head + 1) * HEAD_DIM] = value
        else:
            g, h = divmod(head, HEADS_PER_GROUP)
            nr_refs[g][pl.ds((part - 1) * HEADS_PER_GROUP + h, M, stride=KV_ROWS), :] = value

    _qkv(u, wqkv_ref, gq_ref, gk_ref, store)


def _qkv_call(x, gn, wqkv, gq, gk):
    M = x.shape[0]
    rows = jax.ShapeDtypeStruct((M * KV_ROWS, HEAD_DIM), F32)
    return pl.pallas_call(_qkv_body, out_shape=[jax.ShapeDtypeStruct((M, D_QKV), F32), rows, rows, rows],
                          compiler_params=pltpu.CompilerParams(vmem_limit_bytes=VMEM_LIMIT),
                          name="decode_qkv")(x, gn, wqkv, gq, gk)


def _tail_body(x_ref, ya_ref, gn_ref, wg_ref, bm_ref, wap_ref, wout_ref, o_ref, lse_ref, y_ref):
    x = x_ref[...]
    u = _rms(x, gn_ref[...]).astype(BF16)
    pieces = [(j, slice(g * D_GRP, (g + 1) * D_GRP)) for j in range(o_ref.shape[0]) for g in range(N_GROUPS)]
    ob = _merge_groups([o_ref.at[j, :, gs] for j, gs in pieces], [lse_ref.at[j, :, gs] for j, gs in pieces])
    gate = jnp.concatenate([_gate_chunk(u, wg_ref, bm_ref, c) for c in range(N_GATE_CHUNKS)], axis=1)
    y_ref[...] = _tail(x, ya_ref[...], ob, gate, wap_ref, wout_ref)


def _tail_call(x, ya, gn, wg, bmerge, wap, wout, o, lse):
    return pl.pallas_call(_tail_body, out_shape=jax.ShapeDtypeStruct(x.shape, F32),
                          compiler_params=pltpu.CompilerParams(vmem_limit_bytes=VMEM_LIMIT),
                          name="decode_tail")(x, ya, gn, wg, bmerge, wap, wout, o, lse)


def kernel(x_prompt, x_sample, cache_kv_w128, cache_kv_w512, cache_kv_w2048, state_conv, state_h, g_norm, w_in,
           b_merge, conv_w, conv_b, lru_w_a, lru_b_a, lru_w_x, lru_b_x, lru_lambda, g_q, g_k, rel_bias,
           w_lru_proj, w_attn_proj, w_out):
    assert w_in.shape[0] == 1, "single-layer step"
    B, S, _ = x_prompt.shape
    DB, T, _ = x_sample.shape
    o2 = 2 * D_LRU
    o5 = o2 + 3 * D_QKV
    w_lru = w_in[0, :, :o2].astype(BF16)
    wqkv = w_in[0, :, o2:o5].astype(BF16)
    wg = w_in[0, :, o5:].astype(BF16)
    wa, wx = lru_w_a[0].astype(BF16), lru_w_x[0].astype(BF16)
    wp, wap, wout = w_lru_proj[0].astype(BF16), w_attn_proj[0].astype(BF16), w_out[0].astype(BF16)
    lru_params = (g_norm, w_lru, conv_w[0], conv_b, wa, lru_b_a, wx, lru_b_x, lru_lambda, wp)
    hist = CONV_WIDTH - 1

    caches6 = (cache_kv_w128, cache_kv_w512, cache_kv_w2048)
    caches = [c.reshape(-1, HEAD_DIM) for c in caches6]

    xs2 = x_sample.reshape(DB * T, D_MODEL)
    zq, *newrows = _qkv_call(xs2, g_norm, wqkv, g_q, g_k)

    ya_p, conv_p, h_p, *rest = _lru_call(x_prompt, jnp.zeros((hist * B, D_LRU), F32), jnp.zeros((B, D_LRU), F32),
                                         *lru_params, tt=LRU_TIME_TILE, caches=caches, fresh=newrows,
                                         n_cache_batch=DB, q_dec=zq, relb=rel_bias)
    news, (o_parts, lse_parts) = rest[:N_GROUPS], rest[N_GROUPS:]
    y_p, kv0_p, kv1_p, kv2_p = _attn_call(x_prompt, ya_p, g_norm, wqkv, wg, g_q, g_k, b_merge, rel_bias, wap, wout,
                                          tq=ATTN_ROW_TILE)

    conv0_s = jnp.swapaxes(state_conv[0], 0, 1).reshape(hist * DB, D_LRU)
    ya_s, conv_s, h_s = _lru_call(x_sample, conv0_s, state_h[0], *lru_params, tt=T)
    spb = S // LRU_TIME_TILE // DB
    by_part = lambda a: jnp.swapaxes(a.reshape(DB, spb, T, D_QKV), 0, 1).reshape(spb, DB * T, D_QKV)
    y_s = _tail_call(xs2, ya_s.reshape(DB * T, D_MODEL), g_norm, wg, b_merge, wap, wout,
                     by_part(o_parts), by_part(lse_parts))

    kv_shape = lambda a: a.reshape(1, a.shape[0], a.shape[1] // KV_ROWS, 2, HEADS_PER_GROUP, HEAD_DIM)
    conv_out = lambda c, nb: jnp.swapaxes(c.reshape(hist, nb, D_LRU), 0, 1)[None]
    news = [n.reshape(c.shape) for n, c in zip(news, caches6)]
    return (y_p, y_s.reshape(DB, T, D_MODEL), kv_shape(kv0_p), kv_shape(kv1_p), kv_shape(kv2_p),
            conv_out(conv_p, B), h_p[None], news[0], news[1], news[2], conv_out(conv_s, DB), h_s[None])
```
